```python
import math
import jax, jax.numpy as jnp
from jax import lax
import numpy as np

D_MODEL = 1024
BATCH = 2
SEQ = 16384
DEPTH = 4

GRID_W = 64
CTX_LEN = 256
N_MIXERS = 2
EXPAND_W = 1536
RG_BLOCKS = 16
RG_BLOCK_W = EXPAND_W // RG_BLOCKS
RG_CONV_W = 4
RG_CONV_PAD = (1, 2)
RG_C = 8.0
RG_A_MIN = 0.9
RG_A_MAX = 0.999
HY_ORDER = 2
HY_CONV_W = 3
HY_CONV_PAD = (1, 1)
HY_EMB_BANDS = 8
HY_EMB_DIM = 1 + 2 * HY_EMB_BANDS
HY_FILTER_HIDDEN = 64
HY_FAST_DECAY_PCT = 0.3
HY_SLOW_DECAY_PCT = 1.5
HY_DECAY_TARGET = 1e-2
HY_MOD_SHIFT = 0.05
LN_EPS = 1e-5
DEEPNORM_ALPHA = (2 * DEPTH) ** 0.25
DEEPNORM_BETA = (8 * DEPTH) ** -0.25

kernel_name = "hybrid_rglru_hyena_diffusion_trunk"


def _layer_norm(x, g, b):
    xf = x.astype(jnp.float32)
    mu = jnp.mean(xf, axis=-1, keepdims=True)
    var = jnp.mean(jnp.square(xf - mu), axis=-1, keepdims=True)
    return ((xf - mu) * lax.rsqrt(var + LN_EPS) * g + b).astype(x.dtype)


def _dwconv(x, w, b, pad):
    y = lax.conv_general_dilated(
        x, w[:, None, :].astype(x.dtype), window_strides=(1,), padding=[pad],
        dimension_numbers=("NWC", "WIO", "NWC"), feature_group_count=x.shape[-1])
    return y + b


def _to_col_major(x, rows):
    b, s, d = x.shape
    return x.reshape(b, rows, GRID_W, d).transpose(0, 2, 1, 3).reshape(b, s, d)


def _to_row_major(x, rows):
    b, s, d = x.shape
    return x.reshape(b, GRID_W, rows, d).transpose(0, 2, 1, 3).reshape(b, s, d)


def _linear_scan(a, bx, h0, reverse):
    edge = -1 if reverse else 0
    bx = bx.at[:, edge].add(a[:, edge] * h0)

    def combine(left, right):
        a_l, b_l = left
        a_r, b_r = right
        return a_l * a_r, a_r * b_l + b_r

    _, h = lax.associative_scan(combine, (a, bx), reverse=reverse, axis=1)
    return h


def _rglru_coeffs(xc, w_r, b_r, w_i, b_i, lam):
    bsz, L, _ = xc.shape
    xb = xc.reshape(bsz, L, RG_BLOCKS, RG_BLOCK_W)
    r = jax.nn.sigmoid(jnp.einsum("blhi,hij->blhj", xb, w_r).reshape(bsz, L, EXPAND_W) + b_r)
    i = jax.nn.sigmoid(jnp.einsum("blhi,hij->blhj", xb, w_i).reshape(bsz, L, EXPAND_W) + b_i)
    log_a = (-RG_C * r * jax.nn.softplus(-lam)).astype(jnp.float32)
    a = jnp.exp(log_a)
    mult = jnp.sqrt(-jnp.expm1(2.0 * log_a))
    return a, (mult * (i * xc)).astype(jnp.float32)


def _rglru_mixer(h_lat, h_ctx, w_in, conv_w, conv_b, w_r, b_r, w_i, b_i, lam, w_out, need_ctx_out):
    E = EXPAND_W
    dt = h_lat.dtype
    ug_l = h_lat @ w_in
    u_l, g_l = ug_l[..., :E], ug_l[..., E:]
    xc_l = _dwconv(u_l, conv_w, conv_b, RG_CONV_PAD).astype(jnp.float32)
    if need_ctx_out:
        ug_c = h_ctx @ w_in
        u_c, g_c = ug_c[..., :E], ug_c[..., E:]
    else:
        u_c = h_ctx @ w_in[:, :E]
    xc_c = _dwconv(u_c, conv_w, conv_b, RG_CONV_PAD).astype(jnp.float32)
    zeros = jnp.zeros((xc_c.shape[0], E), jnp.float32)
    outs_l, outs_c = [], []
    for d, reverse in enumerate((False, True)):
        a_c, bx_c = _rglru_coeffs(xc_c, w_r[d], b_r[d], w_i[d], b_i[d], lam[d])
        hs_c = _linear_scan(a_c, bx_c, zeros, reverse)
        h_end = hs_c[:, 0] if reverse else hs_c[:, -1]
        a_l, bx_l = _rglru_coeffs(xc_l, w_r[d], b_r[d], w_i[d], b_i[d], lam[d])
        outs_l.append(_linear_scan(a_l, bx_l, h_end, reverse))
        outs_c.append(hs_c)
    y_l = outs_l[0] + outs_l[1]
    out_l = (y_l.astype(dt) * jax.nn.silu(g_l)) @ w_out
    if not need_ctx_out:
        return out_l, None
    y_c = outs_c[0] + outs_c[1]
    out_c = (y_c.astype(h_ctx.dtype) * jax.nn.silu(g_c)) @ w_out
    return out_l, out_c


def _hyena_filters(L, w1, b1, w2, b2, w3, freq):
    f32 = jnp.float32
    t = jnp.linspace(0.0, 1.0, L, dtype=f32)[:, None]
    bands = jnp.linspace(1e-4, HY_EMB_BANDS - 1, HY_EMB_BANDS, dtype=f32)
    w = (2.0 * math.pi) * jnp.arange(L, dtype=f32)[:, None] / L
    z = jnp.concatenate([t, jnp.cos(bands * w), -jnp.sin(bands * w)], axis=-1)
    fr = freq.astype(f32)
    hid = jnp.sin(fr * (z @ w1.astype(f32) + b1.astype(f32)))
    hid = jnp.sin(fr * (hid @ w2.astype(f32) + b2.astype(f32)))
    h = (hid @ w3.astype(f32)).reshape(L, HY_ORDER, 2, EXPAND_W)
    max_decay = math.log(HY_DECAY_TARGET) / HY_FAST_DECAY_PCT
    min_decay = math.log(HY_DECAY_TARGET) / HY_SLOW_DECAY_PCT
    deltas = jnp.linspace(min_decay, max_decay, EXPAND_W, dtype=f32)
    window = jnp.exp(-t * jnp.abs(deltas)) + HY_MOD_SHIFT
    h = h * window[:, None, None, :]
    fwd = h[:, :, 0]
    bwd = h[1:, :, 1]
    gap = jnp.zeros((1, HY_ORDER, EXPAND_W), f32)
    k = jnp.concatenate([fwd, gap, bwd[::-1]], axis=0)
    k = k / jnp.sum(jnp.abs(k), axis=0, keepdims=True)
    return jnp.fft.rfft(k, axis=0)


def _fftconv(u, k_f, d_bias):
    L = u.shape[1]
    u_f = jnp.fft.rfft(u, n=2 * L, axis=1)
    y = jnp.fft.irfft(u_f * k_f, n=2 * L, axis=1)[:, :L]
    return y + u * d_bias


def _hyena_mixer(h, w_in, conv_w, conv_b, w1, b1, w2, b2, w3, freq, d_bias, w_out):
    E = EXPAND_W
    dt = h.dtype
    L = h.shape[1]
    proj = h @ w_in
    vx = _dwconv(proj[..., :3 * E], conv_w, conv_b, HY_CONV_PAD).astype(jnp.float32)
    g = proj[..., 3 * E:]
    v, x1, x2 = vx[..., :E], vx[..., E:2 * E], vx[..., 2 * E:]
    k_f = _hyena_filters(L, w1, b1, w2, b2, w3, freq)
    dd = d_bias.astype(jnp.float32)
    zz = x1 * _fftconv(v, k_f[:, 0], dd[0])
    y = x2 * _fftconv(zz, k_f[:, 1], dd[1])
    return (y.astype(dt) * jax.nn.silu(g)) @ w_out


def setup_inputs(seed: int = 0) -> dict:
    key = jax.random.key(seed)
    keys = iter(jax.random.split(key, 64))

    def nrm(shape, scale):
        return jax.random.normal(next(keys), shape, jnp.float32) * scale

    D, E, FH = D_MODEL, EXPAND_W, HY_FILTER_HIDDEN
    n_a = len(range(0, DEPTH, N_MIXERS))
    n_b = len(range(1, DEPTH, N_MIXERS))
    a_pow = jax.random.uniform(next(keys), (n_a, 2, E), jnp.float32, RG_A_MIN, RG_A_MAX)
    a_base = a_pow ** (1.0 / RG_C)
    rg_lambda = jnp.log(a_base) - jnp.log1p(-a_base)
    return {
        "x": nrm((BATCH, SEQ, D), 1.0),
        "c": nrm((BATCH, D), 1.0),
        "ctx": nrm((BATCH, CTX_LEN, D), 1.0),
        "c_ctx": nrm((D,), 1.0),
        "w_mod": nrm((DEPTH, D, 3 * D), 0.5 * D ** -0.5),
        "b_mod": nrm((DEPTH, 3 * D), 0.02),
        "ln_g": 1.0 + nrm((DEPTH, D), 0.02),
        "ln_b": nrm((DEPTH, D), 0.02),
        "rg_w_in": nrm((n_a, D, 2 * E), D ** -0.5),
        "rg_conv_w": nrm((n_a, RG_CONV_W, E), RG_CONV_W ** -0.5),
        "rg_conv_b": nrm((n_a, E), 0.02),
        "rg_w_r": nrm((n_a, 2, RG_BLOCKS, RG_BLOCK_W, RG_BLOCK_W), RG_BLOCK_W ** -0.5),
        "rg_b_r": nrm((n_a, 2, E), 0.1),
        "rg_w_i": nrm((n_a, 2, RG_BLOCKS, RG_BLOCK_W, RG_BLOCK_W), RG_BLOCK_W ** -0.5),
        "rg_b_i": nrm((n_a, 2, E), 0.1),
        "rg_lambda": rg_lambda,
        "rg_w_out": nrm((n_a, E, D), DEEPNORM_BETA * E ** -0.5),
        "hy_w_in": nrm((n_b, D, 4 * E), D ** -0.5),
        "hy_conv_w": nrm((n_b, HY_CONV_W, 3 * E), HY_CONV_W ** -0.5),
        "hy_conv_b": nrm((n_b, 3 * E), 0.02),
        "hy_f_w1": nrm((n_b, HY_EMB_DIM, FH), HY_EMB_DIM ** -0.5),
        "hy_f_b1": nrm((n_b, FH), 0.1),
        "hy_f_w2": nrm((n_b, FH, FH), FH ** -0.5),
        "hy_f_b2": nrm((n_b, FH), 0.1),
        "hy_f_w3": nrm((n_b, FH, HY_ORDER * 2 * E), FH ** -0.5),
        "hy_f_freq": 1.0 + nrm((n_b, FH), 0.1),
        "hy_d": nrm((n_b, HY_ORDER, E), 1.0),
        "hy_w_out": nrm((n_b, E, D), DEEPNORM_BETA * E ** -0.5),
    }


def reference(x, c, ctx, c_ctx, w_mod, b_mod, ln_g, ln_b,
              rg_w_in, rg_conv_w, rg_conv_b, rg_w_r, rg_b_r, rg_w_i, rg_b_i, rg_lambda, rg_w_out,
              hy_w_in, hy_conv_w, hy_conv_b, hy_f_w1, hy_f_b1, hy_f_w2, hy_f_b2, hy_f_w3, hy_f_freq,
              hy_d, hy_w_out):
    rows = x.shape[1] // GRID_W
    act_lat = jax.nn.silu(c)
    act_ctx = jax.nn.silu(c_ctx)
    for i in range(DEPTH):
        kind = i % N_MIXERS
        occ = i // N_MIXERS
        need_ctx_out = any(j % N_MIXERS == 0 for j in range(i + 1, DEPTH))
        shift, scale, gate = jnp.split(act_lat @ w_mod[i] + b_mod[i], 3, axis=-1)
        h = x * (1.0 + scale[:, None, :]) + shift[:, None, :]
        col_major = occ % 2 == 1
        if col_major:
            h = _to_col_major(h, rows)
        if kind == 0 or need_ctx_out:
            shift_c, scale_c, gate_c = jnp.split(act_ctx @ w_mod[i] + b_mod[i], 3, axis=-1)
            hc = ctx * (1.0 + scale_c) + shift_c
        if kind == 0:
            y, yc = _rglru_mixer(h, hc, rg_w_in[occ], rg_conv_w[occ], rg_conv_b[occ], rg_w_r[occ],
                                 rg_b_r[occ], rg_w_i[occ], rg_b_i[occ], rg_lambda[occ], rg_w_out[occ],
                                 need_ctx_out)
        else:
            hy_args = (hy_w_in[occ], hy_conv_w[occ], hy_conv_b[occ], hy_f_w1[occ], hy_f_b1[occ],
                       hy_f_w2[occ], hy_f_b2[occ], hy_f_w3[occ], hy_f_freq[occ], hy_d[occ], hy_w_out[occ])
            y = _hyena_mixer(h, *hy_args)
            yc = _hyena_mixer(hc, *hy_args) if need_ctx_out else None
        if col_major:
            y = _to_row_major(y, rows)
        x = _layer_norm(DEEPNORM_ALPHA * x + gate[:, None, :] * y, ln_g[i], ln_b[i])
        if need_ctx_out:
            ctx = _layer_norm(DEEPNORM_ALPHA * ctx + gate_c * yc, ln_g[i], ln_b[i])
    return x
```

```python
import functools
import math

import numpy as np
import jax
import jax.numpy as jnp
from jax import lax
from jax.experimental import pallas as pl
from jax.experimental.pallas import tpu as pltpu

GRID_W = 64
N_MIXERS = 2
RG_C = 8.0
RG_GATE_GROUP = 4
HY_EMB_BANDS = 8
HY_FAST_DECAY_PCT = 0.3
HY_SLOW_DECAY_PCT = 1.5
HY_DECAY_TARGET = 1e-2
HY_MOD_SHIFT = 0.05
LN_EPS = 1e-5

SUBLANES = 8
BF16_ROWS = 16
FFT_N2 = 256
FFT_J = SUBLANES
FFT_EC = 768
VMEM_LIMIT = 56 * 1024 * 1024

_HIGHEST = lax.Precision.HIGHEST


def _round_up(a, m):
    return (a + m - 1) // m * m


def _params(*sem):
    return pltpu.CompilerParams(dimension_semantics=sem, vmem_limit_bytes=VMEM_LIMIT)


def _sigmoid(v):
    return 1.0 / (1.0 + jnp.exp(-v))


def _mod_kernel(c_ref, w_ref, b_ref, o_ref):
    cv = c_ref[...]
    act = cv * _sigmoid(cv)
    o_ref[0] = jnp.dot(act, w_ref[0], preferred_element_type=jnp.float32,
                       precision=_HIGHEST) + b_ref[0]


def _modulation(cvec, w_mod, b_mod):
    depth, d, d3 = w_mod.shape
    nblk = d3 // d
    return pl.pallas_call(
        _mod_kernel,
        grid=(depth, nblk),
        in_specs=[pl.BlockSpec((SUBLANES, d), lambda l, n: (0, 0)),
                  pl.BlockSpec((1, d, d), lambda l, n: (l, 0, n)),
                  pl.BlockSpec((1, 1, d), lambda l, n: (l, 0, n))],
        out_specs=pl.BlockSpec((1, SUBLANES, d), lambda l, n: (l, 0, n)),
        out_shape=jax.ShapeDtypeStruct((depth, SUBLANES, d3), jnp.float32),
        compiler_params=_params("parallel", "parallel"),
        name="modulation",
    )(cvec, w_mod, b_mod.reshape(depth, 1, d3))


def _stream_view(x, col_major):
    b, l, d = x.shape
    if col_major:
        return x.reshape(b, l // GRID_W, GRID_W * d)
    return x


def _stream_specs(l, d, tm, col_major, nb_grid_prefix):
    def wrap(fn):
        return lambda *g: fn(*g[nb_grid_prefix:])
    if col_major:
        rows = l // GRID_W
        assert tm == rows
        main = pl.BlockSpec((1, tm, d), wrap(lambda b, i: (b, 0, i)))
        prev = pl.BlockSpec((1, SUBLANES, d), wrap(lambda b, i: (b, rows // SUBLANES - 1, jnp.maximum(i - 1, 0))))
        nxt = pl.BlockSpec((1, SUBLANES, d), wrap(lambda b, i: (b, 0, jnp.minimum(i + 1, GRID_W - 1))))
    else:
        per = tm // SUBLANES
        last = l // SUBLANES - 1
        main = pl.BlockSpec((1, tm, d), wrap(lambda b, i: (b, i, 0)))
        prev = pl.BlockSpec((1, SUBLANES, d), wrap(lambda b, i: (b, jnp.maximum(i * per - 1, 0), 0)))
        nxt = pl.BlockSpec((1, SUBLANES, d), wrap(lambda b, i: (b, jnp.minimum((i + 1) * per, last), 0)))
    return main, prev, nxt


def _tile_rows(l, col_major):
    if col_major:
        return l // GRID_W
    return min(l, 256)


def _inproj_kernel(x_ref, xp_ref, xn_ref, sh_ref, sc_ref, w_ref, cw_ref, cb_ref, o_ref, u_scr,
                   *, n_conv, conv_k, pad_l, tm):
    n = pl.program_id(0)
    i = pl.program_id(2)
    nt = pl.num_programs(2)
    scale = 1.0 + sc_ref[0]
    shift = sh_ref[0]
    h_main = x_ref[0] * scale + shift

    @pl.when(n < n_conv)
    def _():
        h_prev = xp_ref[0] * scale + shift
        h_next = xn_ref[0] * scale + shift
        hh = jnp.concatenate([h_prev, h_main, h_next], axis=0).astype(jnp.bfloat16)
        u = jnp.dot(hh, w_ref[...], preferred_element_type=jnp.float32)
        row = lax.broadcasted_iota(jnp.int32, u.shape, 0)
        outside = ((row < SUBLANES) & (i == 0)) | ((row >= tm + SUBLANES) & (i == nt - 1))
        u_scr[...] = jnp.where(outside, 0.0, u)
        acc = cb_ref[...] + cw_ref[0:1, :] * u_scr[pl.ds(SUBLANES - pad_l, tm), :]
        for k in range(1, conv_k):
            acc = acc + cw_ref[k:k + 1, :] * u_scr[pl.ds(SUBLANES - pad_l + k, tm), :]
        o_ref[0, 0] = acc

    @pl.when(n >= n_conv)
    def _():
        o_ref[0, 0] = jnp.dot(h_main.astype(jnp.bfloat16), w_ref[...],
                              preferred_element_type=jnp.float32)


def _inproj(x, shift, scale, w_bf16, conv_w, conv_b, *, e, pad_l, col_major):
    b, l, d = x.shape
    nt_groups = w_bf16.shape[1] // e
    conv_k = conv_w.shape[0]
    n_conv = conv_w.shape[1] // e
    tm = _tile_rows(l, col_major)
    ntile = l // tm
    main, prev, nxt = _stream_specs(l, d, tm, col_major, 1)
    xv = _stream_view(x, col_major)
    kern = functools.partial(_inproj_kernel, n_conv=n_conv, conv_k=conv_k, pad_l=pad_l, tm=tm)
    return pl.pallas_call(
        kern,
        grid=(nt_groups, b, ntile),
        in_specs=[main, prev, nxt,
                  pl.BlockSpec((1, 1, d), lambda n, bb, i: (bb, 0, 0)),
                  pl.BlockSpec((1, 1, d), lambda n, bb, i: (bb, 0, 0)),
                  pl.BlockSpec((d, e), lambda n, bb, i: (0, n)),
                  pl.BlockSpec((conv_k, e), lambda n, bb, i: (0, jnp.minimum(n, n_conv - 1))),
                  pl.BlockSpec((1, e), lambda n, bb, i: (0, jnp.minimum(n, n_conv - 1)))],
        out_specs=pl.BlockSpec((1, 1, tm, e), lambda n, bb, i: (n, bb, i, 0)),
        out_shape=jax.ShapeDtypeStruct((nt_groups, b, l, e), jnp.float32),
        scratch_shapes=[pltpu.VMEM((tm + 2 * SUBLANES, e), jnp.float32)],
        compiler_params=_params("parallel", "parallel", "parallel"),
        name="inproj_conv",
    )(xv, xv, xv, shift, scale, w_bf16, conv_w, conv_b.reshape(1, -1))


def _rg_scan_kernel(*refs, reverse, add_prev, write_seq, tm, gw, ngroups):
    it = iter(refs)
    xc_ref = next(it)
    prev_ref = next(it) if add_prev else None
    h0_ref, wg_ref, br_ref, bi_ref, lam_ref = next(it), next(it), next(it), next(it), next(it)
    o_ref = next(it) if write_seq else None
    st_ref = next(it)
    a_scr, b_scr, carry = next(it), next(it), next(it)

    i = pl.program_id(1)
    nchunk = pl.num_programs(1)

    @pl.when(i == 0)
    def _():
        carry[...] = h0_ref[0]

    lam = lam_ref[...]
    neg = -lam
    softplus = jnp.maximum(neg, 0.0) + jnp.log(1.0 + jnp.exp(-jnp.abs(neg)))
    for g in range(ngroups):
        sl = slice(g * gw, (g + 1) * gw)
        xg = xc_ref[0, 0, :, sl]
        pre = jnp.dot(xg.astype(jnp.bfloat16), wg_ref[g], preferred_element_type=jnp.float32)
        r = _sigmoid(pre[:, :gw] + br_ref[:, sl])
        ig = _sigmoid(pre[:, gw:] + bi_ref[:, sl])
        log_a = (-RG_C) * r * softplus[:, sl]
        a = jnp.exp(log_a)
        a_scr[:, sl] = a
        b_scr[:, sl] = jnp.sqrt(-jnp.tanh(log_a) * (a * a + 1.0)) * (ig * xg)

    e = a_scr.shape[1]
    row = lax.broadcasted_iota(jnp.int32, (SUBLANES, e), 0)
    nblk = tm // SUBLANES

    def block(k, c):
        kk = (nblk - 1 - k) if reverse else k
        r0 = pl.multiple_of(kk * SUBLANES, SUBLANES)
        a = a_scr[pl.ds(r0, SUBLANES), :]
        h = b_scr[pl.ds(r0, SUBLANES), :]
        for dist in (1, 2, 4):
            shift = (SUBLANES - dist) if reverse else dist
            valid = (row + dist < SUBLANES) if reverse else (row >= dist)
            a_sh = jnp.where(valid, pltpu.roll(a, shift, 0), 1.0)
            h_sh = jnp.where(valid, pltpu.roll(h, shift, 0), 0.0)
            h = a * h_sh + h
            a = a * a_sh
        h = a * c + h
        if write_seq:
            if add_prev:
                o_ref[0, pl.ds(r0, SUBLANES), :] = h + prev_ref[0, pl.ds(r0, SUBLANES), :]
            else:
                o_ref[0, pl.ds(r0, SUBLANES), :] = h
        return h[0:1, :] if reverse else h[SUBLANES - 1:SUBLANES, :]

    c_fin = lax.fori_loop(0, nblk, block, carry[...])
    carry[...] = c_fin

    @pl.when(i == nchunk - 1)
    def _():
        st_ref[0] = c_fin


def _rg_scan(proj, h0, wg, b_r, b_i, lam, *, reverse, prev=None, write_seq=True):
    _, b, l, e = proj.shape
    tm = min(l, 256)
    nchunk = l // tm
    ngroups, gw, _ = wg.shape
    pos = (lambda i: nchunk - 1 - i) if reverse else (lambda i: i)
    add_prev = prev is not None
    in_specs = [pl.BlockSpec((1, 1, tm, e), lambda bb, i: (0, bb, pos(i), 0))]
    args = [proj]
    if add_prev:
        in_specs.append(pl.BlockSpec((1, tm, e), lambda bb, i: (bb, pos(i), 0)))
        args.append(prev)
    in_specs += [pl.BlockSpec((1, 1, e), lambda bb, i: (bb, 0, 0)),
                 pl.BlockSpec((ngroups, gw, 2 * gw), lambda bb, i: (0, 0, 0)),
                 pl.BlockSpec((1, e), lambda bb, i: (0, 0)),
                 pl.BlockSpec((1, e), lambda bb, i: (0, 0)),
                 pl.BlockSpec((1, e), lambda bb, i: (0, 0))]
    args += [h0, wg, b_r.reshape(1, e), b_i.reshape(1, e), lam.reshape(1, e)]
    out_specs, out_shape = [], []
    if write_seq:
        out_specs.append(pl.BlockSpec((1, tm, e), lambda bb, i: (bb, pos(i), 0)))
        out_shape.append(jax.ShapeDtypeStruct((b, l, e), jnp.float32))
    out_specs.append(pl.BlockSpec((1, 1, e), lambda bb, i: (bb, 0, 0)))
    out_shape.append(jax.ShapeDtypeStruct((b, 1, e), jnp.float32))
    kern = functools.partial(_rg_scan_kernel, reverse=reverse, add_prev=add_prev,
                             write_seq=write_seq, tm=tm, gw=gw, ngroups=ngroups)
    res = pl.pallas_call(
        kern,
        grid=(b, nchunk),
        in_specs=in_specs,
        out_specs=out_specs,
        out_shape=out_shape,
        scratch_shapes=[pltpu.VMEM((tm, e), jnp.float32), pltpu.VMEM((tm, e), jnp.float32),
                        pltpu.VMEM((1, e), jnp.float32)],
        compiler_params=_params("parallel", "arbitrary"),
        name="rg_scan_bwd" if reverse else "rg_scan_fwd",
    )(*args)
    if write_seq:
        return res[0], res[1]
    return None, res[0]


def _gate_weights(w_r, w_i):
    nb, bw, _ = w_r.shape
    g = RG_GATE_GROUP
    ng = nb // g
    eye = jnp.eye(g, dtype=w_r.dtype)

    def bd(w):
        w = w.reshape(ng, g, bw, bw)
        return jnp.einsum("nhij,hk->nhikj", w, eye).reshape(ng, g * bw, g * bw)

    return jnp.concatenate([bd(w_r), bd(w_i)], axis=-1).astype(jnp.bfloat16)


def _outproj_kernel(y_ref, g_ref, w_ref, x_ref, gate_ref, lg_ref, lb_ref, o_ref, *, alpha):
    g = g_ref[0, 0]
    z = (y_ref[0] * (g * _sigmoid(g))).astype(jnp.bfloat16)
    out = jnp.dot(z, w_ref[...], preferred_element_type=jnp.float32)
    r = alpha * x_ref[0] + gate_ref[0] * out
    mu = jnp.mean(r, axis=-1, keepdims=True)
    cen = r - mu
    var = jnp.mean(cen * cen, axis=-1, keepdims=True)
    o_ref[0] = cen * lax.rsqrt(var + LN_EPS) * lg_ref[...] + lb_ref[...]


def _outproj(y, proj, g_group, w_bf16, x, gate, ln_g, ln_b, *, alpha, col_major):
    b, l, e = y.shape
    d = x.shape[-1]
    tm = _tile_rows(l, col_major)
    ntile = l // tm
    main, _, _ = _stream_specs(l, d, tm, col_major, 0)
    xv = _stream_view(x, col_major)
    out = pl.pallas_call(
        functools.partial(_outproj_kernel, alpha=alpha),
        grid=(b, ntile),
        in_specs=[pl.BlockSpec((1, tm, e), lambda bb, i: (bb, i, 0)),
                  pl.BlockSpec((1, 1, tm, e), lambda bb, i: (g_group, bb, i, 0)),
                  pl.BlockSpec((e, d), lambda bb, i: (0, 0)),
                  main,
                  pl.BlockSpec((1, 1, d), lambda bb, i: (bb, 0, 0)),
                  pl.BlockSpec((1, d), lambda bb, i: (0, 0)),
                  pl.BlockSpec((1, d), lambda bb, i: (0, 0))],
        out_specs=main,
        out_shape=jax.ShapeDtypeStruct(xv.shape, jnp.float32),
        compiler_params=_params("parallel", "parallel"),
        name="outproj_ln",
    )(y, proj, w_bf16, xv, gate, ln_g.reshape(1, d), ln_b.reshape(1, d))
    return out.reshape(x.shape)


def _filter_positions(l):
    f32 = jnp.float32
    t = jnp.linspace(0.0, 1.0, l, dtype=f32)[:, None]
    bands = jnp.linspace(1e-4, HY_EMB_BANDS - 1, HY_EMB_BANDS, dtype=f32)
    w = (2.0 * math.pi) * jnp.arange(l, dtype=f32)[:, None] / l
    z = jnp.concatenate([t, jnp.cos(bands * w), -jnp.sin(bands * w)], axis=-1)
    z_rev = jnp.concatenate([z[:1], z[:0:-1]], axis=0)
    zz = jnp.concatenate([z, z_rev], axis=0)
    zz = jnp.pad(zz, ((0, 0), (0, 128 - zz.shape[1])))
    return zz, zz[:, 0:1]


def _hidden_kernel(z_ref, w1_ref, b1_ref, w2_ref, b2_ref, fr_ref, o_ref):
    fr = fr_ref[...]
    h = jnp.dot(z_ref[...], w1_ref[...], preferred_element_type=jnp.float32, precision=_HIGHEST)
    h = jnp.sin(fr * (h + b1_ref[...]))
    h = jnp.dot(h, w2_ref[...], preferred_element_type=jnp.float32, precision=_HIGHEST)
    o_ref[...] = jnp.sin(fr * (h + b2_ref[...]))


def _filter_hidden(zz, w1, b1, w2, b2, freq):
    n, zp = zz.shape
    fh = w1.shape[1]
    tr = min(n, 1024)
    w1p = jnp.pad(w1, ((0, zp - w1.shape[0]), (0, 0)))
    full = lambda shape: pl.BlockSpec(shape, lambda i: (0,) * len(shape))
    return pl.pallas_call(
        _hidden_kernel,
        grid=(n // tr,),
        in_specs=[pl.BlockSpec((tr, zp), lambda i: (i, 0)), full((zp, fh)), full((1, fh)),
                  full((fh, fh)), full((1, fh)), full((1, fh))],
        out_specs=pl.BlockSpec((tr, fh), lambda i: (i, 0)),
        out_shape=jax.ShapeDtypeStruct((n, fh), jnp.float32),
        compiler_params=_params("parallel"),
        name="filter_hidden",
    )(zz, w1p, b1.reshape(1, fh), w2, b2.reshape(1, fh), freq.reshape(1, fh))


def _filter_kernel(hid_ref, t_ref, w3_ref, ad_ref, k_ref, s_ref, *, tr, n_total):
    d = pl.program_id(1)
    i = pl.program_id(2)
    last = (d == pl.num_programs(1) - 1) & (i == pl.num_programs(2) - 1)
    h = jnp.dot(hid_ref[...], w3_ref[...], preferred_element_type=jnp.float32, precision=_HIGHEST)
    window = jnp.exp(-t_ref[...] * ad_ref[...]) + HY_MOD_SHIFT
    row = lax.broadcasted_iota(jnp.int32, h.shape, 0)
    gap = (row == 0) & (d == 1) & (i == 0)
    k = jnp.where(gap, 0.0, h * window)
    k_ref[0] = k

    @pl.when((d == 0) & (i == 0))
    def _():
        s_ref[...] = jnp.zeros_like(s_ref)

    s_ref[0] += jnp.sum(jnp.abs(k).reshape(tr // SUBLANES, SUBLANES, -1), axis=0)

    @pl.when(last)
    def _():
        tot = jnp.sum(s_ref[0], axis=0, keepdims=True)
        s_ref[0] = jnp.broadcast_to(1.0 / (tot * n_total), s_ref.shape[1:])


def _filters(hid, tcol, w3, e):
    n, fh = hid.shape
    l = n // 2
    order = w3.shape[1] // (2 * e)
    tr = min(l, 512)
    nl = l // tr
    max_decay = math.log(HY_DECAY_TARGET) / HY_FAST_DECAY_PCT
    min_decay = math.log(HY_DECAY_TARGET) / HY_SLOW_DECAY_PCT
    absdelta = jnp.abs(jnp.linspace(min_decay, max_decay, e, dtype=jnp.float32)).reshape(1, e)
    return pl.pallas_call(
        functools.partial(_filter_kernel, tr=tr, n_total=float(n)),
        grid=(order, 2, nl),
        in_specs=[pl.BlockSpec((tr, fh), lambda o, d, i: (d * nl + i, 0)),
                  pl.BlockSpec((tr, 1), lambda o, d, i: (d * nl + i, 0)),
                  pl.BlockSpec((fh, e), lambda o, d, i: (0, o * 2 + d)),
                  pl.BlockSpec((1, e), lambda o, d, i: (0, 0))],
        out_specs=[pl.BlockSpec((1, tr, e), lambda o, d, i: (o, d * nl + i, 0)),
                   pl.BlockSpec((1, SUBLANES, e), lambda o, d, i: (o, 0, 0))],
        out_shape=[jax.ShapeDtypeStruct((order, n, e), jnp.float32),
                   jax.ShapeDtypeStruct((order, SUBLANES, e), jnp.float32)],
        compiler_params=_params("parallel", "arbitrary", "arbitrary"),
        name="filter_gen",
    )(hid, tcol, w3, absdelta)


def _angles(num, den):
    ang = 2.0 * np.pi * (np.asarray(num, np.int64) % den).astype(np.float64) / den
    return np.cos(ang), np.sin(ang)


@functools.lru_cache(maxsize=None)
def _fft_tables(n1):
    n2 = FFT_N2
    n = n1 * n2
    f1 = n1 // 2 + 1
    f1p = _round_up(f1, 2)
    fr = np.arange(f1)[:, None]
    eye = np.eye(FFT_J)

    def stage_a(k1):
        c, s = _angles(fr * np.arange(k1)[None, :], n1)
        m = np.zeros((2 * f1p, k1))
        m[:f1], m[f1p:f1p + f1] = c, -s
        return np.kron(m, eye)

    wgt = np.full((f1,), 2.0)
    wgt[0] = wgt[-1] = 1.0
    c, s = _angles(np.arange(n1 // 2)[:, None] * np.arange(f1)[None, :], n1)
    m_c = np.zeros((n1 // 2, 2 * f1p))
    m_c[:, :f1], m_c[:, f1p:f1p + f1] = c * wgt, -s * wgt

    c2, s2 = _angles(np.arange(n2)[:, None] * np.arange(n2)[None, :], n2)
    cphi, sphi = _angles(fr * np.arange(n2)[None, :], n)
    phase = np.stack([cphi, sphi], axis=1)
    bf = lambda a: jnp.asarray(a, jnp.float32).astype(jnp.bfloat16)
    f32 = lambda a: jnp.asarray(a, jnp.float32)
    return dict(f1=f1, f1p=f1p, a_data=bf(stage_a(n1 // 2)), a_full=bf(stage_a(n1)),
                m_c=bf(np.kron(m_c, eye)), base_c=f32(c2), base_s=f32(s2),
                phase_row=f32(phase[:, :, None, :]), phase_col=f32(phase[..., None]))


def _twiddled_dft_kernel(c_ref, s_ref, row_ref, col_ref, mf_ref, mi_ref):
    c, s = c_ref[...], s_ref[...]
    n2 = c.shape[0]
    bf = jnp.bfloat16
    cr, sr = row_ref[0, 0], row_ref[0, 1]
    cf, sf = (c * cr - s * sr).astype(bf), (s * cr + c * sr).astype(bf)
    mf_ref[0, :n2, :n2] = cf
    mf_ref[0, :n2, n2:] = sf
    mf_ref[0, n2:, :n2] = -sf
    mf_ref[0, n2:, n2:] = cf
    cc, sc = col_ref[0, 0], col_ref[0, 1]
    ci, si = (c * cc - s * sc).astype(bf), (s * cc + c * sc).astype(bf)
    mi_ref[0, :n2, :n2] = ci
    mi_ref[0, :n2, n2:] = -si
    mi_ref[0, n2:, :n2] = si
    mi_ref[0, n2:, n2:] = ci


def _twiddled_dfts(tabs):
    f1 = tabs["f1"]
    n2 = FFT_N2
    shape = jax.ShapeDtypeStruct((f1, 2 * n2, 2 * n2), jnp.bfloat16)
    base = pl.BlockSpec((n2, n2), lambda f: (0, 0))
    out = pl.BlockSpec((1, 2 * n2, 2 * n2), lambda f: (f, 0, 0))
    return pl.pallas_call(
        _twiddled_dft_kernel,
        grid=(f1,),
        in_specs=[base, base, pl.BlockSpec((1, 2, 1, n2), lambda f: (f, 0, 0, 0)),
                  pl.BlockSpec((1, 2, n2, 1), lambda f: (f, 0, 0, 0))],
        out_specs=[out, out],
        out_shape=[shape, shape],
        compiler_params=_params("parallel"),
        name="twiddled_dft_tables",
    )(tabs["base_c"], tabs["base_s"], tabs["phase_row"], tabs["phase_col"])


def _fft_a_kernel(*refs, scaled):
    it = iter(refs)
    x_ref = next(it)
    sc_ref = next(it) if scaled else None
    m_ref, re_ref, im_ref = next(it), next(it), next(it)
    k1, ec = x_ref.shape[2], x_ref.shape[4]
    half = m_ref.shape[0] // 2
    f1p = half // FFT_J
    re_parts, im_parts = [], []
    for h in range(x_ref.shape[3] // FFT_J):
        x = x_ref[0, 0, :, h * FFT_J:(h + 1) * FFT_J, :].reshape(k1 * FFT_J, ec)
        if scaled:
            x = x * sc_ref[0]
        r = jnp.dot(m_ref[...], x.astype(jnp.bfloat16), preferred_element_type=jnp.float32)
        re_parts.append(r[:half].reshape(f1p, FFT_J, ec))
        im_parts.append(r[half:].reshape(f1p, FFT_J, ec))
    re_ref[0] = jnp.concatenate(re_parts, axis=1).astype(jnp.bfloat16)
    im_ref[0] = jnp.concatenate(im_parts, axis=1).astype(jnp.bfloat16)


def _fft_stage_a(arr, group, tabs, scale=None):
    _, bt, rows, e = arr.shape
    k1 = rows // FFT_N2
    f1p = tabs["f1p"]
    mat = tabs["a_data"] if k1 * FFT_J == tabs["a_data"].shape[1] else tabs["a_full"]
    assert mat.shape[1] == k1 * FFT_J
    ec = min(e, FFT_EC)
    av = arr.reshape(arr.shape[0], bt, k1, FFT_N2, e)
    in_specs = [pl.BlockSpec((1, 1, k1, BF16_ROWS, ec), lambda bb, j, cc: (group, bb, 0, j, cc))]
    args = [av]
    if scale is not None:
        in_specs.append(pl.BlockSpec((1, 1, ec), lambda bb, j, cc: (bb, 0, cc)))
        args.append(scale)
    in_specs.append(pl.BlockSpec(mat.shape, lambda bb, j, cc: (0, 0)))
    args.append(mat)
    out_spec = pl.BlockSpec((1, f1p, BF16_ROWS, ec), lambda bb, j, cc: (bb, 0, j, cc))
    shape = jax.ShapeDtypeStruct((bt, f1p, FFT_N2, e), jnp.bfloat16)
    return pl.pallas_call(
        functools.partial(_fft_a_kernel, scaled=scale is not None),
        grid=(bt, FFT_N2 // BF16_ROWS, e // ec),
        in_specs=in_specs,
        out_specs=[out_spec, out_spec],
        out_shape=[shape, shape],
        compiler_params=_params("parallel", "parallel", "parallel"),
        name="fft_stage_a",
    )(*args)


def _fft_b_filter_kernel(re_ref, im_ref, mf_ref, kre_ref, kim_ref):
    x = jnp.concatenate([re_ref[0, 0], im_ref[0, 0]], axis=0)
    u = jnp.dot(mf_ref[0], x, preferred_element_type=jnp.float32)
    kre_ref[0, 0] = u[:FFT_N2]
    kim_ref[0, 0] = u[FFT_N2:]


def _fft_b_conv_kernel(re_ref, im_ref, kre_ref, kim_ref, mf_ref, mi_ref, ore_ref, oim_ref):
    x = jnp.concatenate([re_ref[0, 0], im_ref[0, 0]], axis=0)
    u = jnp.dot(mf_ref[0], x, preferred_element_type=jnp.float32)
    u_re, u_im = u[:FFT_N2], u[FFT_N2:]
    k_re, k_im = kre_ref[0, 0], kim_ref[0, 0]
    p = jnp.concatenate([u_re * k_re - u_im * k_im, u_re * k_im + u_im * k_re], axis=0)
    v = jnp.dot(mi_ref[0], p.astype(jnp.bfloat16), preferred_element_type=jnp.float32)
    ore_ref[0, 0] = v[:FFT_N2].astype(jnp.bfloat16)
    oim_ref[0, 0] = v[FFT_N2:].astype(jnp.bfloat16)


def _fft_stage_b_filter(a_re, a_im, mf, tabs):
    bt, _, n2, e = a_re.shape
    f1 = tabs["f1"]
    blk = pl.BlockSpec((1, 1, n2, e), lambda f, bb: (bb, f, 0, 0))
    shape = jax.ShapeDtypeStruct((bt, f1, n2, e), jnp.float32)
    return pl.pallas_call(
        _fft_b_filter_kernel,
        grid=(f1, bt),
        in_specs=[blk, blk, pl.BlockSpec((1, 2 * n2, 2 * n2), lambda f, bb: (f, 0, 0))],
        out_specs=[blk, blk],
        out_shape=[shape, shape],
        compiler_params=_params("parallel", "parallel"),
        name="fft_stage_b_filter",
    )(a_re, a_im, mf)


def _fft_stage_b_conv(a_re, a_im, kf_re, kf_im, order_idx, mf, mi, tabs):
    bt, f1p, n2, e = a_re.shape
    f1 = tabs["f1"]
    fc = lambda f: jnp.minimum(f, f1 - 1)
    blk = pl.BlockSpec((1, 1, n2, e), lambda f, bb: (bb, f, 0, 0))
    kblk = pl.BlockSpec((1, 1, n2, e), lambda f, bb: (order_idx, fc(f), 0, 0))
    wblk = pl.BlockSpec((1, 2 * n2, 2 * n2), lambda f, bb: (fc(f), 0, 0))
    shape = jax.ShapeDtypeStruct((bt, f1p, n2, e), jnp.bfloat16)
    return pl.pallas_call(
        _fft_b_conv_kernel,
        grid=(f1p, bt),
        in_specs=[blk, blk, kblk, kblk, wblk, wblk],
        out_specs=[blk, blk],
        out_shape=[shape, shape],
        compiler_params=_params("parallel", "parallel"),
        name="fft_stage_b_conv",
    )(a_re, a_im, kf_re, kf_im, mf, mi)


def _fft_c_kernel(re_ref, im_ref, m_ref, u_ref, mul_ref, d_ref, o_ref):
    k1, ec = o_ref.shape[1], o_ref.shape[3]
    f1p = re_ref.shape[1]
    re = re_ref[0].astype(jnp.float32)
    im = im_ref[0].astype(jnp.float32)
    for h in range(o_ref.shape[2] // FFT_J):
        sl = slice(h * FFT_J, (h + 1) * FFT_J)
        x = jnp.concatenate([re[:, sl, :].reshape(f1p * FFT_J, ec),
                             im[:, sl, :].reshape(f1p * FFT_J, ec)], axis=0)
        y = jnp.dot(m_ref[...], x.astype(jnp.bfloat16), preferred_element_type=jnp.float32)
        u = u_ref[0, 0, :, sl, :].reshape(k1 * FFT_J, ec)
        mul = mul_ref[0, 0, :, sl, :].reshape(k1 * FFT_J, ec)
        o_ref[0, :, sl, :] = (mul * (y + u * d_ref[...])).reshape(k1, FFT_J, ec)


def _fft_stage_c(b_re, b_im, u_arr, u_group, proj, mul_group, d_row, tabs):
    bt, f1p, n2, e = b_re.shape
    l = u_arr.shape[2]
    k1 = l // n2
    ec = min(e, FFT_EC)
    uv = u_arr.reshape(u_arr.shape[0], bt, k1, n2, e)
    pv = proj.reshape(proj.shape[0], bt, k1, n2, e)
    mat = tabs["m_c"]
    fblk = pl.BlockSpec((1, f1p, BF16_ROWS, ec), lambda bb, j, cc: (bb, 0, j, cc))
    out = pl.pallas_call(
        _fft_c_kernel,
        grid=(bt, n2 // BF16_ROWS, e // ec),
        in_specs=[fblk, fblk, pl.BlockSpec(mat.shape, lambda bb, j, cc: (0, 0)),
                  pl.BlockSpec((1, 1, k1, BF16_ROWS, ec), lambda bb, j, cc: (u_group, bb, 0, j, cc)),
                  pl.BlockSpec((1, 1, k1, BF16_ROWS, ec), lambda bb, j, cc: (mul_group, bb, 0, j, cc)),
                  pl.BlockSpec((1, ec), lambda bb, j, cc: (0, cc))],
        out_specs=pl.BlockSpec((1, k1, BF16_ROWS, ec), lambda bb, j, cc: (bb, 0, j, cc)),
        out_shape=jax.ShapeDtypeStruct((bt, k1, n2, e), jnp.float32),
        compiler_params=_params("parallel", "parallel", "parallel"),
        name="fft_stage_c",
    )(b_re, b_im, mat, uv, pv, d_row)
    return out.reshape(bt, l, e)


def _hyena_long(proj, k, inv_s, d_bias, tabs, mf, mi):
    order, _, e = k.shape
    f_re, f_im = _fft_stage_a(k[None], 0, tabs, scale=inv_s[:, :1, :])
    kf_re, kf_im = _fft_stage_b_filter(f_re, f_im, mf, tabs)
    u_arr, u_group = proj, 0
    for o in range(order):
        a_re, a_im = _fft_stage_a(u_arr, u_group, tabs)
        b_re, b_im = _fft_stage_b_conv(a_re, a_im, kf_re, kf_im, o, mf, mi, tabs)
        y = _fft_stage_c(b_re, b_im, u_arr, u_group, proj, o + 1, d_bias[o].reshape(1, e), tabs)
        u_arr, u_group = y[None], 0
    return y


@functools.lru_cache(maxsize=None)
def _dft_tables(l):
    n = 2 * l
    nf = l + 1
    nfp = _round_up(nf, BF16_ROWS)
    c, s = _angles(np.arange(nf)[:, None] * np.arange(n)[None, :], n)
    fwd = np.zeros((2 * nfp, n))
    fwd[:nf], fwd[nfp:nfp + nf] = c, -s
    wgt = np.full((nf,), 2.0)
    wgt[0] = wgt[-1] = 1.0
    inv = np.zeros((l, 2 * nfp))
    inv[:, :nf], inv[:, nfp:nfp + nf] = (c[:, :l] * wgt[:, None]).T, (-s[:, :l] * wgt[:, None]).T
    bf = lambda a: jnp.asarray(a, jnp.float32).astype(jnp.bfloat16)
    return nfp, bf(fwd), bf(inv)


def _hyena_short_kernel(p_ref, k_ref, s_ref, d_ref, fwd_ref, inv_ref, o_ref, *, l, nfp, order):
    fwd = fwd_ref[...]
    u = p_ref[0, 0]
    for o in range(order):
        kn = (k_ref[o] * s_ref[o, 0:1, :]).astype(jnp.bfloat16)
        kf = jnp.dot(fwd, kn, preferred_element_type=jnp.float32)
        uf = jnp.dot(fwd[:, :l], u.astype(jnp.bfloat16), preferred_element_type=jnp.float32)
        k_re, k_im, u_re, u_im = kf[:nfp], kf[nfp:], uf[:nfp], uf[nfp:]
        p = jnp.concatenate([u_re * k_re - u_im * k_im, u_re * k_im + u_im * k_re], axis=0)
        y = jnp.dot(inv_ref[...], p.astype(jnp.bfloat16), preferred_element_type=jnp.float32)
        u = p_ref[o + 1, 0] * (y + u * d_ref[o:o + 1, :])
    o_ref[0] = u


def _hyena_short(proj, k, inv_s, d_bias, e):
    nt, b, l, _ = proj.shape
    order = k.shape[0]
    nfp, fwd, inv = _dft_tables(l)
    return pl.pallas_call(
        functools.partial(_hyena_short_kernel, l=l, nfp=nfp, order=order),
        grid=(b,),
        in_specs=[pl.BlockSpec((nt, 1, l, e), lambda bb: (0, bb, 0, 0)),
                  pl.BlockSpec((order, 2 * l, e), lambda bb: (0, 0, 0)),
                  pl.BlockSpec((order, SUBLANES, e), lambda bb: (0, 0, 0)),
                  pl.BlockSpec((order, e), lambda bb: (0, 0)),
                  pl.BlockSpec(fwd.shape, lambda bb: (0, 0)),
                  pl.BlockSpec(inv.shape, lambda bb: (0, 0))],
        out_specs=pl.BlockSpec((1, l, e), lambda bb: (bb, 0, 0)),
        out_shape=jax.ShapeDtypeStruct((b, l, e), jnp.float32),
        compiler_params=_params("parallel"),
        name="hyena_short",
    )(proj, k, inv_s, d_bias, fwd, inv)


def kernel(x, c, ctx, c_ctx, w_mod, b_mod, ln_g, ln_b, rg_w_in, rg_conv_w, rg_conv_b, rg_w_r, rg_b_r, rg_w_i, rg_b_i, rg_lambda, rg_w_out, hy_w_in, hy_conv_w, hy_conv_b, hy_f_w1, hy_f_b1, hy_f_w2, hy_f_b2, hy_f_w3, hy_f_freq, hy_d, hy_w_out):
    bsz, seq, d = x.shape
    depth = w_mod.shape[0]
    e = rg_w_out.shape[1]
    alpha = (2 * depth) ** 0.25
    assert bsz + 1 <= SUBLANES

    cvec = jnp.zeros((SUBLANES, d), jnp.float32).at[:bsz].set(c).at[bsz].set(c_ctx)
    mod = _modulation(cvec, w_mod, b_mod)

    def mod_rows(layer, part, for_ctx):
        v = mod[layer, :, part * d:(part + 1) * d]
        v = jnp.broadcast_to(v[bsz], (bsz, d)) if for_ctx else v[:bsz]
        return v.reshape(bsz, 1, d)

    hy_cache = {}
    fft = None

    def hyena_filters(occ, l):
        if (occ, l) not in hy_cache:
            zz, tcol = _filter_positions(l)
            hid = _filter_hidden(zz, hy_f_w1[occ], hy_f_b1[occ], hy_f_w2[occ], hy_f_b2[occ], hy_f_freq[occ])
            hy_cache[(occ, l)] = _filters(hid, tcol, hy_f_w3[occ], e)
        return hy_cache[(occ, l)]

    for i in range(depth):
        kind = i % N_MIXERS
        occ = i // N_MIXERS
        need_ctx_out = any(j % N_MIXERS == 0 for j in range(i + 1, depth))
        col_major = occ % 2 == 1
        use_ctx = kind == 0 or need_ctx_out
        shift, scale, gate = (mod_rows(i, p, False) for p in range(3))
        if use_ctx:
            shift_c, scale_c, gate_c = (mod_rows(i, p, True) for p in range(3))

        if kind == 0:
            w_in = rg_w_in[occ].astype(jnp.bfloat16)
            w_out = rg_w_out[occ].astype(jnp.bfloat16)
            proj = _inproj(x, shift, scale, w_in, rg_conv_w[occ], rg_conv_b[occ],
                           e=e, pad_l=1, col_major=col_major)
            w_in_c = w_in if need_ctx_out else w_in[:, :e]
            proj_c = _inproj(ctx, shift_c, scale_c, w_in_c, rg_conv_w[occ], rg_conv_b[occ],
                             e=e, pad_l=1, col_major=False)
            zeros = jnp.zeros((bsz, 1, e), jnp.float32)
            gates = [(_gate_weights(rg_w_r[occ, dd], rg_w_i[occ, dd]), rg_b_r[occ, dd], rg_b_i[occ, dd],
                      rg_lambda[occ, dd]) for dd in range(2)]
            hc_f, st_f = _rg_scan(proj_c, zeros, *gates[0], reverse=False, write_seq=need_ctx_out)
            h_f, _ = _rg_scan(proj, st_f, *gates[0], reverse=False)
            y_c, st_b = _rg_scan(proj_c, zeros, *gates[1], reverse=True, prev=hc_f, write_seq=need_ctx_out)
            y, _ = _rg_scan(proj, st_b, *gates[1], reverse=True, prev=h_f)
            g_group = 1
        else:
            w_in = hy_w_in[occ].astype(jnp.bfloat16)
            w_out = hy_w_out[occ].astype(jnp.bfloat16)
            proj = _inproj(x, shift, scale, w_in, hy_conv_w[occ], hy_conv_b[occ],
                           e=e, pad_l=1, col_major=col_major)
            k, inv_s = hyena_filters(occ, seq)
            if fft is None:
                tabs = _fft_tables(2 * seq // FFT_N2)
                fft = (tabs,) + tuple(_twiddled_dfts(tabs))
            y = _hyena_long(proj, k, inv_s, hy_d[occ], *fft)
            if need_ctx_out:
                proj_c = _inproj(ctx, shift_c, scale_c, w_in, hy_conv_w[occ], hy_conv_b[occ],
                                 e=e, pad_l=1, col_major=False)
                k_c, inv_s_c = hyena_filters(occ, ctx.shape[1])
                y_c = _hyena_short(proj_c, k_c, inv_s_c, hy_d[occ], e)
            g_group = 3

        x = _outproj(y, proj, g_group, w_out, x, gate, ln_g[i], ln_b[i], alpha=alpha, col_major=col_major)
        if need_ctx_out:
            ctx = _outproj(y_c, proj_c, g_group, w_out, ctx, gate_c, ln_g[i], ln_b[i],
                           alpha=alpha, col_major=False)
    return x
```

```python
import functools
import math

import numpy as np
import jax
import jax.numpy as jnp
from jax import lax
from jax.experimental import pallas as pl
from jax.experimental.pallas import tpu as pltpu

GRID_W = 64
N_MIXERS = 2
RG_C = 8.0
RG_GATE_GROUP = 4
HY_EMB_BANDS = 8
HY_FAST_DECAY_PCT = 0.3
HY_SLOW_DECAY_PCT = 1.5
HY_DECAY_TARGET = 1e-2
HY_MOD_SHIFT = 0.05
LN_EPS = 1e-5

SUBLANES = 8
BF16_ROWS = 16
FFT_N2 = 256
FFT_J = SUBLANES
FFT_EC = 768
STREAM_TILE = 512
LANE_CHUNK = 512
VMEM_LIMIT = 56 * 1024 * 1024

_HIGHEST = lax.Precision.HIGHEST


def _round_up(a, m):
    return (a + m - 1) // m * m


def _params(*sem):
    return pltpu.CompilerParams(dimension_semantics=sem, vmem_limit_bytes=VMEM_LIMIT)


def _sigmoid(v):
    return 1.0 / (1.0 + jnp.exp(-v))


def _mod_kernel(c_ref, w_ref, b_ref, o_ref):
    cv = c_ref[...]
    act = cv * _sigmoid(cv)
    o_ref[0] = jnp.dot(act, w_ref[0], preferred_element_type=jnp.float32,
                       precision=_HIGHEST) + b_ref[0]


def _modulation(cvec, w_mod, b_mod):
    depth, d, d3 = w_mod.shape
    nblk = d3 // d
    return pl.pallas_call(
        _mod_kernel,
        grid=(depth, nblk),
        in_specs=[pl.BlockSpec((SUBLANES, d), lambda l, n: (0, 0)),
                  pl.BlockSpec((1, d, d), lambda l, n: (l, 0, n)),
                  pl.BlockSpec((1, 1, d), lambda l, n: (l, 0, n))],
        out_specs=pl.BlockSpec((1, SUBLANES, d), lambda l, n: (l, 0, n)),
        out_shape=jax.ShapeDtypeStruct((depth, SUBLANES, d3), jnp.float32),
        compiler_params=_params("parallel", "parallel"),
        name="modulation",
    )(cvec, w_mod, b_mod.reshape(depth, 1, d3))


def _stream_view(x, col_major):
    b, l, d = x.shape
    if col_major:
        return x.reshape(b, l // GRID_W, GRID_W * d)
    return x


def _stream_specs(l, d, tm, col_major, nb_grid_prefix):
    def wrap(fn):
        return lambda *g: fn(*g[nb_grid_prefix:])
    if col_major:
        rows = l // GRID_W
        ncol = tm // rows
        assert tm == rows * ncol
        main = pl.BlockSpec((1, rows, ncol * d), wrap(lambda b, i: (b, 0, i)))
        prev = pl.BlockSpec((1, SUBLANES, d),
                            wrap(lambda b, i: (b, rows // SUBLANES - 1, jnp.maximum(i * ncol - 1, 0))))
        nxt = pl.BlockSpec((1, SUBLANES, d),
                           wrap(lambda b, i: (b, 0, jnp.minimum((i + 1) * ncol, GRID_W - 1))))
    else:
        per = tm // SUBLANES
        last = l // SUBLANES - 1
        main = pl.BlockSpec((1, tm, d), wrap(lambda b, i: (b, i, 0)))
        prev = pl.BlockSpec((1, SUBLANES, d), wrap(lambda b, i: (b, jnp.maximum(i * per - 1, 0), 0)))
        nxt = pl.BlockSpec((1, SUBLANES, d), wrap(lambda b, i: (b, jnp.minimum((i + 1) * per, last), 0)))
    return main, prev, nxt


def _tile_rows(l, col_major):
    tm = min(l, STREAM_TILE)
    if col_major:
        rows = l // GRID_W
        return max(tm // rows, 1) * rows
    return tm


def _tile_tokens(ref, d):
    blk = ref[0]
    ncol = blk.shape[1] // d
    if ncol == 1:
        return blk
    return jnp.concatenate([blk[:, j * d:(j + 1) * d] for j in range(ncol)], axis=0)


def _inproj_kernel(x_ref, xp_ref, xn_ref, sh_ref, sc_ref, w_ref, cw_ref, cb_ref, o_ref, u_scr,
                   *, n_conv, conv_k, pad_l, tm):
    n = pl.program_id(0)
    i = pl.program_id(2)
    nt = pl.num_programs(2)
    d, e = w_ref.shape
    scale = 1.0 + sc_ref[0]
    shift = sh_ref[0]
    h_main = _tile_tokens(x_ref, d) * scale + shift
    chunks = [slice(c0, min(c0 + LANE_CHUNK, e)) for c0 in range(0, e, LANE_CHUNK)]

    @pl.when(n < n_conv)
    def _():
        h_prev = xp_ref[0] * scale + shift
        h_next = xn_ref[0] * scale + shift
        hh = jnp.concatenate([h_prev, h_main, h_next], axis=0).astype(jnp.bfloat16)
        row = lax.broadcasted_iota(jnp.int32, (tm + 2 * SUBLANES, LANE_CHUNK), 0)
        outside = ((row < SUBLANES) & (i == 0)) | ((row >= tm + SUBLANES) & (i == nt - 1))
        for sl in chunks:
            u = jnp.dot(hh, w_ref[:, sl], preferred_element_type=jnp.float32)
            u_scr[:, sl] = jnp.where(outside[:, :u.shape[1]], 0.0, u)
            acc = cb_ref[:, sl] + cw_ref[0:1, sl] * u_scr[pl.ds(SUBLANES - pad_l, tm), sl]
            for k in range(1, conv_k):
                acc = acc + cw_ref[k:k + 1, sl] * u_scr[pl.ds(SUBLANES - pad_l + k, tm), sl]
            o_ref[0, 0, :, sl] = acc.astype(o_ref.dtype)

    @pl.when(n >= n_conv)
    def _():
        hb = h_main.astype(jnp.bfloat16)
        for sl in chunks:
            o_ref[0, 0, :, sl] = jnp.dot(hb, w_ref[:, sl],
                                         preferred_element_type=jnp.float32).astype(o_ref.dtype)


def _inproj(x, shift, scale, w_bf16, conv_w, conv_b, *, e, pad_l, col_major):
    b, l, d = x.shape
    nt_groups = w_bf16.shape[1] // e
    conv_k = conv_w.shape[0]
    n_conv = conv_w.shape[1] // e
    tm = _tile_rows(l, col_major)
    ntile = l // tm
    main, prev, nxt = _stream_specs(l, d, tm, col_major, 1)
    xv = _stream_view(x, col_major)
    kern = functools.partial(_inproj_kernel, n_conv=n_conv, conv_k=conv_k, pad_l=pad_l, tm=tm)
    return pl.pallas_call(
        kern,
        grid=(nt_groups, b, ntile),
        in_specs=[main, prev, nxt,
                  pl.BlockSpec((1, 1, d), lambda n, bb, i: (bb, 0, 0)),
                  pl.BlockSpec((1, 1, d), lambda n, bb, i: (bb, 0, 0)),
                  pl.BlockSpec((d, e), lambda n, bb, i: (0, n)),
                  pl.BlockSpec((conv_k, e), lambda n, bb, i: (0, jnp.minimum(n, n_conv - 1))),
                  pl.BlockSpec((1, e), lambda n, bb, i: (0, jnp.minimum(n, n_conv - 1)))],
        out_specs=pl.BlockSpec((1, 1, tm, e), lambda n, bb, i: (n, bb, i, 0)),
        out_shape=jax.ShapeDtypeStruct((nt_groups, b, l, e), jnp.bfloat16),
        scratch_shapes=[pltpu.VMEM((tm + 2 * SUBLANES, e), jnp.float32)],
        compiler_params=_params("parallel", "parallel", "parallel"),
        name="inproj_conv",
    )(xv, xv, xv, shift, scale, w_bf16, conv_w, conv_b.reshape(1, -1))


def _rg_scan_kernel(*refs, reverse, add_prev, write_seq, tm, gw, ngroups):
    it = iter(refs)
    xc_ref = next(it)
    prev_ref = next(it) if add_prev else None
    h0_ref, wg_ref, br_ref, bi_ref, lam_ref = next(it), next(it), next(it), next(it), next(it)
    o_ref = next(it) if write_seq else None
    st_ref = next(it)
    a_scr, b_scr, carry = next(it), next(it), next(it)

    i = pl.program_id(1)
    nchunk = pl.num_programs(1)

    @pl.when(i == 0)
    def _():
        carry[...] = h0_ref[0]

    lam = lam_ref[...]
    neg = -lam
    softplus = jnp.maximum(neg, 0.0) + jnp.log(1.0 + jnp.exp(-jnp.abs(neg)))
    for g in range(ngroups):
        sl = slice(g * gw, (g + 1) * gw)
        xg = xc_ref[0, 0, :, sl]
        pre = jnp.dot(xg, wg_ref[g], preferred_element_type=jnp.float32)
        r = _sigmoid(pre[:, :gw] + br_ref[:, sl])
        ig = _sigmoid(pre[:, gw:] + bi_ref[:, sl])
        log_a = (-RG_C) * r * softplus[:, sl]
        a = jnp.exp(log_a)
        a_scr[:, sl] = a
        b_scr[:, sl] = jnp.sqrt(-jnp.tanh(log_a) * (a * a + 1.0)) * (ig * xg.astype(jnp.float32))

    e = a_scr.shape[1]
    row = lax.broadcasted_iota(jnp.int32, (SUBLANES, e), 0)
    nblk = tm // BF16_ROWS

    def scan8(r0, c):
        a = a_scr[pl.ds(r0, SUBLANES), :]
        h = b_scr[pl.ds(r0, SUBLANES), :]
        for dist in (1, 2, 4):
            shift = (SUBLANES - dist) if reverse else dist
            valid = (row + dist < SUBLANES) if reverse else (row >= dist)
            a_sh = jnp.where(valid, pltpu.roll(a, shift, 0), 1.0)
            h_sh = jnp.where(valid, pltpu.roll(h, shift, 0), 0.0)
            h = a * h_sh + h
            a = a * a_sh
        h = a * c + h
        return h, (h[0:1, :] if reverse else h[SUBLANES - 1:SUBLANES, :])

    def block(k, c):
        kk = (nblk - 1 - k) if reverse else k
        r0 = pl.multiple_of(kk * BF16_ROWS, BF16_ROWS)
        halves = [None, None]
        for hh in ((1, 0) if reverse else (0, 1)):
            halves[hh], c = scan8(r0 + hh * SUBLANES, c)
        if write_seq:
            h = jnp.concatenate(halves, axis=0)
            if add_prev:
                h = h + prev_ref[0, pl.ds(r0, BF16_ROWS), :].astype(jnp.float32)
            o_ref[0, pl.ds(r0, BF16_ROWS), :] = h.astype(o_ref.dtype)
        return c

    c_fin = lax.fori_loop(0, nblk, block, carry[...])
    carry[...] = c_fin

    @pl.when(i == nchunk - 1)
    def _():
        st_ref[0] = c_fin


def _rg_scan(proj, h0, wg, b_r, b_i, lam, *, reverse, prev=None, write_seq=True):
    _, b, l, e = proj.shape
    tm = min(l, 256)
    nchunk = l // tm
    ngroups, gw, _ = wg.shape
    pos = (lambda i: nchunk - 1 - i) if reverse else (lambda i: i)
    add_prev = prev is not None
    in_specs = [pl.BlockSpec((1, 1, tm, e), lambda bb, i: (0, bb, pos(i), 0))]
    args = [proj]
    if add_prev:
        in_specs.append(pl.BlockSpec((1, tm, e), lambda bb, i: (bb, pos(i), 0)))
        args.append(prev)
    in_specs += [pl.BlockSpec((1, 1, e), lambda bb, i: (bb, 0, 0)),
                 pl.BlockSpec((ngroups, gw, 2 * gw), lambda bb, i: (0, 0, 0)),
                 pl.BlockSpec((1, e), lambda bb, i: (0, 0)),
                 pl.BlockSpec((1, e), lambda bb, i: (0, 0)),
                 pl.BlockSpec((1, e), lambda bb, i: (0, 0))]
    args += [h0, wg, b_r.reshape(1, e), b_i.reshape(1, e), lam.reshape(1, e)]
    out_specs, out_shape = [], []
    if write_seq:
        out_specs.append(pl.BlockSpec((1, tm, e), lambda bb, i: (bb, pos(i), 0)))
        out_shape.append(jax.ShapeDtypeStruct((b, l, e), jnp.bfloat16))
    out_specs.append(pl.BlockSpec((1, 1, e), lambda bb, i: (bb, 0, 0)))
    out_shape.append(jax.ShapeDtypeStruct((b, 1, e), jnp.float32))
    kern = functools.partial(_rg_scan_kernel, reverse=reverse, add_prev=add_prev,
                             write_seq=write_seq, tm=tm, gw=gw, ngroups=ngroups)
    res = pl.pallas_call(
        kern,
        grid=(b, nchunk),
        in_specs=in_specs,
        out_specs=out_specs,
        out_shape=out_shape,
        scratch_shapes=[pltpu.VMEM((tm, e), jnp.float32), pltpu.VMEM((tm, e), jnp.float32),
                        pltpu.VMEM((1, e), jnp.float32)],
        compiler_params=_params("parallel", "arbitrary"),
        name="rg_scan_bwd" if reverse else "rg_scan_fwd",
    )(*args)
    if write_seq:
        return res[0], res[1]
    return None, res[0]


def _gate_weights(w_r, w_i):
    nb, bw, _ = w_r.shape
    g = RG_GATE_GROUP
    ng = nb // g
    eye = jnp.eye(g, dtype=w_r.dtype)

    def bd(w):
        w = w.reshape(ng, g, bw, bw)
        return jnp.einsum("nhij,hk->nhikj", w, eye).reshape(ng, g * bw, g * bw)

    return jnp.concatenate([bd(w_r), bd(w_i)], axis=-1).astype(jnp.bfloat16)


def _outproj_kernel(y_ref, g_ref, w_ref, x_ref, gate_ref, lg_ref, lb_ref, o_ref, *, alpha):
    d = w_ref.shape[1]
    g = g_ref[0, 0].astype(jnp.float32)
    z = (y_ref[0].astype(jnp.float32) * (g * _sigmoid(g))).astype(jnp.bfloat16)
    out = jnp.dot(z, w_ref[...], preferred_element_type=jnp.float32)
    r = alpha * _tile_tokens(x_ref, d) + gate_ref[0] * out
    mu = jnp.mean(r, axis=-1, keepdims=True)
    cen = r - mu
    var = jnp.mean(cen * cen, axis=-1, keepdims=True)
    res = cen * lax.rsqrt(var + LN_EPS) * lg_ref[...] + lb_ref[...]
    rows = o_ref.shape[1]
    for j in range(o_ref.shape[2] // d):
        o_ref[0, :, j * d:(j + 1) * d] = res[j * rows:(j + 1) * rows]


def _outproj(y, proj, g_group, w_bf16, x, gate, ln_g, ln_b, *, alpha, col_major):
    b, l, e = y.shape
    d = x.shape[-1]
    tm = _tile_rows(l, col_major)
    ntile = l // tm
    main, _, _ = _stream_specs(l, d, tm, col_major, 0)
    xv = _stream_view(x, col_major)
    out = pl.pallas_call(
        functools.partial(_outproj_kernel, alpha=alpha),
        grid=(b, ntile),
        in_specs=[pl.BlockSpec((1, tm, e), lambda bb, i: (bb, i, 0)),
                  pl.BlockSpec((1, 1, tm, e), lambda bb, i: (g_group, bb, i, 0)),
                  pl.BlockSpec((e, d), lambda bb, i: (0, 0)),
                  main,
                  pl.BlockSpec((1, 1, d), lambda bb, i: (bb, 0, 0)),
                  pl.BlockSpec((1, d), lambda bb, i: (0, 0)),
                  pl.BlockSpec((1, d), lambda bb, i: (0, 0))],
        out_specs=main,
        out_shape=jax.ShapeDtypeStruct(xv.shape, jnp.float32),
        compiler_params=_params("parallel", "parallel"),
        name="outproj_ln",
    )(y, proj, w_bf16, xv, gate, ln_g.reshape(1, d), ln_b.reshape(1, d))
    return out.reshape(x.shape)


def _filter_positions(l, fh):
    f32 = jnp.float32
    t = jnp.linspace(0.0, 1.0, l, dtype=f32)[:, None]
    bands = jnp.linspace(1e-4, HY_EMB_BANDS - 1, HY_EMB_BANDS, dtype=f32)
    w = (2.0 * math.pi) * jnp.arange(l, dtype=f32)[:, None] / l
    z = jnp.concatenate([t, jnp.cos(bands * w), -jnp.sin(bands * w)], axis=-1)
    z_rev = jnp.concatenate([z[:1], z[:0:-1]], axis=0)
    pad = lambda a: jnp.pad(a, ((0, 0), (0, fh - a.shape[1])))
    return jnp.concatenate([pad(z), pad(z_rev)], axis=1), jnp.stack([z[:, 0:1], z_rev[:, 0:1]])


def _hidden_kernel(z_ref, w1_ref, b1_ref, w2_ref, b2_ref, fr_ref, o_ref):
    fr = fr_ref[...]
    h = jnp.dot(z_ref[...], w1_ref[...], preferred_element_type=jnp.float32, precision=_HIGHEST)
    h = jnp.sin(fr * (h + b1_ref[...]))
    h = jnp.dot(h, w2_ref[...], preferred_element_type=jnp.float32, precision=_HIGHEST)
    o_ref[...] = jnp.sin(fr * (h + b2_ref[...]))


def _filter_hidden(zz, w1, b1, w2, b2, freq):
    n, zp = zz.shape
    fh = w1.shape[1]
    tr = min(n, 1024)
    both = lambda v: jnp.concatenate([v, v]).reshape(1, 2 * fh)
    blockdiag = lambda w: jnp.kron(jnp.eye(2, dtype=w.dtype), w)
    w1p = blockdiag(jnp.pad(w1, ((0, fh - w1.shape[0]), (0, 0))))
    full = lambda shape: pl.BlockSpec(shape, lambda i: (0,) * len(shape))
    return pl.pallas_call(
        _hidden_kernel,
        grid=(n // tr,),
        in_specs=[pl.BlockSpec((tr, zp), lambda i: (i, 0)), full((zp, zp)), full((1, zp)),
                  full((zp, zp)), full((1, zp)), full((1, zp))],
        out_specs=pl.BlockSpec((tr, zp), lambda i: (i, 0)),
        out_shape=jax.ShapeDtypeStruct((n, zp), jnp.float32),
        compiler_params=_params("parallel"),
        name="filter_hidden",
    )(zz, w1p, both(b1), blockdiag(w2), both(b2), both(freq))


def _split_bf16(a):
    hi = a.astype(jnp.bfloat16)
    return hi, (a - hi.astype(jnp.float32)).astype(jnp.bfloat16)


def _filter_kernel(hid_ref, t_ref, w3_ref, ad_ref, k_ref, s_ref, *, tr, n_total):
    d = pl.program_id(1)
    i = pl.program_id(2)
    last = (d == pl.num_programs(1) - 1) & (i == pl.num_programs(2) - 1)
    hi, lo = _split_bf16(hid_ref[...])
    h = jnp.dot(jnp.concatenate([hi, lo, hi], axis=1), w3_ref[0, 0], preferred_element_type=jnp.float32)
    window = jnp.exp(-t_ref[0] * ad_ref[...]) + HY_MOD_SHIFT
    row = lax.broadcasted_iota(jnp.int32, h.shape, 0)
    gap = (row == 0) & (d == 1) & (i == 0)
    k = jnp.where(gap, 0.0, h * window)
    k_ref[0] = k

    @pl.when((d == 0) & (i == 0))
    def _():
        s_ref[...] = jnp.zeros_like(s_ref)

    s_ref[0] += jnp.sum(jnp.abs(k).reshape(tr // SUBLANES, SUBLANES, -1), axis=0)

    @pl.when(last)
    def _():
        tot = jnp.sum(s_ref[0], axis=0, keepdims=True)
        s_ref[0] = jnp.broadcast_to(1.0 / (tot * n_total), s_ref.shape[1:])


def _filters(hid, tcol, w3, e):
    l, fh2 = hid.shape
    fh = fh2 // 2
    n = 2 * l
    order = w3.shape[1] // (2 * e)
    tr = min(l, 512)
    nl = l // tr
    max_decay = math.log(HY_DECAY_TARGET) / HY_FAST_DECAY_PCT
    min_decay = math.log(HY_DECAY_TARGET) / HY_SLOW_DECAY_PCT
    absdelta = jnp.abs(jnp.linspace(min_decay, max_decay, e, dtype=jnp.float32)).reshape(1, e)
    w3 = w3.reshape(fh, order, 2, e).transpose(1, 2, 0, 3)
    w3 = jnp.stack([jnp.pad(w3[:, 0], ((0, 0), (0, fh), (0, 0))),
                    jnp.pad(w3[:, 1], ((0, 0), (fh, 0), (0, 0)))], axis=1)
    w_hi, w_lo = _split_bf16(w3)
    w_stack = jnp.concatenate([w_hi, w_hi, w_lo], axis=2)
    return pl.pallas_call(
        functools.partial(_filter_kernel, tr=tr, n_total=float(n)),
        grid=(order, 2, nl),
        in_specs=[pl.BlockSpec((tr, fh2), lambda o, d, i: (i, 0)),
                  pl.BlockSpec((1, tr, 1), lambda o, d, i: (d, i, 0)),
                  pl.BlockSpec((1, 1, 3 * fh2, e), lambda o, d, i: (o, d, 0, 0)),
                  pl.BlockSpec((1, e), lambda o, d, i: (0, 0))],
        out_specs=[pl.BlockSpec((1, tr, e), lambda o, d, i: (o, d * nl + i, 0)),
                   pl.BlockSpec((1, SUBLANES, e), lambda o, d, i: (o, 0, 0))],
        out_shape=[jax.ShapeDtypeStruct((order, n, e), jnp.float32),
                   jax.ShapeDtypeStruct((order, SUBLANES, e), jnp.float32)],
        compiler_params=_params("parallel", "arbitrary", "arbitrary"),
        name="filter_gen",
    )(hid, tcol, w_stack, absdelta)


def _angles(num, den):
    ang = 2.0 * np.pi * (np.asarray(num, np.int64) % den).astype(np.float64) / den
    return np.cos(ang), np.sin(ang)


@functools.lru_cache(maxsize=None)
def _fft_tables(n1):
    n2 = FFT_N2
    n = n1 * n2
    f1 = n1 // 2 + 1
    f1p = _round_up(f1, 2)
    fr = np.arange(f1)[:, None]
    eye = np.eye(FFT_J)

    def stage_a(k1):
        c, s = _angles(fr * np.arange(k1)[None, :], n1)
        m = np.zeros((2 * f1p, k1))
        m[:f1], m[f1p:f1p + f1] = c, -s
        return np.kron(m, eye)

    wgt = np.full((f1,), 2.0)
    wgt[0] = wgt[-1] = 1.0
    c, s = _angles(np.arange(n1 // 2)[:, None] * np.arange(f1)[None, :], n1)
    m_c = np.zeros((n1 // 2, 2 * f1p))
    m_c[:, :f1], m_c[:, f1p:f1p + f1] = c * wgt, -s * wgt

    c2, s2 = _angles(np.arange(n2)[:, None] * np.arange(n2)[None, :], n2)
    cphi, sphi = _angles(fr * np.arange(n2)[None, :], n)
    phase = np.stack([cphi, sphi], axis=1)
    bf = lambda a: jnp.asarray(a, jnp.float32).astype(jnp.bfloat16)
    f32 = lambda a: jnp.asarray(a, jnp.float32)
    return dict(f1=f1, f1p=f1p, a_data=bf(stage_a(n1 // 2)), a_full=bf(stage_a(n1)),
                m_c=bf(np.kron(m_c, eye)), base_c=f32(c2), base_s=f32(s2),
                phase_row=f32(phase[:, :, None, :]), phase_col=f32(phase[..., None]))


def _twiddled_dft_kernel(c_ref, s_ref, row_ref, col_ref, mf_ref, mi_ref):
    c, s = c_ref[...], s_ref[...]
    n2 = c.shape[0]
    bf = jnp.bfloat16
    cr, sr = row_ref[0, 0], row_ref[0, 1]
    cf, sf = (c * cr - s * sr).astype(bf), (s * cr + c * sr).astype(bf)
    mf_ref[0, :n2, :n2] = cf
    mf_ref[0, :n2, n2:] = sf
    mf_ref[0, n2:, :n2] = -sf
    mf_ref[0, n2:, n2:] = cf
    cc, sc = col_ref[0, 0], col_ref[0, 1]
    ci, si = (c * cc - s * sc).astype(bf), (s * cc + c * sc).astype(bf)
    mi_ref[0, :n2, :n2] = ci
    mi_ref[0, :n2, n2:] = -si
    mi_ref[0, n2:, :n2] = si
    mi_ref[0, n2:, n2:] = ci


def _twiddled_dfts(tabs):
    f1 = tabs["f1"]
    n2 = FFT_N2
    shape = jax.ShapeDtypeStruct((f1, 2 * n2, 2 * n2), jnp.bfloat16)
    base = pl.BlockSpec((n2, n2), lambda f: (0, 0))
    out = pl.BlockSpec((1, 2 * n2, 2 * n2), lambda f: (f, 0, 0))
    return pl.pallas_call(
        _twiddled_dft_kernel,
        grid=(f1,),
        in_specs=[base, base, pl.BlockSpec((1, 2, 1, n2), lambda f: (f, 0, 0, 0)),
                  pl.BlockSpec((1, 2, n2, 1), lambda f: (f, 0, 0, 0))],
        out_specs=[out, out],
        out_shape=[shape, shape],
        compiler_params=_params("parallel"),
        name="twiddled_dft_tables",
    )(tabs["base_c"], tabs["base_s"], tabs["phase_row"], tabs["phase_col"])


def _fft_a_kernel(*refs, scaled):
    it = iter(refs)
    x_ref = next(it)
    sc_ref = next(it) if scaled else None
    m_ref, re_ref, im_ref = next(it), next(it), next(it)
    k1, ec = x_ref.shape[2], x_ref.shape[4]
    half = m_ref.shape[0] // 2
    f1p = half // FFT_J
    re_parts, im_parts = [], []
    xf = x_ref[0, 0].astype(jnp.float32)
    for h in range(x_ref.shape[3] // FFT_J):
        x = xf[:, h * FFT_J:(h + 1) * FFT_J, :].reshape(k1 * FFT_J, ec)
        if scaled:
            x = x * sc_ref[0]
        r = jnp.dot(m_ref[...], x.astype(jnp.bfloat16), preferred_element_type=jnp.float32)
        re_parts.append(r[:half].reshape(f1p, FFT_J, ec))
        im_parts.append(r[half:].reshape(f1p, FFT_J, ec))
    re_ref[0] = jnp.concatenate(re_parts, axis=1).astype(jnp.bfloat16)
    im_ref[0] = jnp.concatenate(im_parts, axis=1).astype(jnp.bfloat16)


def _fft_stage_a(arr, group, tabs, scale=None):
    _, bt, rows, e = arr.shape
    k1 = rows // FFT_N2
    f1p = tabs["f1p"]
    mat = tabs["a_data"] if k1 * FFT_J == tabs["a_data"].shape[1] else tabs["a_full"]
    assert mat.shape[1] == k1 * FFT_J
    ec = min(e, FFT_EC)
    av = arr.reshape(arr.shape[0], bt, k1, FFT_N2, e)
    in_specs = [pl.BlockSpec((1, 1, k1, BF16_ROWS, ec), lambda bb, j, cc: (group, bb, 0, j, cc))]
    args = [av]
    if scale is not None:
        in_specs.append(pl.BlockSpec((1, 1, ec), lambda bb, j, cc: (bb, 0, cc)))
        args.append(scale)
    in_specs.append(pl.BlockSpec(mat.shape, lambda bb, j, cc: (0, 0)))
    args.append(mat)
    out_spec = pl.BlockSpec((1, f1p, BF16_ROWS, ec), lambda bb, j, cc: (bb, 0, j, cc))
    shape = jax.ShapeDtypeStruct((bt, f1p, FFT_N2, e), jnp.bfloat16)
    return pl.pallas_call(
        functools.partial(_fft_a_kernel, scaled=scale is not None),
        grid=(bt, FFT_N2 // BF16_ROWS, e // ec),
        in_specs=in_specs,
        out_specs=[out_spec, out_spec],
        out_shape=[shape, shape],
        compiler_params=_params("parallel", "parallel", "parallel"),
        name="fft_stage_a",
    )(*args)


def _lane_chunks(e):
    return [slice(c0, min(c0 + LANE_CHUNK, e)) for c0 in range(0, e, LANE_CHUNK)]


def _fft_b_filter_kernel(re_ref, im_ref, mf_ref, kre_ref, kim_ref):
    for b in range(re_ref.shape[0]):
        for sl in _lane_chunks(re_ref.shape[3]):
            x = jnp.concatenate([re_ref[b, 0, :, sl], im_ref[b, 0, :, sl]], axis=0)
            u = jnp.dot(mf_ref[0], x, preferred_element_type=jnp.float32)
            kre_ref[b, 0, :, sl] = u[:FFT_N2].astype(kre_ref.dtype)
            kim_ref[b, 0, :, sl] = u[FFT_N2:].astype(kim_ref.dtype)


def _fft_b_conv_kernel(re_ref, im_ref, kre_ref, kim_ref, mf_ref, mi_ref, ore_ref, oim_ref):
    for sl in _lane_chunks(re_ref.shape[3]):
        k_re = kre_ref[0, 0, :, sl].astype(jnp.float32)
        k_im = kim_ref[0, 0, :, sl].astype(jnp.float32)
        for b in range(re_ref.shape[0]):
            x = jnp.concatenate([re_ref[b, 0, :, sl], im_ref[b, 0, :, sl]], axis=0)
            u = jnp.dot(mf_ref[0], x, preferred_element_type=jnp.float32)
            u_re, u_im = u[:FFT_N2], u[FFT_N2:]
            p = jnp.concatenate([u_re * k_re - u_im * k_im, u_re * k_im + u_im * k_re], axis=0)
            v = jnp.dot(mi_ref[0], p.astype(jnp.bfloat16), preferred_element_type=jnp.float32)
            ore_ref[b, 0, :, sl] = v[:FFT_N2].astype(jnp.bfloat16)
            oim_ref[b, 0, :, sl] = v[FFT_N2:].astype(jnp.bfloat16)


def _fft_stage_b_filter(a_re, a_im, mf, tabs):
    bt, _, n2, e = a_re.shape
    f1 = tabs["f1"]
    blk = pl.BlockSpec((bt, 1, n2, e), lambda f: (0, f, 0, 0))
    shape = jax.ShapeDtypeStruct((bt, f1, n2, e), jnp.bfloat16)
    return pl.pallas_call(
        _fft_b_filter_kernel,
        grid=(f1,),
        in_specs=[blk, blk, pl.BlockSpec((1, 2 * n2, 2 * n2), lambda f: (f, 0, 0))],
        out_specs=[blk, blk],
        out_shape=[shape, shape],
        compiler_params=_params("parallel"),
        name="fft_stage_b_filter",
    )(a_re, a_im, mf)


def _fft_stage_b_conv(a_re, a_im, kf_re, kf_im, order_idx, mf, mi, tabs):
    bt, f1p, n2, e = a_re.shape
    f1 = tabs["f1"]
    fc = lambda f: jnp.minimum(f, f1 - 1)
    blk = pl.BlockSpec((bt, 1, n2, e), lambda f: (0, f, 0, 0))
    kblk = pl.BlockSpec((1, 1, n2, e), lambda f: (order_idx, fc(f), 0, 0))
    wblk = pl.BlockSpec((1, 2 * n2, 2 * n2), lambda f: (fc(f), 0, 0))
    shape = jax.ShapeDtypeStruct((bt, f1p, n2, e), jnp.bfloat16)
    return pl.pallas_call(
        _fft_b_conv_kernel,
        grid=(f1p,),
        in_specs=[blk, blk, kblk, kblk, wblk, wblk],
        out_specs=[blk, blk],
        out_shape=[shape, shape],
        compiler_params=_params("parallel"),
        name="fft_stage_b_conv",
    )(a_re, a_im, kf_re, kf_im, mf, mi)


def _fft_c_kernel(re_ref, im_ref, m_ref, u_ref, mul_ref, d_ref, o_ref):
    k1, ec = o_ref.shape[1], o_ref.shape[3]
    f1p = re_ref.shape[1]
    re = re_ref[0].astype(jnp.float32)
    im = im_ref[0].astype(jnp.float32)
    uf = u_ref[0, 0].astype(jnp.float32)
    mf = mul_ref[0, 0].astype(jnp.float32)
    parts = []
    for h in range(o_ref.shape[2] // FFT_J):
        sl = slice(h * FFT_J, (h + 1) * FFT_J)
        x = jnp.concatenate([re[:, sl, :].reshape(f1p * FFT_J, ec),
                             im[:, sl, :].reshape(f1p * FFT_J, ec)], axis=0)
        y = jnp.dot(m_ref[...], x.astype(jnp.bfloat16), preferred_element_type=jnp.float32)
        u = uf[:, sl, :].reshape(k1 * FFT_J, ec)
        mul = mf[:, sl, :].reshape(k1 * FFT_J, ec)
        parts.append((mul * (y + u * d_ref[...])).reshape(k1, FFT_J, ec))
    o_ref[0] = jnp.concatenate(parts, axis=1).astype(o_ref.dtype)


def _fft_stage_c(b_re, b_im, u_arr, u_group, proj, mul_group, d_row, tabs):
    bt, f1p, n2, e = b_re.shape
    l = u_arr.shape[2]
    k1 = l // n2
    ec = min(e, FFT_EC)
    uv = u_arr.reshape(u_arr.shape[0], bt, k1, n2, e)
    pv = proj.reshape(proj.shape[0], bt, k1, n2, e)
    mat = tabs["m_c"]
    fblk = pl.BlockSpec((1, f1p, BF16_ROWS, ec), lambda bb, j, cc: (bb, 0, j, cc))
    out = pl.pallas_call(
        _fft_c_kernel,
        grid=(bt, n2 // BF16_ROWS, e // ec),
        in_specs=[fblk, fblk, pl.BlockSpec(mat.shape, lambda bb, j, cc: (0, 0)),
                  pl.BlockSpec((1, 1, k1, BF16_ROWS, ec), lambda bb, j, cc: (u_group, bb, 0, j, cc)),
                  pl.BlockSpec((1, 1, k1, BF16_ROWS, ec), lambda bb, j, cc: (mul_group, bb, 0, j, cc)),
                  pl.BlockSpec((1, ec), lambda bb, j, cc: (0, cc))],
        out_specs=pl.BlockSpec((1, k1, BF16_ROWS, ec), lambda bb, j, cc: (bb, 0, j, cc)),
        out_shape=jax.ShapeDtypeStruct((bt, k1, n2, e), jnp.bfloat16),
        compiler_params=_params("parallel", "parallel", "parallel"),
        name="fft_stage_c",
    )(b_re, b_im, mat, uv, pv, d_row)
    return out.reshape(bt, l, e)


def _hyena_long(proj, k, inv_s, d_bias, tabs, mf, mi):
    order, _, e = k.shape
    f_re, f_im = _fft_stage_a(k[None], 0, tabs, scale=inv_s[:, :1, :])
    kf_re, kf_im = _fft_stage_b_filter(f_re, f_im, mf, tabs)
    u_arr, u_group = proj, 0
    for o in range(order):
        a_re, a_im = _fft_stage_a(u_arr, u_group, tabs)
        b_re, b_im = _fft_stage_b_conv(a_re, a_im, kf_re, kf_im, o, mf, mi, tabs)
        y = _fft_stage_c(b_re, b_im, u_arr, u_group, proj, o + 1, d_bias[o].reshape(1, e), tabs)
        u_arr, u_group = y[None], 0
    return y


@functools.lru_cache(maxsize=None)
def _dft_tables(l):
    n = 2 * l
    nf = l + 1
    nfp = _round_up(nf, BF16_ROWS)
    c, s = _angles(np.arange(nf)[:, None] * np.arange(n)[None, :], n)
    fwd = np.zeros((2 * nfp, n))
    fwd[:nf], fwd[nfp:nfp + nf] = c, -s
    wgt = np.full((nf,), 2.0)
    wgt[0] = wgt[-1] = 1.0
    inv = np.zeros((l, 2 * nfp))
    inv[:, :nf], inv[:, nfp:nfp + nf] = (c[:, :l] * wgt[:, None]).T, (-s[:, :l] * wgt[:, None]).T
    bf = lambda a: jnp.asarray(a, jnp.float32).astype(jnp.bfloat16)
    return nfp, bf(fwd), bf(inv)


def _hyena_short_kernel(p_ref, k_ref, s_ref, d_ref, fwd_ref, inv_ref, o_ref, *, l, nfp, order):
    fwd = fwd_ref[...]
    u = p_ref[0, 0]
    for o in range(order):
        kn = (k_ref[o] * s_ref[o, 0:1, :]).astype(jnp.bfloat16)
        kf = jnp.dot(fwd, kn, preferred_element_type=jnp.float32)
        uf = jnp.dot(fwd[:, :l], u.astype(jnp.bfloat16), preferred_element_type=jnp.float32)
        k_re, k_im, u_re, u_im = kf[:nfp], kf[nfp:], uf[:nfp], uf[nfp:]
        p = jnp.concatenate([u_re * k_re - u_im * k_im, u_re * k_im + u_im * k_re], axis=0)
        y = jnp.dot(inv_ref[...], p.astype(jnp.bfloat16), preferred_element_type=jnp.float32)
        u = p_ref[o + 1, 0] * (y + u * d_ref[o:o + 1, :])
    o_ref[0] = u


def _hyena_short(proj, k, inv_s, d_bias, e):
    nt, b, l, _ = proj.shape
    order = k.shape[0]
    nfp, fwd, inv = _dft_tables(l)
    return pl.pallas_call(
        functools.partial(_hyena_short_kernel, l=l, nfp=nfp, order=order),
        grid=(b,),
        in_specs=[pl.BlockSpec((nt, 1, l, e), lambda bb: (0, bb, 0, 0)),
                  pl.BlockSpec((order, 2 * l, e), lambda bb: (0, 0, 0)),
                  pl.BlockSpec((order, SUBLANES, e), lambda bb: (0, 0, 0)),
                  pl.BlockSpec((order, e), lambda bb: (0, 0)),
                  pl.BlockSpec(fwd.shape, lambda bb: (0, 0)),
                  pl.BlockSpec(inv.shape, lambda bb: (0, 0))],
        out_specs=pl.BlockSpec((1, l, e), lambda bb: (bb, 0, 0)),
        out_shape=jax.ShapeDtypeStruct((b, l, e), jnp.float32),
        compiler_params=_params("parallel"),
        name="hyena_short",
    )(proj, k, inv_s, d_bias, fwd, inv)


def kernel(x, c, ctx, c_ctx, w_mod, b_mod, ln_g, ln_b, rg_w_in, rg_conv_w, rg_conv_b, rg_w_r, rg_b_r, rg_w_i, rg_b_i, rg_lambda, rg_w_out, hy_w_in, hy_conv_w, hy_conv_b, hy_f_w1, hy_f_b1, hy_f_w2, hy_f_b2, hy_f_w3, hy_f_freq, hy_d, hy_w_out):
    bsz, seq, d = x.shape
    depth = w_mod.shape[0]
    e = rg_w_out.shape[1]
    alpha = (2 * depth) ** 0.25
    assert bsz + 1 <= SUBLANES

    cvec = jnp.zeros((SUBLANES, d), jnp.float32).at[:bsz].set(c).at[bsz].set(c_ctx)
    mod = _modulation(cvec, w_mod, b_mod)

    def mod_rows(layer, part, for_ctx):
        v = mod[layer, :, part * d:(part + 1) * d]
        v = jnp.broadcast_to(v[bsz], (bsz, d)) if for_ctx else v[:bsz]
        return v.reshape(bsz, 1, d)

    hy_cache = {}
    fft = None

    def hyena_filters(occ, l):
        if (occ, l) not in hy_cache:
            zz, tcol = _filter_positions(l, hy_f_w1.shape[-1])
            hid = _filter_hidden(zz, hy_f_w1[occ], hy_f_b1[occ], hy_f_w2[occ], hy_f_b2[occ], hy_f_freq[occ])
            hy_cache[(occ, l)] = _filters(hid, tcol, hy_f_w3[occ], e)
        return hy_cache[(occ, l)]

    for i in range(depth):
        kind = i % N_MIXERS
        occ = i // N_MIXERS
        need_ctx_out = any(j % N_MIXERS == 0 for j in range(i + 1, depth))
        col_major = occ % 2 == 1
        use_ctx = kind == 0 or need_ctx_out
        shift, scale, gate = (mod_rows(i, p, False) for p in range(3))
        if use_ctx:
            shift_c, scale_c, gate_c = (mod_rows(i, p, True) for p in range(3))

        if kind == 0:
            w_in = rg_w_in[occ].astype(jnp.bfloat16)
            w_out = rg_w_out[occ].astype(jnp.bfloat16)
            proj = _inproj(x, shift, scale, w_in, rg_conv_w[occ], rg_conv_b[occ],
                           e=e, pad_l=1, col_major=col_major)
            w_in_c = w_in if need_ctx_out else w_in[:, :e]
            proj_c = _inproj(ctx, shift_c, scale_c, w_in_c, rg_conv_w[occ], rg_conv_b[occ],
                             e=e, pad_l=1, col_major=False)
            zeros = jnp.zeros((bsz, 1, e), jnp.float32)
            gates = [(_gate_weights(rg_w_r[occ, dd], rg_w_i[occ, dd]), rg_b_r[occ, dd], rg_b_i[occ, dd],
                      rg_lambda[occ, dd]) for dd in range(2)]
            hc_f, st_f = _rg_scan(proj_c, zeros, *gates[0], reverse=False, write_seq=need_ctx_out)
            h_f, _ = _rg_scan(proj, st_f, *gates[0], reverse=False)
            y_c, st_b = _rg_scan(proj_c, zeros, *gates[1], reverse=True, prev=hc_f, write_seq=need_ctx_out)
            y, _ = _rg_scan(proj, st_b, *gates[1], reverse=True, prev=h_f)
            g_group = 1
        else:
            w_in = hy_w_in[occ].astype(jnp.bfloat16)
            w_out = hy_w_out[occ].astype(jnp.bfloat16)
            proj = _inproj(x, shift, scale, w_in, hy_conv_w[occ], hy_conv_b[occ],
                           e=e, pad_l=1, col_major=col_major)
            k, inv_s = hyena_filters(occ, seq)
            if fft is None:
                tabs = _fft_tables(2 * seq // FFT_N2)
                fft = (tabs,) + tuple(_twiddled_dfts(tabs))
            y = _hyena_long(proj, k, inv_s, hy_d[occ], *fft)
            if need_ctx_out:
                proj_c = _inproj(ctx, shift_c, scale_c, w_in, hy_conv_w[occ], hy_conv_b[occ],
                                 e=e, pad_l=1, col_major=False)
                k_c, inv_s_c = hyena_filters(occ, ctx.shape[1])
                y_c = _hyena_short(proj_c, k_c, inv_s_c, hy_d[occ], e)
            g_group = 3

        x = _outproj(y, proj, g_group, w_out, x, gate, ln_g[i], ln_b[i], alpha=alpha, col_major=col_major)
        if need_ctx_out:
            ctx = _outproj(y_c, proj_c, g_group, w_out, ctx, gate_c, ln_g[i], ln_b[i],
                           alpha=alpha, col_major=False)
    return x
```

```python
import functools
import math

import numpy as np
import jax
import jax.numpy as jnp
from jax import lax
from jax.experimental import pallas as pl
from jax.experimental.pallas import tpu as pltpu

GRID_W = 64
N_MIXERS = 2
RG_C = 8.0
RG_GATE_GROUP = 4
HY_EMB_BANDS = 8
HY_FAST_DECAY_PCT = 0.3
HY_SLOW_DECAY_PCT = 1.5
HY_DECAY_TARGET = 1e-2
HY_MOD_SHIFT = 0.05
LN_EPS = 1e-5

SUBLANES = 8
BF16_ROWS = 16
FFT_N2 = 256
FFT_J = SUBLANES
FFT_EC = 768
STREAM_TILE = 512
LANE_CHUNK = 512
VMEM_LIMIT = 56 * 1024 * 1024

_HIGHEST = lax.Precision.HIGHEST


def _round_up(a, m):
    return (a + m - 1) // m * m


def _params(*sem):
    return pltpu.CompilerParams(dimension_semantics=sem, vmem_limit_bytes=VMEM_LIMIT)


def _sigmoid(v):
    return 1.0 / (1.0 + jnp.exp(-v))


def _mod_kernel(c_ref, w_ref, b_ref, o_ref):
    cv = c_ref[...]
    act = cv * _sigmoid(cv)
    o_ref[0] = jnp.dot(act, w_ref[0], preferred_element_type=jnp.float32,
                       precision=_HIGHEST) + b_ref[0]


def _modulation(cvec, w_mod, b_mod):
    depth, d, d3 = w_mod.shape
    nblk = d3 // d
    return pl.pallas_call(
        _mod_kernel,
        grid=(depth, nblk),
        in_specs=[pl.BlockSpec((SUBLANES, d), lambda l, n: (0, 0)),
                  pl.BlockSpec((1, d, d), lambda l, n: (l, 0, n)),
                  pl.BlockSpec((1, 1, d), lambda l, n: (l, 0, n))],
        out_specs=pl.BlockSpec((1, SUBLANES, d), lambda l, n: (l, 0, n)),
        out_shape=jax.ShapeDtypeStruct((depth, SUBLANES, d3), jnp.float32),
        compiler_params=_params("parallel", "parallel"),
        name="modulation",
    )(cvec, w_mod, b_mod.reshape(depth, 1, d3))


def _stream_view(x, col_major):
    b, l, d = x.shape
    if col_major:
        return x.reshape(b, l // GRID_W, GRID_W * d)
    return x


def _stream_specs(l, d, tm, col_major, nb_grid_prefix):
    def wrap(fn):
        return lambda *g: fn(*g[nb_grid_prefix:])
    if col_major:
        rows = l // GRID_W
        ncol = tm // rows
        assert tm == rows * ncol
        main = pl.BlockSpec((1, rows, ncol * d), wrap(lambda b, i: (b, 0, i)))
        prev = pl.BlockSpec((1, SUBLANES, d),
                            wrap(lambda b, i: (b, rows // SUBLANES - 1, jnp.maximum(i * ncol - 1, 0))))
        nxt = pl.BlockSpec((1, SUBLANES, d),
                           wrap(lambda b, i: (b, 0, jnp.minimum((i + 1) * ncol, GRID_W - 1))))
    else:
        per = tm // SUBLANES
        last = l // SUBLANES - 1
        main = pl.BlockSpec((1, tm, d), wrap(lambda b, i: (b, i, 0)))
        prev = pl.BlockSpec((1, SUBLANES, d), wrap(lambda b, i: (b, jnp.maximum(i * per - 1, 0), 0)))
        nxt = pl.BlockSpec((1, SUBLANES, d), wrap(lambda b, i: (b, jnp.minimum((i + 1) * per, last), 0)))
    return main, prev, nxt


def _tile_rows(l, col_major):
    tm = min(l, STREAM_TILE)
    if col_major:
        rows = l // GRID_W
        return max(tm // rows, 1) * rows
    return tm


def _tile_tokens(ref, d):
    blk = ref[0]
    ncol = blk.shape[1] // d
    if ncol == 1:
        return blk
    return jnp.concatenate([blk[:, j * d:(j + 1) * d] for j in range(ncol)], axis=0)


def _inproj_kernel(x_ref, xp_ref, xn_ref, sh_ref, sc_ref, w_ref, cw_ref, cb_ref, o_ref, u_scr,
                   *, n_conv, conv_k, pad_l, tm):
    n = pl.program_id(0)
    i = pl.program_id(2)
    nt = pl.num_programs(2)
    d, e = w_ref.shape
    scale = 1.0 + sc_ref[0]
    shift = sh_ref[0]
    h_main = _tile_tokens(x_ref, d) * scale + shift
    chunks = [slice(c0, min(c0 + LANE_CHUNK, e)) for c0 in range(0, e, LANE_CHUNK)]

    @pl.when(n < n_conv)
    def _():
        h_prev = xp_ref[0] * scale + shift
        h_next = xn_ref[0] * scale + shift
        hh = jnp.concatenate([h_prev, h_main, h_next], axis=0).astype(jnp.bfloat16)
        row = lax.broadcasted_iota(jnp.int32, (tm + 2 * SUBLANES, LANE_CHUNK), 0)
        outside = ((row < SUBLANES) & (i == 0)) | ((row >= tm + SUBLANES) & (i == nt - 1))
        for sl in chunks:
            u = jnp.dot(hh, w_ref[:, sl], preferred_element_type=jnp.float32)
            u_scr[:, sl] = jnp.where(outside[:, :u.shape[1]], 0.0, u)
            acc = cb_ref[:, sl] + cw_ref[0:1, sl] * u_scr[pl.ds(SUBLANES - pad_l, tm), sl]
            for k in range(1, conv_k):
                acc = acc + cw_ref[k:k + 1, sl] * u_scr[pl.ds(SUBLANES - pad_l + k, tm), sl]
            o_ref[0, 0, :, sl] = acc.astype(o_ref.dtype)

    @pl.when(n >= n_conv)
    def _():
        hb = h_main.astype(jnp.bfloat16)
        for sl in chunks:
            o_ref[0, 0, :, sl] = jnp.dot(hb, w_ref[:, sl],
                                         preferred_element_type=jnp.float32).astype(o_ref.dtype)


def _inproj(x, shift, scale, w_bf16, conv_w, conv_b, *, e, pad_l, col_major):
    b, l, d = x.shape
    nt_groups = w_bf16.shape[1] // e
    conv_k = conv_w.shape[0]
    n_conv = conv_w.shape[1] // e
    tm = _tile_rows(l, col_major)
    ntile = l // tm
    main, prev, nxt = _stream_specs(l, d, tm, col_major, 1)
    xv = _stream_view(x, col_major)
    kern = functools.partial(_inproj_kernel, n_conv=n_conv, conv_k=conv_k, pad_l=pad_l, tm=tm)
    return pl.pallas_call(
        kern,
        grid=(nt_groups, b, ntile),
        in_specs=[main, prev, nxt,
                  pl.BlockSpec((1, 1, d), lambda n, bb, i: (bb, 0, 0)),
                  pl.BlockSpec((1, 1, d), lambda n, bb, i: (bb, 0, 0)),
                  pl.BlockSpec((d, e), lambda n, bb, i: (0, n)),
                  pl.BlockSpec((conv_k, e), lambda n, bb, i: (0, jnp.minimum(n, n_conv - 1))),
                  pl.BlockSpec((1, e), lambda n, bb, i: (0, jnp.minimum(n, n_conv - 1)))],
        out_specs=pl.BlockSpec((1, 1, tm, e), lambda n, bb, i: (n, bb, i, 0)),
        out_shape=jax.ShapeDtypeStruct((nt_groups, b, l, e), jnp.bfloat16),
        scratch_shapes=[pltpu.VMEM((tm + 2 * SUBLANES, e), jnp.float32)],
        compiler_params=_params("parallel", "parallel", "parallel"),
        name="inproj_conv",
    )(xv, xv, xv, shift, scale, w_bf16, conv_w, conv_b.reshape(1, -1))


def _rg_scan_kernel(*refs, reverse, add_prev, write_seq, tm, gw, ngroups):
    it = iter(refs)
    xc_ref = next(it)
    prev_ref = next(it) if add_prev else None
    h0_ref, wg_ref, br_ref, bi_ref, lam_ref = next(it), next(it), next(it), next(it), next(it)
    o_ref = next(it) if write_seq else None
    st_ref = next(it)
    a_scr, b_scr, carry = next(it), next(it), next(it)

    i = pl.program_id(1)
    nchunk = pl.num_programs(1)

    @pl.when(i == 0)
    def _():
        carry[...] = h0_ref[0]

    lam = lam_ref[...]
    neg = -lam
    softplus = jnp.maximum(neg, 0.0) + jnp.log(1.0 + jnp.exp(-jnp.abs(neg)))
    rate = (0.5 * RG_C) * softplus
    for g in range(ngroups):
        sl = slice(g * gw, (g + 1) * gw)
        xg = xc_ref[0, 0, :, sl]
        pre = jnp.dot(xg, wg_ref[g], preferred_element_type=jnp.float32)
        t_r = jnp.tanh(pre[:, :gw] + br_ref[:, sl])
        t_i = jnp.tanh(pre[:, gw:] + bi_ref[:, sl])
        neg_log_a = rate[:, sl] * t_r + rate[:, sl]
        a = jnp.exp(-neg_log_a)
        a_scr[:, sl] = a
        s = jnp.tanh(neg_log_a) * (a * a + 1.0)
        mult = jnp.where(s > 0.0, s * lax.rsqrt(s), 0.0)
        b_scr[:, sl] = (0.5 * mult) * ((t_i + 1.0) * xg.astype(jnp.float32))

    e = a_scr.shape[1]
    row = lax.broadcasted_iota(jnp.int32, (SUBLANES, e), 0)
    nblk = tm // BF16_ROWS

    def scan8(r0, c):
        a = a_scr[pl.ds(r0, SUBLANES), :]
        h = b_scr[pl.ds(r0, SUBLANES), :]
        for dist in (1, 2, 4):
            shift = (SUBLANES - dist) if reverse else dist
            valid = (row + dist < SUBLANES) if reverse else (row >= dist)
            a_sh = jnp.where(valid, pltpu.roll(a, shift, 0), 1.0)
            h_sh = jnp.where(valid, pltpu.roll(h, shift, 0), 0.0)
            h = a * h_sh + h
            a = a * a_sh
        h = a * c + h
        return h, (h[0:1, :] if reverse else h[SUBLANES - 1:SUBLANES, :])

    def block(k, c):
        kk = (nblk - 1 - k) if reverse else k
        r0 = pl.multiple_of(kk * BF16_ROWS, BF16_ROWS)
        halves = [None, None]
        for hh in ((1, 0) if reverse else (0, 1)):
            halves[hh], c = scan8(r0 + hh * SUBLANES, c)
        if write_seq:
            h = jnp.concatenate(halves, axis=0)
            if add_prev:
                h = h + prev_ref[0, pl.ds(r0, BF16_ROWS), :].astype(jnp.float32)
            o_ref[0, pl.ds(r0, BF16_ROWS), :] = h.astype(o_ref.dtype)
        return c

    c_fin = lax.fori_loop(0, nblk, block, carry[...], unroll=2)
    carry[...] = c_fin

    @pl.when(i == nchunk - 1)
    def _():
        st_ref[0] = c_fin


def _rg_scan(proj, h0, wg, b_r, b_i, lam, *, reverse, prev=None, write_seq=True):
    _, b, l, e = proj.shape
    tm = min(l, 256)
    nchunk = l // tm
    ngroups, gw, _ = wg.shape
    pos = (lambda i: nchunk - 1 - i) if reverse else (lambda i: i)
    add_prev = prev is not None
    in_specs = [pl.BlockSpec((1, 1, tm, e), lambda bb, i: (0, bb, pos(i), 0))]
    args = [proj]
    if add_prev:
        in_specs.append(pl.BlockSpec((1, tm, e), lambda bb, i: (bb, pos(i), 0)))
        args.append(prev)
    in_specs += [pl.BlockSpec((1, 1, e), lambda bb, i: (bb, 0, 0)),
                 pl.BlockSpec((ngroups, gw, 2 * gw), lambda bb, i: (0, 0, 0)),
                 pl.BlockSpec((1, e), lambda bb, i: (0, 0)),
                 pl.BlockSpec((1, e), lambda bb, i: (0, 0)),
                 pl.BlockSpec((1, e), lambda bb, i: (0, 0))]
    args += [h0, wg, 0.5 * b_r.reshape(1, e), 0.5 * b_i.reshape(1, e), lam.reshape(1, e)]
    out_specs, out_shape = [], []
    if write_seq:
        out_specs.append(pl.BlockSpec((1, tm, e), lambda bb, i: (bb, pos(i), 0)))
        out_shape.append(jax.ShapeDtypeStruct((b, l, e), jnp.bfloat16))
    out_specs.append(pl.BlockSpec((1, 1, e), lambda bb, i: (bb, 0, 0)))
    out_shape.append(jax.ShapeDtypeStruct((b, 1, e), jnp.float32))
    kern = functools.partial(_rg_scan_kernel, reverse=reverse, add_prev=add_prev,
                             write_seq=write_seq, tm=tm, gw=gw, ngroups=ngroups)
    res = pl.pallas_call(
        kern,
        grid=(b, nchunk),
        in_specs=in_specs,
        out_specs=out_specs,
        out_shape=out_shape,
        scratch_shapes=[pltpu.VMEM((tm, e), jnp.float32), pltpu.VMEM((tm, e), jnp.float32),
                        pltpu.VMEM((1, e), jnp.float32)],
        compiler_params=_params("parallel", "arbitrary"),
        name="rg_scan_bwd" if reverse else "rg_scan_fwd",
    )(*args)
    if write_seq:
        return res[0], res[1]
    return None, res[0]


def _gate_weights(w_r, w_i):
    nb, bw, _ = w_r.shape
    g = RG_GATE_GROUP
    ng = nb // g
    eye = jnp.eye(g, dtype=w_r.dtype)

    def bd(w):
        w = w.reshape(ng, g, bw, bw)
        return jnp.einsum("nhij,hk->nhikj", w, eye).reshape(ng, g * bw, g * bw)

    return (0.5 * jnp.concatenate([bd(w_r), bd(w_i)], axis=-1)).astype(jnp.bfloat16)


def _outproj_kernel(y_ref, g_ref, w_ref, x_ref, gate_ref, lg_ref, lb_ref, o_ref, *, alpha):
    d = w_ref.shape[1]
    g = g_ref[0, 0].astype(jnp.float32)
    z = (y_ref[0].astype(jnp.float32) * (g * _sigmoid(g))).astype(jnp.bfloat16)
    out = jnp.dot(z, w_ref[...], preferred_element_type=jnp.float32)
    r = alpha * _tile_tokens(x_ref, d) + gate_ref[0] * out
    mu = jnp.mean(r, axis=-1, keepdims=True)
    cen = r - mu
    var = jnp.mean(cen * cen, axis=-1, keepdims=True)
    res = cen * lax.rsqrt(var + LN_EPS) * lg_ref[...] + lb_ref[...]
    rows = o_ref.shape[1]
    for j in range(o_ref.shape[2] // d):
        o_ref[0, :, j * d:(j + 1) * d] = res[j * rows:(j + 1) * rows]


def _outproj(y, proj, g_group, w_bf16, x, gate, ln_g, ln_b, *, alpha, col_major):
    b, l, e = y.shape
    d = x.shape[-1]
    tm = _tile_rows(l, col_major)
    ntile = l // tm
    main, _, _ = _stream_specs(l, d, tm, col_major, 0)
    xv = _stream_view(x, col_major)
    out = pl.pallas_call(
        functools.partial(_outproj_kernel, alpha=alpha),
        grid=(b, ntile),
        in_specs=[pl.BlockSpec((1, tm, e), lambda bb, i: (bb, i, 0)),
                  pl.BlockSpec((1, 1, tm, e), lambda bb, i: (g_group, bb, i, 0)),
                  pl.BlockSpec((e, d), lambda bb, i: (0, 0)),
                  main,
                  pl.BlockSpec((1, 1, d), lambda bb, i: (bb, 0, 0)),
                  pl.BlockSpec((1, d), lambda bb, i: (0, 0)),
                  pl.BlockSpec((1, d), lambda bb, i: (0, 0))],
        out_specs=main,
        out_shape=jax.ShapeDtypeStruct(xv.shape, jnp.float32),
        compiler_params=_params("parallel", "parallel"),
        name="outproj_ln",
    )(y, proj, w_bf16, xv, gate, ln_g.reshape(1, d), ln_b.reshape(1, d))
    return out.reshape(x.shape)


def _filter_positions(l, fh):
    f32 = jnp.float32
    t = jnp.linspace(0.0, 1.0, l, dtype=f32)[:, None]
    bands = jnp.linspace(1e-4, HY_EMB_BANDS - 1, HY_EMB_BANDS, dtype=f32)
    w = (2.0 * math.pi) * jnp.arange(l, dtype=f32)[:, None] / l
    z = jnp.concatenate([t, jnp.cos(bands * w), -jnp.sin(bands * w)], axis=-1)
    z_rev = jnp.concatenate([z[:1], z[:0:-1]], axis=0)
    pad = lambda a: jnp.pad(a, ((0, 0), (0, fh - a.shape[1])))
    return jnp.concatenate([pad(z), pad(z_rev)], axis=1), jnp.stack([z[:, 0:1], z_rev[:, 0:1]])


def _hidden_kernel(z_ref, w1_ref, b1_ref, w2_ref, b2_ref, fr_ref, o_ref):
    fr = fr_ref[...]
    h = jnp.dot(z_ref[...], w1_ref[...], preferred_element_type=jnp.float32, precision=_HIGHEST)
    h = jnp.sin(fr * (h + b1_ref[...]))
    h = jnp.dot(h, w2_ref[...], preferred_element_type=jnp.float32, precision=_HIGHEST)
    o_ref[...] = jnp.sin(fr * (h + b2_ref[...]))


def _filter_hidden(zz, w1, b1, w2, b2, freq):
    n, zp = zz.shape
    fh = w1.shape[1]
    tr = min(n, 1024)
    both = lambda v: jnp.concatenate([v, v]).reshape(1, 2 * fh)
    blockdiag = lambda w: jnp.kron(jnp.eye(2, dtype=w.dtype), w)
    w1p = blockdiag(jnp.pad(w1, ((0, fh - w1.shape[0]), (0, 0))))
    full = lambda shape: pl.BlockSpec(shape, lambda i: (0,) * len(shape))
    return pl.pallas_call(
        _hidden_kernel,
        grid=(n // tr,),
        in_specs=[pl.BlockSpec((tr, zp), lambda i: (i, 0)), full((zp, zp)), full((1, zp)),
                  full((zp, zp)), full((1, zp)), full((1, zp))],
        out_specs=pl.BlockSpec((tr, zp), lambda i: (i, 0)),
        out_shape=jax.ShapeDtypeStruct((n, zp), jnp.float32),
        compiler_params=_params("parallel"),
        name="filter_hidden",
    )(zz, w1p, both(b1), blockdiag(w2), both(b2), both(freq))


def _split_bf16(a):
    hi = a.astype(jnp.bfloat16)
    return hi, (a - hi.astype(jnp.float32)).astype(jnp.bfloat16)


def _filter_kernel(hid_ref, t_ref, w3_ref, ad_ref, k_ref, s_ref, *, tr, n_total):
    d = pl.program_id(1)
    i = pl.program_id(2)
    last = (d == pl.num_programs(1) - 1) & (i == pl.num_programs(2) - 1)
    hi, lo = _split_bf16(hid_ref[...])
    h = jnp.dot(jnp.concatenate([hi, lo, hi], axis=1), w3_ref[0, 0], preferred_element_type=jnp.float32)
    window = jnp.exp(-t_ref[0] * ad_ref[...]) + HY_MOD_SHIFT
    row = lax.broadcasted_iota(jnp.int32, h.shape, 0)
    gap = (row == 0) & (d == 1) & (i == 0)
    k = jnp.where(gap, 0.0, h * window)
    k_ref[0] = k

    @pl.when((d == 0) & (i == 0))
    def _():
        s_ref[...] = jnp.zeros_like(s_ref)

    s_ref[0] += jnp.sum(jnp.abs(k).reshape(tr // SUBLANES, SUBLANES, -1), axis=0)

    @pl.when(last)
    def _():
        tot = jnp.sum(s_ref[0], axis=0, keepdims=True)
        s_ref[0] = jnp.broadcast_to(1.0 / (tot * n_total), s_ref.shape[1:])


def _filter_weights(w3, fh, e):
    order = w3.shape[1] // (2 * e)
    max_decay = math.log(HY_DECAY_TARGET) / HY_FAST_DECAY_PCT
    min_decay = math.log(HY_DECAY_TARGET) / HY_SLOW_DECAY_PCT
    absdelta = jnp.abs(jnp.linspace(min_decay, max_decay, e, dtype=jnp.float32)).reshape(1, e)
    w3 = w3.reshape(fh, order, 2, e).transpose(1, 2, 0, 3)
    w3 = jnp.stack([jnp.pad(w3[:, 0], ((0, 0), (0, fh), (0, 0))),
                    jnp.pad(w3[:, 1], ((0, 0), (fh, 0), (0, 0)))], axis=1)
    w_hi, w_lo = _split_bf16(w3)
    return jnp.concatenate([w_hi, w_hi, w_lo], axis=2), absdelta


def _filter_fft_a_kernel(hid_ref, t_ref, w3_ref, ad_ref, m_ref, re_ref, im_ref, s_ref, *, n_total):
    j = pl.program_id(2)
    k1h, rows, fh2 = hid_ref.shape
    ec = ad_ref.shape[1]
    hi, lo = _split_bf16(hid_ref[...].reshape(k1h * rows, fh2))
    stacked = jnp.concatenate([hi, lo, hi], axis=1)
    halves = []
    for d in range(2):
        h = jnp.dot(stacked, w3_ref[0, d], preferred_element_type=jnp.float32)
        window = jnp.exp(-t_ref[d].reshape(k1h * rows, 1) * ad_ref[...]) + HY_MOD_SHIFT
        k = h * window
        if d == 1:
            row = lax.broadcasted_iota(jnp.int32, k.shape, 0)
            k = jnp.where((row == 0) & (j == 0), 0.0, k)
        halves.append(k)

    @pl.when(j == 0)
    def _():
        s_ref[...] = jnp.zeros_like(s_ref)

    k = jnp.concatenate(halves, axis=0)
    s_ref[0] += jnp.sum(jnp.abs(k).reshape(-1, SUBLANES, ec), axis=0)

    @pl.when(j == pl.num_programs(2) - 1)
    def _():
        tot = jnp.sum(s_ref[0], axis=0, keepdims=True)
        s_ref[0] = jnp.broadcast_to(1.0 / (tot * n_total), s_ref.shape[1:])

    k3 = k.reshape(2 * k1h, rows, ec)
    half = m_ref.shape[0] // 2
    f1p = half // FFT_J
    re_parts, im_parts = [], []
    for h in range(rows // FFT_J):
        x = k3[:, h * FFT_J:(h + 1) * FFT_J, :].reshape(2 * k1h * FFT_J, ec)
        r = jnp.dot(m_ref[...], x.astype(jnp.bfloat16), preferred_element_type=jnp.float32)
        re_parts.append(r[:half].reshape(f1p, FFT_J, ec))
        im_parts.append(r[half:].reshape(f1p, FFT_J, ec))
    re_ref[0] = jnp.concatenate(re_parts, axis=1).astype(jnp.bfloat16)
    im_ref[0] = jnp.concatenate(im_parts, axis=1).astype(jnp.bfloat16)


def _filter_fft_a(hid, tcol, w3, e, tabs):
    l, fh2 = hid.shape
    n2 = FFT_N2
    k1h = l // n2
    f1p = tabs["f1p"]
    mat = tabs["a_full"]
    w_stack, absdelta = _filter_weights(w3, fh2 // 2, e)
    order = w_stack.shape[0]
    ec = min(e, FFT_EC)
    out_spec = pl.BlockSpec((1, f1p, BF16_ROWS, ec), lambda o, cc, j: (o, 0, j, cc))
    shape = jax.ShapeDtypeStruct((order, f1p, n2, e), jnp.bfloat16)
    return pl.pallas_call(
        functools.partial(_filter_fft_a_kernel, n_total=float(2 * l)),
        grid=(order, e // ec, n2 // BF16_ROWS),
        in_specs=[pl.BlockSpec((k1h, BF16_ROWS, fh2), lambda o, cc, j: (0, j, 0)),
                  pl.BlockSpec((2, k1h, BF16_ROWS, 1), lambda o, cc, j: (0, 0, j, 0)),
                  pl.BlockSpec((1, 2, w_stack.shape[2], ec), lambda o, cc, j: (o, 0, 0, cc)),
                  pl.BlockSpec((1, ec), lambda o, cc, j: (0, cc)),
                  pl.BlockSpec(mat.shape, lambda o, cc, j: (0, 0))],
        out_specs=[out_spec, out_spec,
                   pl.BlockSpec((1, SUBLANES, ec), lambda o, cc, j: (o, 0, cc))],
        out_shape=[shape, shape, jax.ShapeDtypeStruct((order, SUBLANES, e), jnp.float32)],
        compiler_params=_params("parallel", "parallel", "arbitrary"),
        name="filter_fft_a",
    )(hid.reshape(k1h, n2, fh2), tcol.reshape(2, k1h, n2, 1), w_stack, absdelta, mat)


def _filters(hid, tcol, w3, e):
    l, fh2 = hid.shape
    n = 2 * l
    tr = min(l, 512)
    nl = l // tr
    w_stack, absdelta = _filter_weights(w3, fh2 // 2, e)
    order = w_stack.shape[0]
    return pl.pallas_call(
        functools.partial(_filter_kernel, tr=tr, n_total=float(n)),
        grid=(order, 2, nl),
        in_specs=[pl.BlockSpec((tr, fh2), lambda o, d, i: (i, 0)),
                  pl.BlockSpec((1, tr, 1), lambda o, d, i: (d, i, 0)),
                  pl.BlockSpec((1, 1, 3 * fh2, e), lambda o, d, i: (o, d, 0, 0)),
                  pl.BlockSpec((1, e), lambda o, d, i: (0, 0))],
        out_specs=[pl.BlockSpec((1, tr, e), lambda o, d, i: (o, d * nl + i, 0)),
                   pl.BlockSpec((1, SUBLANES, e), lambda o, d, i: (o, 0, 0))],
        out_shape=[jax.ShapeDtypeStruct((order, n, e), jnp.float32),
                   jax.ShapeDtypeStruct((order, SUBLANES, e), jnp.float32)],
        compiler_params=_params("parallel", "arbitrary", "arbitrary"),
        name="filter_gen",
    )(hid, tcol, w_stack, absdelta)


def _angles(num, den):
    ang = 2.0 * np.pi * (np.asarray(num, np.int64) % den).astype(np.float64) / den
    return np.cos(ang), np.sin(ang)


@functools.lru_cache(maxsize=None)
def _fft_tables(n1):
    n2 = FFT_N2
    n = n1 * n2
    f1 = n1 // 2 + 1
    f1p = _round_up(f1, 2)
    fr = np.arange(f1)[:, None]
    eye = np.eye(FFT_J)

    def stage_a(k1):
        c, s = _angles(fr * np.arange(k1)[None, :], n1)
        m = np.zeros((2 * f1p, k1))
        m[:f1], m[f1p:f1p + f1] = c, -s
        return np.kron(m, eye)

    wgt = np.full((f1,), 2.0)
    wgt[0] = wgt[-1] = 1.0
    c, s = _angles(np.arange(n1 // 2)[:, None] * np.arange(f1)[None, :], n1)
    m_c = np.zeros((n1 // 2, 2 * f1p))
    m_c[:, :f1], m_c[:, f1p:f1p + f1] = c * wgt, -s * wgt

    c2, s2 = _angles(np.arange(n2)[:, None] * np.arange(n2)[None, :], n2)
    cphi, sphi = _angles(fr * np.arange(n2)[None, :], n)
    phase = np.stack([cphi, sphi], axis=1)
    bf = lambda a: jnp.asarray(a, jnp.float32).astype(jnp.bfloat16)
    f32 = lambda a: jnp.asarray(a, jnp.float32)
    return dict(f1=f1, f1p=f1p, a_data=bf(stage_a(n1 // 2)), a_full=bf(stage_a(n1)),
                m_c=bf(np.kron(m_c, eye)), base_c=f32(c2), base_s=f32(s2),
                phase_row=f32(phase[:, :, None, :]), phase_col=f32(phase[..., None]))


def _twiddled_dft_kernel(c_ref, s_ref, row_ref, col_ref, mf_ref, mi_ref):
    c, s = c_ref[...], s_ref[...]
    n2 = c.shape[0]
    bf = jnp.bfloat16
    cr, sr = row_ref[0, 0], row_ref[0, 1]
    cf, sf = (c * cr - s * sr).astype(bf), (s * cr + c * sr).astype(bf)
    mf_ref[0, :n2, :n2] = cf
    mf_ref[0, :n2, n2:] = sf
    mf_ref[0, n2:, :n2] = -sf
    mf_ref[0, n2:, n2:] = cf
    cc, sc = col_ref[0, 0], col_ref[0, 1]
    ci, si = (c * cc - s * sc).astype(bf), (s * cc + c * sc).astype(bf)
    mi_ref[0, :n2, :n2] = ci
    mi_ref[0, :n2, n2:] = -si
    mi_ref[0, n2:, :n2] = si
    mi_ref[0, n2:, n2:] = ci


def _twiddled_dfts(tabs):
    f1 = tabs["f1"]
    n2 = FFT_N2
    shape = jax.ShapeDtypeStruct((f1, 2 * n2, 2 * n2), jnp.bfloat16)
    base = pl.BlockSpec((n2, n2), lambda f: (0, 0))
    out = pl.BlockSpec((1, 2 * n2, 2 * n2), lambda f: (f, 0, 0))
    return pl.pallas_call(
        _twiddled_dft_kernel,
        grid=(f1,),
        in_specs=[base, base, pl.BlockSpec((1, 2, 1, n2), lambda f: (f, 0, 0, 0)),
                  pl.BlockSpec((1, 2, n2, 1), lambda f: (f, 0, 0, 0))],
        out_specs=[out, out],
        out_shape=[shape, shape],
        compiler_params=_params("parallel"),
        name="twiddled_dft_tables",
    )(tabs["base_c"], tabs["base_s"], tabs["phase_row"], tabs["phase_col"])


def _fft_a_kernel(*refs, scaled):
    it = iter(refs)
    x_ref = next(it)
    sc_ref = next(it) if scaled else None
    m_ref, re_ref, im_ref = next(it), next(it), next(it)
    k1, ec = x_ref.shape[2], x_ref.shape[4]
    half = m_ref.shape[0] // 2
    f1p = half // FFT_J
    re_parts, im_parts = [], []
    xf = x_ref[0, 0].astype(jnp.float32)
    for h in range(x_ref.shape[3] // FFT_J):
        x = xf[:, h * FFT_J:(h + 1) * FFT_J, :].reshape(k1 * FFT_J, ec)
        if scaled:
            x = x * sc_ref[0]
        r = jnp.dot(m_ref[...], x.astype(jnp.bfloat16), preferred_element_type=jnp.float32)
        re_parts.append(r[:half].reshape(f1p, FFT_J, ec))
        im_parts.append(r[half:].reshape(f1p, FFT_J, ec))
    re_ref[0] = jnp.concatenate(re_parts, axis=1).astype(jnp.bfloat16)
    im_ref[0] = jnp.concatenate(im_parts, axis=1).astype(jnp.bfloat16)


def _fft_stage_a(arr, group, tabs, scale=None):
    _, bt, rows, e = arr.shape
    k1 = rows // FFT_N2
    f1p = tabs["f1p"]
    mat = tabs["a_data"] if k1 * FFT_J == tabs["a_data"].shape[1] else tabs["a_full"]
    assert mat.shape[1] == k1 * FFT_J
    ec = min(e, FFT_EC)
    av = arr.reshape(arr.shape[0], bt, k1, FFT_N2, e)
    in_specs = [pl.BlockSpec((1, 1, k1, BF16_ROWS, ec), lambda bb, j, cc: (group, bb, 0, j, cc))]
    args = [av]
    if scale is not None:
        in_specs.append(pl.BlockSpec((1, 1, ec), lambda bb, j, cc: (bb, 0, cc)))
        args.append(scale)
    in_specs.append(pl.BlockSpec(mat.shape, lambda bb, j, cc: (0, 0)))
    args.append(mat)
    out_spec = pl.BlockSpec((1, f1p, BF16_ROWS, ec), lambda bb, j, cc: (bb, 0, j, cc))
    shape = jax.ShapeDtypeStruct((bt, f1p, FFT_N2, e), jnp.bfloat16)
    return pl.pallas_call(
        functools.partial(_fft_a_kernel, scaled=scale is not None),
        grid=(bt, FFT_N2 // BF16_ROWS, e // ec),
        in_specs=in_specs,
        out_specs=[out_spec, out_spec],
        out_shape=[shape, shape],
        compiler_params=_params("parallel", "parallel", "parallel"),
        name="fft_stage_a",
    )(*args)


def _lane_chunks(e, width=256):
    return [slice(c0, min(c0 + width, e)) for c0 in range(0, e, width)]


def _fft_b_filter_kernel(re_ref, im_ref, sc_ref, mf_ref, kre_ref, kim_ref):
    for b in range(re_ref.shape[0]):
        for sl in _lane_chunks(re_ref.shape[3]):
            x = jnp.concatenate([re_ref[b, 0, :, sl], im_ref[b, 0, :, sl]], axis=0)
            u = jnp.dot(mf_ref[0], x, preferred_element_type=jnp.float32) * sc_ref[b, 0:1, sl]
            kre_ref[b, 0, :, sl] = u[:FFT_N2].astype(kre_ref.dtype)
            kim_ref[b, 0, :, sl] = u[FFT_N2:].astype(kim_ref.dtype)


def _fft_b_conv_kernel(re_ref, im_ref, kre_ref, kim_ref, mf_ref, mi_ref, ore_ref, oim_ref):
    for sl in _lane_chunks(re_ref.shape[3]):
        k_re = kre_ref[0, 0, :, sl].astype(jnp.float32)
        k_im = kim_ref[0, 0, :, sl].astype(jnp.float32)
        for b in range(re_ref.shape[0]):
            x = jnp.concatenate([re_ref[b, 0, :, sl], im_ref[b, 0, :, sl]], axis=0)
            u = jnp.dot(mf_ref[0], x, preferred_element_type=jnp.float32)
            u_re, u_im = u[:FFT_N2], u[FFT_N2:]
            p = jnp.concatenate([u_re * k_re - u_im * k_im, u_re * k_im + u_im * k_re], axis=0)
            v = jnp.dot(mi_ref[0], p.astype(jnp.bfloat16), preferred_element_type=jnp.float32)
            ore_ref[b, 0, :, sl] = v[:FFT_N2].astype(jnp.bfloat16)
            oim_ref[b, 0, :, sl] = v[FFT_N2:].astype(jnp.bfloat16)


def _fft_stage_b_filter(a_re, a_im, inv_s, mf, tabs):
    bt, _, n2, e = a_re.shape
    f1 = tabs["f1"]
    blk = pl.BlockSpec((bt, 1, n2, e), lambda f: (0, f, 0, 0))
    shape = jax.ShapeDtypeStruct((bt, f1, n2, e), jnp.bfloat16)
    return pl.pallas_call(
        _fft_b_filter_kernel,
        grid=(f1,),
        in_specs=[blk, blk, pl.BlockSpec(inv_s.shape, lambda f: (0, 0, 0)),
                  pl.BlockSpec((1, 2 * n2, 2 * n2), lambda f: (f, 0, 0))],
        out_specs=[blk, blk],
        out_shape=[shape, shape],
        compiler_params=_params("parallel"),
        name="fft_stage_b_filter",
    )(a_re, a_im, inv_s, mf)


def _fft_stage_b_conv(a_re, a_im, kf_re, kf_im, order_idx, mf, mi, tabs):
    bt, f1p, n2, e = a_re.shape
    f1 = tabs["f1"]
    fc = lambda f: jnp.minimum(f, f1 - 1)
    blk = pl.BlockSpec((bt, 1, n2, e), lambda f: (0, f, 0, 0))
    kblk = pl.BlockSpec((1, 1, n2, e), lambda f: (order_idx, fc(f), 0, 0))
    wblk = pl.BlockSpec((1, 2 * n2, 2 * n2), lambda f: (fc(f), 0, 0))
    shape = jax.ShapeDtypeStruct((bt, f1p, n2, e), jnp.bfloat16)
    return pl.pallas_call(
        _fft_b_conv_kernel,
        grid=(f1p,),
        in_specs=[blk, blk, kblk, kblk, wblk, wblk],
        out_specs=[blk, blk],
        out_shape=[shape, shape],
        compiler_params=_params("parallel"),
        name="fft_stage_b_conv",
    )(a_re, a_im, kf_re, kf_im, mf, mi)


def _fft_c_kernel(*refs, chain):
    re_ref, im_ref, m_ref, u_ref, mul_ref, d_ref = refs[:6]
    if chain:
        ma_ref, o_ref, are_ref, aim_ref = refs[6:]
    else:
        o_ref, = refs[6:]
    k1, ec = o_ref.shape[1], o_ref.shape[3]
    f1p = re_ref.shape[1]
    re = re_ref[0].astype(jnp.float32)
    im = im_ref[0].astype(jnp.float32)
    uf = u_ref[0, 0].astype(jnp.float32)
    mf = mul_ref[0, 0].astype(jnp.float32)
    parts, re_parts, im_parts = [], [], []
    for h in range(o_ref.shape[2] // FFT_J):
        sl = slice(h * FFT_J, (h + 1) * FFT_J)
        x = jnp.concatenate([re[:, sl, :].reshape(f1p * FFT_J, ec),
                             im[:, sl, :].reshape(f1p * FFT_J, ec)], axis=0)
        y = jnp.dot(m_ref[...], x.astype(jnp.bfloat16), preferred_element_type=jnp.float32)
        u = uf[:, sl, :].reshape(k1 * FFT_J, ec)
        mul = mf[:, sl, :].reshape(k1 * FFT_J, ec)
        z = (mul * (y + u * d_ref[...])).astype(jnp.bfloat16)
        parts.append(z.astype(jnp.float32).reshape(k1, FFT_J, ec))
        if chain:
            r = jnp.dot(ma_ref[...], z, preferred_element_type=jnp.float32)
            half = ma_ref.shape[0] // 2
            re_parts.append(r[:half].reshape(f1p, FFT_J, ec))
            im_parts.append(r[half:].reshape(f1p, FFT_J, ec))
    o_ref[0] = jnp.concatenate(parts, axis=1).astype(o_ref.dtype)
    if chain:
        are_ref[0] = jnp.concatenate(re_parts, axis=1).astype(jnp.bfloat16)
        aim_ref[0] = jnp.concatenate(im_parts, axis=1).astype(jnp.bfloat16)


def _fft_stage_c(b_re, b_im, u_arr, u_group, proj, mul_group, d_row, tabs, chain):
    bt, f1p, n2, e = b_re.shape
    l = u_arr.shape[2]
    k1 = l // n2
    ec = min(e, FFT_EC)
    uv = u_arr.reshape(u_arr.shape[0], bt, k1, n2, e)
    pv = proj.reshape(proj.shape[0], bt, k1, n2, e)
    mat = tabs["m_c"]
    fblk = pl.BlockSpec((1, f1p, BF16_ROWS, ec), lambda bb, j, cc: (bb, 0, j, cc))
    full = lambda m: pl.BlockSpec(m.shape, lambda bb, j, cc: (0, 0))
    in_specs = [fblk, fblk, full(mat),
                pl.BlockSpec((1, 1, k1, BF16_ROWS, ec), lambda bb, j, cc: (u_group, bb, 0, j, cc)),
                pl.BlockSpec((1, 1, k1, BF16_ROWS, ec), lambda bb, j, cc: (mul_group, bb, 0, j, cc)),
                pl.BlockSpec((1, ec), lambda bb, j, cc: (0, cc))]
    args = [b_re, b_im, mat, uv, pv, d_row]
    out_specs = [pl.BlockSpec((1, k1, BF16_ROWS, ec), lambda bb, j, cc: (bb, 0, j, cc))]
    out_shape = [jax.ShapeDtypeStruct((bt, k1, n2, e), jnp.bfloat16)]
    if chain:
        in_specs.append(full(tabs["a_data"]))
        args.append(tabs["a_data"])
        out_specs += [fblk, fblk]
        out_shape += [jax.ShapeDtypeStruct((bt, f1p, n2, e), jnp.bfloat16)] * 2
    res = pl.pallas_call(
        functools.partial(_fft_c_kernel, chain=chain),
        grid=(bt, n2 // BF16_ROWS, e // ec),
        in_specs=in_specs,
        out_specs=out_specs,
        out_shape=out_shape,
        compiler_params=_params("parallel", "parallel", "parallel"),
        name="fft_stage_ca" if chain else "fft_stage_c",
    )(*args)
    return (res[0].reshape(bt, l, e),) + tuple(res[1:])


def _hyena_long(proj, f_re, f_im, inv_s, d_bias, tabs, mf, mi):
    order, e = d_bias.shape
    kf_re, kf_im = _fft_stage_b_filter(f_re, f_im, inv_s, mf, tabs)
    u_arr, u_group = proj, 0
    a_re, a_im = _fft_stage_a(u_arr, u_group, tabs)
    for o in range(order):
        b_re, b_im = _fft_stage_b_conv(a_re, a_im, kf_re, kf_im, o, mf, mi, tabs)
        chain = o + 1 < order
        res = _fft_stage_c(b_re, b_im, u_arr, u_group, proj, o + 1, d_bias[o].reshape(1, e), tabs, chain)
        if chain:
            a_re, a_im = res[1:]
        u_arr, u_group = res[0][None], 0
    return res[0]


@functools.lru_cache(maxsize=None)
def _dft_tables(l):
    n = 2 * l
    nf = l + 1
    nfp = _round_up(nf, BF16_ROWS)
    c, s = _angles(np.arange(nf)[:, None] * np.arange(n)[None, :], n)
    fwd = np.zeros((2 * nfp, n))
    fwd[:nf], fwd[nfp:nfp + nf] = c, -s
    wgt = np.full((nf,), 2.0)
    wgt[0] = wgt[-1] = 1.0
    inv = np.zeros((l, 2 * nfp))
    inv[:, :nf], inv[:, nfp:nfp + nf] = (c[:, :l] * wgt[:, None]).T, (-s[:, :l] * wgt[:, None]).T
    bf = lambda a: jnp.asarray(a, jnp.float32).astype(jnp.bfloat16)
    return nfp, bf(fwd), bf(inv)


def _hyena_short_kernel(p_ref, k_ref, s_ref, d_ref, fwd_ref, inv_ref, o_ref, *, l, nfp, order):
    fwd = fwd_ref[...]
    u = p_ref[0, 0]
    for o in range(order):
        kn = (k_ref[o] * s_ref[o, 0:1, :]).astype(jnp.bfloat16)
        kf = jnp.dot(fwd, kn, preferred_element_type=jnp.float32)
        uf = jnp.dot(fwd[:, :l], u.astype(jnp.bfloat16), preferred_element_type=jnp.float32)
        k_re, k_im, u_re, u_im = kf[:nfp], kf[nfp:], uf[:nfp], uf[nfp:]
        p = jnp.concatenate([u_re * k_re - u_im * k_im, u_re * k_im + u_im * k_re], axis=0)
        y = jnp.dot(inv_ref[...], p.astype(jnp.bfloat16), preferred_element_type=jnp.float32)
        u = p_ref[o + 1, 0] * (y + u * d_ref[o:o + 1, :])
    o_ref[0] = u


def _hyena_short(proj, k, inv_s, d_bias, e):
    nt, b, l, _ = proj.shape
    order = k.shape[0]
    nfp, fwd, inv = _dft_tables(l)
    return pl.pallas_call(
        functools.partial(_hyena_short_kernel, l=l, nfp=nfp, order=order),
        grid=(b,),
        in_specs=[pl.BlockSpec((nt, 1, l, e), lambda bb: (0, bb, 0, 0)),
                  pl.BlockSpec((order, 2 * l, e), lambda bb: (0, 0, 0)),
                  pl.BlockSpec((order, SUBLANES, e), lambda bb: (0, 0, 0)),
                  pl.BlockSpec((order, e), lambda bb: (0, 0)),
                  pl.BlockSpec(fwd.shape, lambda bb: (0, 0)),
                  pl.BlockSpec(inv.shape, lambda bb: (0, 0))],
        out_specs=pl.BlockSpec((1, l, e), lambda bb: (bb, 0, 0)),
        out_shape=jax.ShapeDtypeStruct((b, l, e), jnp.float32),
        compiler_params=_params("parallel"),
        name="hyena_short",
    )(proj, k, inv_s, d_bias, fwd, inv)


def kernel(x, c, ctx, c_ctx, w_mod, b_mod, ln_g, ln_b, rg_w_in, rg_conv_w, rg_conv_b, rg_w_r, rg_b_r, rg_w_i, rg_b_i, rg_lambda, rg_w_out, hy_w_in, hy_conv_w, hy_conv_b, hy_f_w1, hy_f_b1, hy_f_w2, hy_f_b2, hy_f_w3, hy_f_freq, hy_d, hy_w_out):
    bsz, seq, d = x.shape
    depth = w_mod.shape[0]
    e = rg_w_out.shape[1]
    alpha = (2 * depth) ** 0.25
    assert bsz + 1 <= SUBLANES

    cvec = jnp.zeros((SUBLANES, d), jnp.float32).at[:bsz].set(c).at[bsz].set(c_ctx)
    mod = _modulation(cvec, w_mod, b_mod)

    def mod_rows(layer, part, for_ctx):
        v = mod[layer, :, part * d:(part + 1) * d]
        v = jnp.broadcast_to(v[bsz], (bsz, d)) if for_ctx else v[:bsz]
        return v.reshape(bsz, 1, d)

    fft = None

    def filter_hidden(occ, l):
        zz, tcol = _filter_positions(l, hy_f_w1.shape[-1])
        hid = _filter_hidden(zz, hy_f_w1[occ], hy_f_b1[occ], hy_f_w2[occ], hy_f_b2[occ], hy_f_freq[occ])
        return hid, tcol

    for i in range(depth):
        kind = i % N_MIXERS
        occ = i // N_MIXERS
        need_ctx_out = any(j % N_MIXERS == 0 for j in range(i + 1, depth))
        col_major = occ % 2 == 1
        use_ctx = kind == 0 or need_ctx_out
        shift, scale, gate = (mod_rows(i, p, False) for p in range(3))
        if use_ctx:
            shift_c, scale_c, gate_c = (mod_rows(i, p, True) for p in range(3))

        if kind == 0:
            w_in = rg_w_in[occ].astype(jnp.bfloat16)
            w_out = rg_w_out[occ].astype(jnp.bfloat16)
            proj = _inproj(x, shift, scale, w_in, rg_conv_w[occ], rg_conv_b[occ],
                           e=e, pad_l=1, col_major=col_major)
            w_in_c = w_in if need_ctx_out else w_in[:, :e]
            proj_c = _inproj(ctx, shift_c, scale_c, w_in_c, rg_conv_w[occ], rg_conv_b[occ],
                             e=e, pad_l=1, col_major=False)
            zeros = jnp.zeros((bsz, 1, e), jnp.float32)
            gates = [(_gate_weights(rg_w_r[occ, dd], rg_w_i[occ, dd]), rg_b_r[occ, dd], rg_b_i[occ, dd],
                      rg_lambda[occ, dd]) for dd in range(2)]
            hc_f, st_f = _rg_scan(proj_c, zeros, *gates[0], reverse=False, write_seq=need_ctx_out)
            h_f, _ = _rg_scan(proj, st_f, *gates[0], reverse=False)
            y_c, st_b = _rg_scan(proj_c, zeros, *gates[1], reverse=True, prev=hc_f, write_seq=need_ctx_out)
            y, _ = _rg_scan(proj, st_b, *gates[1], reverse=True, prev=h_f)
            g_group = 1
        else:
            w_in = hy_w_in[occ].astype(jnp.bfloat16)
            w_out = hy_w_out[occ].astype(jnp.bfloat16)
            proj = _inproj(x, shift, scale, w_in, hy_conv_w[occ], hy_conv_b[occ],
                           e=e, pad_l=1, col_major=col_major)
            if fft is None:
                tabs = _fft_tables(2 * seq // FFT_N2)
                fft = (tabs,) + tuple(_twiddled_dfts(tabs))
            f_re, f_im, inv_s = _filter_fft_a(*filter_hidden(occ, seq), hy_f_w3[occ], e, fft[0])
            y = _hyena_long(proj, f_re, f_im, inv_s, hy_d[occ], *fft)
            if need_ctx_out:
                proj_c = _inproj(ctx, shift_c, scale_c, w_in, hy_conv_w[occ], hy_conv_b[occ],
                                 e=e, pad_l=1, col_major=False)
                k_c, inv_s_c = _filters(*filter_hidden(occ, ctx.shape[1]), hy_f_w3[occ], e)
                y_c = _hyena_short(proj_c, k_c, inv_s_c, hy_d[occ], e)
            g_group = 3

        x = _outproj(y, proj, g_group, w_out, x, gate, ln_g[i], ln_b[i], alpha=alpha, col_major=col_major)
        if need_ctx_out:
            ctx = _outproj(y_c, proj_c, g_group, w_out, ctx, gate_c, ln_g[i], ln_b[i],
                           alpha=alpha, col_major=False)
    return x
```

```python
import functools
import math

import numpy as np
import jax
import jax.numpy as jnp
from jax import lax
from jax.experimental import pallas as pl
from jax.experimental.pallas import tpu as pltpu

GRID_W = 64
N_MIXERS = 2
RG_C = 8.0
RG_GATE_GROUP = 4
HY_EMB_BANDS = 8
HY_FAST_DECAY_PCT = 0.3
HY_SLOW_DECAY_PCT = 1.5
HY_DECAY_TARGET = 1e-2
HY_MOD_SHIFT = 0.05
LN_EPS = 1e-5

SUBLANES = 8
BF16_ROWS = 16
FFT_N2 = 256
FFT_J = SUBLANES
FFT_EC = 768
STREAM_TILE = 512
SCAN_CHUNK = 512
LANE_CHUNK = 256
VMEM_LIMIT = 56 * 1024 * 1024

_HIGHEST = lax.Precision.HIGHEST


def _round_up(a, m):
    return (a + m - 1) // m * m


def _params(*sem):
    return pltpu.CompilerParams(dimension_semantics=sem, vmem_limit_bytes=VMEM_LIMIT)


def _sigmoid(v):
    return 1.0 / (1.0 + jnp.exp(-v))


def _mod_kernel(c_ref, w_ref, b_ref, o_ref):
    cv = c_ref[...]
    act = cv * _sigmoid(cv)
    o_ref[0] = jnp.dot(act, w_ref[0], preferred_element_type=jnp.float32,
                       precision=_HIGHEST) + b_ref[0]


def _modulation(cvec, w_mod, b_mod):
    depth, d, d3 = w_mod.shape
    nblk = d3 // d
    return pl.pallas_call(
        _mod_kernel,
        grid=(depth, nblk),
        in_specs=[pl.BlockSpec((SUBLANES, d), lambda l, n: (0, 0)),
                  pl.BlockSpec((1, d, d), lambda l, n: (l, 0, n)),
                  pl.BlockSpec((1, 1, d), lambda l, n: (l, 0, n))],
        out_specs=pl.BlockSpec((1, SUBLANES, d), lambda l, n: (l, 0, n)),
        out_shape=jax.ShapeDtypeStruct((depth, SUBLANES, d3), jnp.float32),
        compiler_params=_params("parallel", "parallel"),
        name="modulation",
    )(cvec, w_mod, b_mod.reshape(depth, 1, d3))


def _stream_view(x, col_major):
    b, l, d = x.shape
    if col_major:
        return x.reshape(b, l // GRID_W, GRID_W * d)
    return x


def _stream_specs(l, d, tm, col_major, nb_grid_prefix):
    def wrap(fn):
        return lambda *g: fn(*g[nb_grid_prefix:])
    if col_major:
        rows = l // GRID_W
        ncol = tm // rows
        assert tm == rows * ncol
        main = pl.BlockSpec((1, rows, ncol * d), wrap(lambda b, i: (b, 0, i)))
        prev = pl.BlockSpec((1, SUBLANES, d),
                            wrap(lambda b, i: (b, rows // SUBLANES - 1, jnp.maximum(i * ncol - 1, 0))))
        nxt = pl.BlockSpec((1, SUBLANES, d),
                           wrap(lambda b, i: (b, 0, jnp.minimum((i + 1) * ncol, GRID_W - 1))))
    else:
        per = tm // SUBLANES
        last = l // SUBLANES - 1
        main = pl.BlockSpec((1, tm, d), wrap(lambda b, i: (b, i, 0)))
        prev = pl.BlockSpec((1, SUBLANES, d), wrap(lambda b, i: (b, jnp.maximum(i * per - 1, 0), 0)))
        nxt = pl.BlockSpec((1, SUBLANES, d), wrap(lambda b, i: (b, jnp.minimum((i + 1) * per, last), 0)))
    return main, prev, nxt


def _tile_rows(l, col_major):
    tm = min(l, STREAM_TILE)
    if col_major:
        rows = l // GRID_W
        return max(tm // rows, 1) * rows
    return tm


def _tile_tokens(ref, d):
    blk = ref[0]
    ncol = blk.shape[1] // d
    if ncol == 1:
        return blk
    return jnp.concatenate([blk[:, j * d:(j + 1) * d] for j in range(ncol)], axis=0)


def _inproj_kernel(x_ref, xp_ref, xn_ref, sh_ref, sc_ref, w_ref, cw_ref, cb_ref, o_ref,
                   *, n_conv, conv_k, pad_l, tm):
    i = pl.program_id(1)
    nt = pl.num_programs(1)
    d = w_ref.shape[0]
    e = o_ref.shape[3]
    scale = 1.0 + sc_ref[0]
    shift = sh_ref[0]
    mod = lambda v: v * scale + shift
    h_prev = jnp.where(i > 0, mod(xp_ref[0]), 0.0)
    h_next = jnp.where(i < nt - 1, mod(xn_ref[0]), 0.0)
    hh = jnp.concatenate([h_prev, mod(_tile_tokens(x_ref, d)), h_next], axis=0).astype(jnp.bfloat16)
    ext = tm + 2 * SUBLANES
    for n in range(o_ref.shape[0]):
        for c0 in range(0, e, LANE_CHUNK):
            sl = slice(c0, min(c0 + LANE_CHUNK, e))
            col = slice(n * e + sl.start, n * e + sl.stop)
            u = jnp.dot(hh, w_ref[:, col], preferred_element_type=jnp.float32)
            if n < n_conv:
                acc = cb_ref[:, col]
                for k in range(conv_k):
                    tap = u if k == pad_l else pltpu.roll(u, (pad_l - k) % ext, 0)
                    acc = acc + cw_ref[k:k + 1, col] * tap[SUBLANES:SUBLANES + tm]
            else:
                acc = u[SUBLANES:SUBLANES + tm]
            o_ref[n, 0, :, sl] = acc.astype(o_ref.dtype)


def _inproj(x, shift, scale, w_bf16, conv_w, conv_b, *, e, pad_l, col_major):
    b, l, d = x.shape
    nt_groups = w_bf16.shape[1] // e
    conv_k = conv_w.shape[0]
    n_conv = conv_w.shape[1] // e
    tm = _tile_rows(l, col_major)
    ntile = l // tm
    main, prev, nxt = _stream_specs(l, d, tm, col_major, 0)
    xv = _stream_view(x, col_major)
    kern = functools.partial(_inproj_kernel, n_conv=n_conv, conv_k=conv_k, pad_l=pad_l, tm=tm)
    const = lambda shape: pl.BlockSpec(shape, lambda bb, i: (0, 0), pipeline_mode=pl.Buffered(1))
    return pl.pallas_call(
        kern,
        grid=(b, ntile),
        in_specs=[main, prev, nxt,
                  pl.BlockSpec((1, 1, d), lambda bb, i: (bb, 0, 0)),
                  pl.BlockSpec((1, 1, d), lambda bb, i: (bb, 0, 0)),
                  const(w_bf16.shape), const(conv_w.shape), const((1, conv_w.shape[1]))],
        out_specs=pl.BlockSpec((nt_groups, 1, tm, e), lambda bb, i: (0, bb, i, 0)),
        out_shape=jax.ShapeDtypeStruct((nt_groups, b, l, e), jnp.bfloat16),
        compiler_params=_params("parallel", "parallel"),
        name="inproj_conv",
    )(xv, xv, xv, shift, scale, w_bf16, conv_w, conv_b.reshape(1, -1))


def _rg_scan_kernel(*refs, reverse, add_prev, write_seq, tm, gw, ngroups):
    it = iter(refs)
    xc_ref = next(it)
    prev_ref = next(it) if add_prev else None
    h0_ref, wg_ref, br_ref, bi_ref, lam_ref = next(it), next(it), next(it), next(it), next(it)
    o_ref = next(it) if write_seq else None
    st_ref = next(it)
    a_scr, b_scr, carry = next(it), next(it), next(it)

    i = pl.program_id(1)
    nchunk = pl.num_programs(1)

    @pl.when(i == 0)
    def _():
        carry[...] = h0_ref[0]

    lam = lam_ref[...]
    neg = -lam
    softplus = jnp.maximum(neg, 0.0) + jnp.log(1.0 + jnp.exp(-jnp.abs(neg)))
    rate = (0.5 * RG_C) * softplus
    for g in range(ngroups):
        sl = slice(g * gw, (g + 1) * gw)
        xg = xc_ref[0, 0, :, sl]
        pre = jnp.dot(xg, wg_ref[g], preferred_element_type=jnp.float32)
        t_r = jnp.tanh(pre[:, :gw] + br_ref[:, sl])
        t_i = jnp.tanh(pre[:, gw:] + bi_ref[:, sl])
        neg_log_a = rate[:, sl] * t_r + rate[:, sl]
        a = jnp.exp(-neg_log_a)
        a_scr[:, sl] = a
        s = jnp.tanh(neg_log_a) * (a * a + 1.0)
        mult = jnp.where(s > 0.0, s * lax.rsqrt(s), 0.0)
        b_scr[:, sl] = (0.5 * mult) * ((t_i + 1.0) * xg.astype(jnp.float32))

    e = a_scr.shape[1]
    row = lax.broadcasted_iota(jnp.int32, (SUBLANES, e), 0)
    nblk = tm // BF16_ROWS

    def scan8(r0, c):
        a = a_scr[pl.ds(r0, SUBLANES), :]
        h = b_scr[pl.ds(r0, SUBLANES), :]
        for dist in (1, 2, 4):
            shift = (SUBLANES - dist) if reverse else dist
            valid = (row + dist < SUBLANES) if reverse else (row >= dist)
            a_sh = jnp.where(valid, pltpu.roll(a, shift, 0), 1.0)
            h_sh = jnp.where(valid, pltpu.roll(h, shift, 0), 0.0)
            h = a * h_sh + h
            a = a * a_sh
        h = a * c + h
        return h, (h[0:1, :] if reverse else h[SUBLANES - 1:SUBLANES, :])

    def block(k, c):
        kk = (nblk - 1 - k) if reverse else k
        r0 = pl.multiple_of(kk * BF16_ROWS, BF16_ROWS)
        halves = [None, None]
        for hh in ((1, 0) if reverse else (0, 1)):
            halves[hh], c = scan8(r0 + hh * SUBLANES, c)
        if write_seq:
            h = jnp.concatenate(halves, axis=0)
            if add_prev:
                h = h + prev_ref[0, pl.ds(r0, BF16_ROWS), :].astype(jnp.float32)
            o_ref[0, pl.ds(r0, BF16_ROWS), :] = h.astype(o_ref.dtype)
        return c

    c_fin = lax.fori_loop(0, nblk, block, carry[...], unroll=2)
    carry[...] = c_fin

    @pl.when(i == nchunk - 1)
    def _():
        st_ref[0] = c_fin


def _rg_scan(proj, h0, wg, b_r, b_i, lam, *, reverse, prev=None, write_seq=True):
    _, b, l, e = proj.shape
    tm = min(l, SCAN_CHUNK)
    nchunk = l // tm
    ngroups, gw, _ = wg.shape
    pos = (lambda i: nchunk - 1 - i) if reverse else (lambda i: i)
    add_prev = prev is not None
    in_specs = [pl.BlockSpec((1, 1, tm, e), lambda bb, i: (0, bb, pos(i), 0))]
    args = [proj]
    if add_prev:
        in_specs.append(pl.BlockSpec((1, tm, e), lambda bb, i: (bb, pos(i), 0)))
        args.append(prev)
    in_specs += [pl.BlockSpec((1, 1, e), lambda bb, i: (bb, 0, 0)),
                 pl.BlockSpec((ngroups, gw, 2 * gw), lambda bb, i: (0, 0, 0)),
                 pl.BlockSpec((1, e), lambda bb, i: (0, 0)),
                 pl.BlockSpec((1, e), lambda bb, i: (0, 0)),
                 pl.BlockSpec((1, e), lambda bb, i: (0, 0))]
    args += [h0, wg, 0.5 * b_r.reshape(1, e), 0.5 * b_i.reshape(1, e), lam.reshape(1, e)]
    out_specs, out_shape = [], []
    if write_seq:
        out_specs.append(pl.BlockSpec((1, tm, e), lambda bb, i: (bb, pos(i), 0)))
        out_shape.append(jax.ShapeDtypeStruct((b, l, e), jnp.bfloat16))
    out_specs.append(pl.BlockSpec((1, 1, e), lambda bb, i: (bb, 0, 0)))
    out_shape.append(jax.ShapeDtypeStruct((b, 1, e), jnp.float32))
    kern = functools.partial(_rg_scan_kernel, reverse=reverse, add_prev=add_prev,
                             write_seq=write_seq, tm=tm, gw=gw, ngroups=ngroups)
    res = pl.pallas_call(
        kern,
        grid=(b, nchunk),
        in_specs=in_specs,
        out_specs=out_specs,
        out_shape=out_shape,
        scratch_shapes=[pltpu.VMEM((tm, e), jnp.float32), pltpu.VMEM((tm, e), jnp.float32),
                        pltpu.VMEM((1, e), jnp.float32)],
        compiler_params=_params("parallel", "arbitrary"),
        name="rg_scan_bwd" if reverse else "rg_scan_fwd",
    )(*args)
    if write_seq:
        return res[0], res[1]
    return None, res[0]


def _gate_weights(w_r, w_i):
    nb, bw, _ = w_r.shape
    g = RG_GATE_GROUP
    ng = nb // g
    eye = jnp.eye(g, dtype=w_r.dtype)

    def bd(w):
        w = w.reshape(ng, g, bw, bw)
        return jnp.einsum("nhij,hk->nhikj", w, eye).reshape(ng, g * bw, g * bw)

    return (0.5 * jnp.concatenate([bd(w_r), bd(w_i)], axis=-1)).astype(jnp.bfloat16)


def _outproj_kernel(y_ref, g_ref, w_ref, x_ref, gate_ref, lg_ref, lb_ref, o_ref, *, alpha):
    d = w_ref.shape[1]
    hg = 0.5 * g_ref[0, 0].astype(jnp.float32)
    z = (y_ref[0].astype(jnp.float32) * (hg * (jnp.tanh(hg) + 1.0))).astype(jnp.bfloat16)
    out = jnp.dot(z, w_ref[...], preferred_element_type=jnp.float32)
    r = alpha * _tile_tokens(x_ref, d) + gate_ref[0] * out
    mu = jnp.mean(r, axis=-1, keepdims=True)
    cen = r - mu
    var = jnp.mean(cen * cen, axis=-1, keepdims=True)
    res = cen * lax.rsqrt(var + LN_EPS) * lg_ref[...] + lb_ref[...]
    rows = o_ref.shape[1]
    for j in range(o_ref.shape[2] // d):
        o_ref[0, :, j * d:(j + 1) * d] = res[j * rows:(j + 1) * rows]


def _outproj(y, proj, g_group, w_bf16, x, gate, ln_g, ln_b, *, alpha, col_major):
    b, l, e = y.shape
    d = x.shape[-1]
    tm = _tile_rows(l, col_major)
    ntile = l // tm
    main, _, _ = _stream_specs(l, d, tm, col_major, 0)
    xv = _stream_view(x, col_major)
    out = pl.pallas_call(
        functools.partial(_outproj_kernel, alpha=alpha),
        grid=(b, ntile),
        in_specs=[pl.BlockSpec((1, tm, e), lambda bb, i: (bb, i, 0)),
                  pl.BlockSpec((1, 1, tm, e), lambda bb, i: (g_group, bb, i, 0)),
                  pl.BlockSpec((e, d), lambda bb, i: (0, 0)),
                  main,
                  pl.BlockSpec((1, 1, d), lambda bb, i: (bb, 0, 0)),
                  pl.BlockSpec((1, d), lambda bb, i: (0, 0)),
                  pl.BlockSpec((1, d), lambda bb, i: (0, 0))],
        out_specs=main,
        out_shape=jax.ShapeDtypeStruct(xv.shape, jnp.float32),
        compiler_params=_params("parallel", "parallel"),
        name="outproj_ln",
    )(y, proj, w_bf16, xv, gate, ln_g.reshape(1, d), ln_b.reshape(1, d))
    return out.reshape(x.shape)


def _filter_positions(l, fh):
    f32 = jnp.float32
    t = jnp.linspace(0.0, 1.0, l, dtype=f32)[:, None]
    bands = jnp.linspace(1e-4, HY_EMB_BANDS - 1, HY_EMB_BANDS, dtype=f32)
    w = (2.0 * math.pi) * jnp.arange(l, dtype=f32)[:, None] / l
    z = jnp.concatenate([t, jnp.cos(bands * w), -jnp.sin(bands * w)], axis=-1)
    z_rev = jnp.concatenate([z[:1], z[:0:-1]], axis=0)
    pad = lambda a: jnp.pad(a, ((0, 0), (0, fh - a.shape[1])))
    return jnp.concatenate([pad(z), pad(z_rev)], axis=1), jnp.stack([z[:, 0:1], z_rev[:, 0:1]])


def _hidden_kernel(z_ref, w1_ref, b1_ref, w2_ref, b2_ref, fr_ref, o_ref):
    fr = fr_ref[...]
    h = jnp.dot(z_ref[...], w1_ref[...], preferred_element_type=jnp.float32, precision=_HIGHEST)
    h = jnp.sin(fr * (h + b1_ref[...]))
    h = jnp.dot(h, w2_ref[...], preferred_element_type=jnp.float32, precision=_HIGHEST)
    o_ref[...] = jnp.sin(fr * (h + b2_ref[...]))


def _filter_hidden(zz, w1, b1, w2, b2, freq):
    n, zp = zz.shape
    fh = w1.shape[1]
    tr = min(n, 1024)
    both = lambda v: jnp.concatenate([v, v]).reshape(1, 2 * fh)
    blockdiag = lambda w: jnp.kron(jnp.eye(2, dtype=w.dtype), w)
    w1p = blockdiag(jnp.pad(w1, ((0, fh - w1.shape[0]), (0, 0))))
    full = lambda shape: pl.BlockSpec(shape, lambda i: (0,) * len(shape))
    return pl.pallas_call(
        _hidden_kernel,
        grid=(n // tr,),
        in_specs=[pl.BlockSpec((tr, zp), lambda i: (i, 0)), full((zp, zp)), full((1, zp)),
                  full((zp, zp)), full((1, zp)), full((1, zp))],
        out_specs=pl.BlockSpec((tr, zp), lambda i: (i, 0)),
        out_shape=jax.ShapeDtypeStruct((n, zp), jnp.float32),
        compiler_params=_params("parallel"),
        name="filter_hidden",
    )(zz, w1p, both(b1), blockdiag(w2), both(b2), both(freq))


def _split_bf16(a):
    hi = a.astype(jnp.bfloat16)
    return hi, (a - hi.astype(jnp.float32)).astype(jnp.bfloat16)


def _filter_kernel(hid_ref, t_ref, w3_ref, ad_ref, k_ref, s_ref, *, tr, n_total):
    d = pl.program_id(1)
    i = pl.program_id(2)
    last = (d == pl.num_programs(1) - 1) & (i == pl.num_programs(2) - 1)
    hi, lo = _split_bf16(hid_ref[...])
    h = jnp.dot(jnp.concatenate([hi, lo, hi], axis=1), w3_ref[0, 0], preferred_element_type=jnp.float32)
    window = jnp.exp(-t_ref[0] * ad_ref[...]) + HY_MOD_SHIFT
    row = lax.broadcasted_iota(jnp.int32, h.shape, 0)
    gap = (row == 0) & (d == 1) & (i == 0)
    k = jnp.where(gap, 0.0, h * window)
    k_ref[0] = k

    @pl.when((d == 0) & (i == 0))
    def _():
        s_ref[...] = jnp.zeros_like(s_ref)

    s_ref[0] += jnp.sum(jnp.abs(k).reshape(tr // SUBLANES, SUBLANES, -1), axis=0)

    @pl.when(last)
    def _():
        tot = jnp.sum(s_ref[0], axis=0, keepdims=True)
        s_ref[0] = jnp.broadcast_to(1.0 / (tot * n_total), s_ref.shape[1:])


def _filter_weights(w3, fh, e):
    order = w3.shape[1] // (2 * e)
    max_decay = math.log(HY_DECAY_TARGET) / HY_FAST_DECAY_PCT
    min_decay = math.log(HY_DECAY_TARGET) / HY_SLOW_DECAY_PCT
    absdelta = jnp.abs(jnp.linspace(min_decay, max_decay, e, dtype=jnp.float32)).reshape(1, e)
    w3 = w3.reshape(fh, order, 2, e).transpose(1, 2, 0, 3)
    w3 = jnp.stack([jnp.pad(w3[:, 0], ((0, 0), (0, fh), (0, 0))),
                    jnp.pad(w3[:, 1], ((0, 0), (fh, 0), (0, 0)))], axis=1)
    w_hi, w_lo = _split_bf16(w3)
    return jnp.concatenate([w_hi, w_hi, w_lo], axis=2), absdelta


def _filter_fft_a_kernel(hid_ref, t_ref, w3_ref, ad_ref, m_ref, re_ref, im_ref, s_ref, *, n_total):
    j = pl.program_id(2)
    k1h, rows, fh2 = hid_ref.shape
    ec = ad_ref.shape[1]
    hi, lo = _split_bf16(hid_ref[...].reshape(k1h * rows, fh2))
    stacked = jnp.concatenate([hi, lo, hi], axis=1)
    halves = []
    for d in range(2):
        h = jnp.dot(stacked, w3_ref[0, d], preferred_element_type=jnp.float32)
        window = jnp.exp(-t_ref[d].reshape(k1h * rows, 1) * ad_ref[...]) + HY_MOD_SHIFT
        k = h * window
        if d == 1:
            row = lax.broadcasted_iota(jnp.int32, k.shape, 0)
            k = jnp.where((row == 0) & (j == 0), 0.0, k)
        halves.append(k)

    @pl.when(j == 0)
    def _():
        s_ref[...] = jnp.zeros_like(s_ref)

    k = jnp.concatenate(halves, axis=0)
    s_ref[0] += jnp.sum(jnp.abs(k).reshape(-1, SUBLANES, ec), axis=0)

    @pl.when(j == pl.num_programs(2) - 1)
    def _():
        tot = jnp.sum(s_ref[0], axis=0, keepdims=True)
        s_ref[0] = jnp.broadcast_to(1.0 / (tot * n_total), s_ref.shape[1:])

    k3 = k.reshape(2 * k1h, rows, ec)
    half = m_ref.shape[0] // 2
    f1p = half // FFT_J
    re_parts, im_parts = [], []
    for h in range(rows // FFT_J):
        x = k3[:, h * FFT_J:(h + 1) * FFT_J, :].reshape(2 * k1h * FFT_J, ec)
        r = jnp.dot(m_ref[...], x.astype(jnp.bfloat16), preferred_element_type=jnp.float32)
        re_parts.append(r[:half].reshape(f1p, FFT_J, ec))
        im_parts.append(r[half:].reshape(f1p, FFT_J, ec))
    re_ref[0] = jnp.concatenate(re_parts, axis=1).astype(jnp.bfloat16)
    im_ref[0] = jnp.concatenate(im_parts, axis=1).astype(jnp.bfloat16)


def _filter_fft_a(hid, tcol, w3, e, tabs):
    l, fh2 = hid.shape
    n2 = FFT_N2
    k1h = l // n2
    f1p = tabs["f1p"]
    mat = tabs["a_full"]
    w_stack, absdelta = _filter_weights(w3, fh2 // 2, e)
    order = w_stack.shape[0]
    ec = min(e, FFT_EC)
    out_spec = pl.BlockSpec((1, f1p, BF16_ROWS, ec), lambda o, cc, j: (o, 0, j, cc))
    shape = jax.ShapeDtypeStruct((order, f1p, n2, e), jnp.bfloat16)
    return pl.pallas_call(
        functools.partial(_filter_fft_a_kernel, n_total=float(2 * l)),
        grid=(order, e // ec, n2 // BF16_ROWS),
        in_specs=[pl.BlockSpec((k1h, BF16_ROWS, fh2), lambda o, cc, j: (0, j, 0)),
                  pl.BlockSpec((2, k1h, BF16_ROWS, 1), lambda o, cc, j: (0, 0, j, 0)),
                  pl.BlockSpec((1, 2, w_stack.shape[2], ec), lambda o, cc, j: (o, 0, 0, cc)),
                  pl.BlockSpec((1, ec), lambda o, cc, j: (0, cc)),
                  pl.BlockSpec(mat.shape, lambda o, cc, j: (0, 0))],
        out_specs=[out_spec, out_spec,
                   pl.BlockSpec((1, SUBLANES, ec), lambda o, cc, j: (o, 0, cc))],
        out_shape=[shape, shape, jax.ShapeDtypeStruct((order, SUBLANES, e), jnp.float32)],
        compiler_params=_params("parallel", "parallel", "arbitrary"),
        name="filter_fft_a",
    )(hid.reshape(k1h, n2, fh2), tcol.reshape(2, k1h, n2, 1), w_stack, absdelta, mat)


def _filters(hid, tcol, w3, e):
    l, fh2 = hid.shape
    n = 2 * l
    tr = min(l, 512)
    nl = l // tr
    w_stack, absdelta = _filter_weights(w3, fh2 // 2, e)
    order = w_stack.shape[0]
    return pl.pallas_call(
        functools.partial(_filter_kernel, tr=tr, n_total=float(n)),
        grid=(order, 2, nl),
        in_specs=[pl.BlockSpec((tr, fh2), lambda o, d, i: (i, 0)),
                  pl.BlockSpec((1, tr, 1), lambda o, d, i: (d, i, 0)),
                  pl.BlockSpec((1, 1, 3 * fh2, e), lambda o, d, i: (o, d, 0, 0)),
                  pl.BlockSpec((1, e), lambda o, d, i: (0, 0))],
        out_specs=[pl.BlockSpec((1, tr, e), lambda o, d, i: (o, d * nl + i, 0)),
                   pl.BlockSpec((1, SUBLANES, e), lambda o, d, i: (o, 0, 0))],
        out_shape=[jax.ShapeDtypeStruct((order, n, e), jnp.float32),
                   jax.ShapeDtypeStruct((order, SUBLANES, e), jnp.float32)],
        compiler_params=_params("parallel", "arbitrary", "arbitrary"),
        name="filter_gen",
    )(hid, tcol, w_stack, absdelta)


def _angles(num, den):
    ang = 2.0 * np.pi * (np.asarray(num, np.int64) % den).astype(np.float64) / den
    return np.cos(ang), np.sin(ang)


@functools.lru_cache(maxsize=None)
def _fft_tables(n1):
    n2 = FFT_N2
    n = n1 * n2
    f1 = n1 // 2 + 1
    f1p = _round_up(f1, 2)
    fr = np.arange(f1)[:, None]
    eye = np.eye(FFT_J)

    def stage_a(k1):
        c, s = _angles(fr * np.arange(k1)[None, :], n1)
        m = np.zeros((2 * f1p, k1))
        m[:f1], m[f1p:f1p + f1] = c, -s
        return np.kron(m, eye)

    nyq = n1 // 2
    wgt = np.full((nyq,), 2.0)
    wgt[0] = 1.0
    c, s = _angles(np.arange(n1 // 2)[:, None] * np.arange(nyq)[None, :], n1)
    m_c = np.concatenate([c * wgt, -s * wgt], axis=1)

    c2, s2 = _angles(np.arange(n2)[:, None] * np.arange(n2)[None, :], n2)
    cphi, sphi = _angles(fr * np.arange(n2)[None, :], n)
    phase = np.stack([cphi, sphi], axis=1)
    bf = lambda a: jnp.asarray(a, jnp.float32).astype(jnp.bfloat16)
    f32 = lambda a: jnp.asarray(a, jnp.float32)
    return dict(f1=f1, f1p=f1p, a_data=bf(stage_a(n1 // 2)), a_full=bf(stage_a(n1)),
                m_c=bf(np.kron(m_c, eye)), base_c=f32(c2), base_s=f32(s2),
                phase_row=f32(phase[:, :, None, :]), phase_col=f32(phase[..., None]))


def _twiddled_dft_kernel(c_ref, s_ref, row_ref, col_ref, mf_ref, mi_ref):
    c, s = c_ref[...], s_ref[...]
    n2 = c.shape[0]
    bf = jnp.bfloat16
    cr, sr = row_ref[0, 0], row_ref[0, 1]
    cf, sf = (c * cr - s * sr).astype(bf), (s * cr + c * sr).astype(bf)
    mf_ref[0, :n2, :n2] = cf
    mf_ref[0, :n2, n2:] = sf
    mf_ref[0, n2:, :n2] = -sf
    mf_ref[0, n2:, n2:] = cf
    cc, sc = col_ref[0, 0], col_ref[0, 1]
    ci, si = (c * cc - s * sc).astype(bf), (s * cc + c * sc).astype(bf)
    mi_ref[0, :n2, :n2] = ci
    mi_ref[0, :n2, n2:] = -si
    mi_ref[0, n2:, :n2] = si
    mi_ref[0, n2:, n2:] = ci


def _twiddled_dfts(tabs):
    f1 = tabs["f1"]
    n2 = FFT_N2
    shape = jax.ShapeDtypeStruct((f1, 2 * n2, 2 * n2), jnp.bfloat16)
    base = pl.BlockSpec((n2, n2), lambda f: (0, 0))
    out = pl.BlockSpec((1, 2 * n2, 2 * n2), lambda f: (f, 0, 0))
    return pl.pallas_call(
        _twiddled_dft_kernel,
        grid=(f1,),
        in_specs=[base, base, pl.BlockSpec((1, 2, 1, n2), lambda f: (f, 0, 0, 0)),
                  pl.BlockSpec((1, 2, n2, 1), lambda f: (f, 0, 0, 0))],
        out_specs=[out, out],
        out_shape=[shape, shape],
        compiler_params=_params("parallel"),
        name="twiddled_dft_tables",
    )(tabs["base_c"], tabs["base_s"], tabs["phase_row"], tabs["phase_col"])


def _fft_a_kernel(*refs, scaled):
    it = iter(refs)
    x_ref = next(it)
    sc_ref = next(it) if scaled else None
    m_ref, re_ref, im_ref = next(it), next(it), next(it)
    k1, ec = x_ref.shape[2], x_ref.shape[4]
    half = m_ref.shape[0] // 2
    f1p = half // FFT_J
    re_parts, im_parts = [], []
    xf = x_ref[0, 0].astype(jnp.float32)
    for h in range(x_ref.shape[3] // FFT_J):
        x = xf[:, h * FFT_J:(h + 1) * FFT_J, :].reshape(k1 * FFT_J, ec)
        if scaled:
            x = x * sc_ref[0]
        r = jnp.dot(m_ref[...], x.astype(jnp.bfloat16), preferred_element_type=jnp.float32)
        re_parts.append(r[:half].reshape(f1p, FFT_J, ec))
        im_parts.append(r[half:].reshape(f1p, FFT_J, ec))
    re_ref[0] = jnp.concatenate(re_parts, axis=1).astype(jnp.bfloat16)
    im_ref[0] = jnp.concatenate(im_parts, axis=1).astype(jnp.bfloat16)


def _fft_stage_a(arr, group, tabs, scale=None):
    _, bt, rows, e = arr.shape
    k1 = rows // FFT_N2
    f1p = tabs["f1p"]
    mat = tabs["a_data"] if k1 * FFT_J == tabs["a_data"].shape[1] else tabs["a_full"]
    assert mat.shape[1] == k1 * FFT_J
    ec = min(e, FFT_EC)
    av = arr.reshape(arr.shape[0], bt, k1, FFT_N2, e)
    in_specs = [pl.BlockSpec((1, 1, k1, BF16_ROWS, ec), lambda bb, j, cc: (group, bb, 0, j, cc))]
    args = [av]
    if scale is not None:
        in_specs.append(pl.BlockSpec((1, 1, ec), lambda bb, j, cc: (bb, 0, cc)))
        args.append(scale)
    in_specs.append(pl.BlockSpec(mat.shape, lambda bb, j, cc: (0, 0)))
    args.append(mat)
    out_spec = pl.BlockSpec((1, f1p, BF16_ROWS, ec), lambda bb, j, cc: (bb, 0, j, cc))
    shape = jax.ShapeDtypeStruct((bt, f1p, FFT_N2, e), jnp.bfloat16)
    return pl.pallas_call(
        functools.partial(_fft_a_kernel, scaled=scale is not None),
        grid=(bt, FFT_N2 // BF16_ROWS, e // ec),
        in_specs=in_specs,
        out_specs=[out_spec, out_spec],
        out_shape=[shape, shape],
        compiler_params=_params("parallel", "parallel", "parallel"),
        name="fft_stage_a",
    )(*args)


def _lane_chunks(e, width=256):
    return [slice(c0, min(c0 + width, e)) for c0 in range(0, e, width)]


def _fft_b_filter_kernel(re_ref, im_ref, sc_ref, mf_ref, kre_ref, kim_ref):
    for b in range(re_ref.shape[0]):
        for sl in _lane_chunks(re_ref.shape[3]):
            x = jnp.concatenate([re_ref[b, 0, :, sl], im_ref[b, 0, :, sl]], axis=0)
            u = jnp.dot(mf_ref[0], x, preferred_element_type=jnp.float32) * sc_ref[b, 0:1, sl]
            kre_ref[b, 0, :, sl] = u[:FFT_N2].astype(kre_ref.dtype)
            kim_ref[b, 0, :, sl] = u[FFT_N2:].astype(kim_ref.dtype)


def _fft_b_conv_kernel(re_ref, im_ref, kre_ref, kim_ref, mf_ref, mi_ref, ore_ref, oim_ref):
    for sl in _lane_chunks(re_ref.shape[3]):
        k_re = kre_ref[0, 0, :, sl].astype(jnp.float32)
        k_im = kim_ref[0, 0, :, sl].astype(jnp.float32)
        for b in range(re_ref.shape[0]):
            x = jnp.concatenate([re_ref[b, 0, :, sl], im_ref[b, 0, :, sl]], axis=0)
            u = jnp.dot(mf_ref[0], x, preferred_element_type=jnp.float32)
            u_re, u_im = u[:FFT_N2], u[FFT_N2:]
            p = jnp.concatenate([u_re * k_re - u_im * k_im, u_re * k_im + u_im * k_re], axis=0)
            v = jnp.dot(mi_ref[0], p.astype(jnp.bfloat16), preferred_element_type=jnp.float32)
            ore_ref[b, 0, :, sl] = v[:FFT_N2].astype(jnp.bfloat16)
            oim_ref[b, 0, :, sl] = v[FFT_N2:].astype(jnp.bfloat16)


def _fft_stage_b_filter(a_re, a_im, inv_s, mf, tabs):
    bt, _, n2, e = a_re.shape
    f1 = tabs["f1"]
    blk = pl.BlockSpec((bt, 1, n2, e), lambda f: (0, f, 0, 0))
    shape = jax.ShapeDtypeStruct((bt, f1, n2, e), jnp.bfloat16)
    return pl.pallas_call(
        _fft_b_filter_kernel,
        grid=(f1,),
        in_specs=[blk, blk, pl.BlockSpec(inv_s.shape, lambda f: (0, 0, 0)),
                  pl.BlockSpec((1, 2 * n2, 2 * n2), lambda f: (f, 0, 0))],
        out_specs=[blk, blk],
        out_shape=[shape, shape],
        compiler_params=_params("parallel"),
        name="fft_stage_b_filter",
    )(a_re, a_im, inv_s, mf)


def _fft_stage_b_conv(a_re, a_im, kf_re, kf_im, order_idx, mf, mi, tabs):
    bt, f1p, n2, e = a_re.shape
    f1 = tabs["f1"]
    fc = lambda f: jnp.minimum(f, f1 - 1)
    blk = pl.BlockSpec((bt, 1, n2, e), lambda f: (0, f, 0, 0))
    kblk = pl.BlockSpec((1, 1, n2, e), lambda f: (order_idx, fc(f), 0, 0))
    wblk = pl.BlockSpec((1, 2 * n2, 2 * n2), lambda f: (fc(f), 0, 0))
    shape = jax.ShapeDtypeStruct((bt, f1p, n2, e), jnp.bfloat16)
    return pl.pallas_call(
        _fft_b_conv_kernel,
        grid=(f1p,),
        in_specs=[blk, blk, kblk, kblk, wblk, wblk],
        out_specs=[blk, blk],
        out_shape=[shape, shape],
        compiler_params=_params("parallel"),
        name="fft_stage_b_conv",
    )(a_re, a_im, kf_re, kf_im, mf, mi)


def _fft_c_kernel(*refs, chain):
    re_ref, im_ref, m_ref, u_ref, mul_ref, d_ref = refs[:6]
    if chain:
        ma_ref, o_ref, are_ref, aim_ref = refs[6:]
    else:
        o_ref, = refs[6:]
    k1, ec = o_ref.shape[1], o_ref.shape[3]
    f1p = re_ref.shape[1]
    nyq = m_ref.shape[1] // (2 * FFT_J)
    re = re_ref[0].astype(jnp.float32)
    im = im_ref[0].astype(jnp.float32)
    uf = u_ref[0, 0].astype(jnp.float32)
    mf = mul_ref[0, 0].astype(jnp.float32)
    parts, re_parts, im_parts = [], [], []
    for h in range(o_ref.shape[2] // FFT_J):
        sl = slice(h * FFT_J, (h + 1) * FFT_J)
        x = jnp.concatenate([re[:nyq, sl, :].reshape(nyq * FFT_J, ec),
                             im[:nyq, sl, :].reshape(nyq * FFT_J, ec)], axis=0)
        y = jnp.dot(m_ref[...], x.astype(jnp.bfloat16), preferred_element_type=jnp.float32)
        r_nyq = re[nyq, sl, :]
        y = (y.reshape(k1 // 2, 2, FFT_J, ec) + jnp.stack([r_nyq, -r_nyq])[None]).reshape(k1 * FFT_J, ec)
        u = uf[:, sl, :].reshape(k1 * FFT_J, ec)
        mul = mf[:, sl, :].reshape(k1 * FFT_J, ec)
        z = (mul * (y + u * d_ref[...])).astype(jnp.bfloat16)
        parts.append(z.astype(jnp.float32).reshape(k1, FFT_J, ec))
        if chain:
            r = jnp.dot(ma_ref[...], z, preferred_element_type=jnp.float32)
            half = ma_ref.shape[0] // 2
            re_parts.append(r[:half].reshape(f1p, FFT_J, ec))
            im_parts.append(r[half:].reshape(f1p, FFT_J, ec))
    o_ref[0] = jnp.concatenate(parts, axis=1).astype(o_ref.dtype)
    if chain:
        are_ref[0] = jnp.concatenate(re_parts, axis=1).astype(jnp.bfloat16)
        aim_ref[0] = jnp.concatenate(im_parts, axis=1).astype(jnp.bfloat16)


def _fft_stage_c(b_re, b_im, u_arr, u_group, proj, mul_group, d_row, tabs, chain):
    bt, f1p, n2, e = b_re.shape
    l = u_arr.shape[2]
    k1 = l // n2
    ec = min(e, FFT_EC)
    uv = u_arr.reshape(u_arr.shape[0], bt, k1, n2, e)
    pv = proj.reshape(proj.shape[0], bt, k1, n2, e)
    mat = tabs["m_c"]
    fblk = pl.BlockSpec((1, f1p, BF16_ROWS, ec), lambda bb, j, cc: (bb, 0, j, cc))
    full = lambda m: pl.BlockSpec(m.shape, lambda bb, j, cc: (0, 0))
    in_specs = [fblk, fblk, full(mat),
                pl.BlockSpec((1, 1, k1, BF16_ROWS, ec), lambda bb, j, cc: (u_group, bb, 0, j, cc)),
                pl.BlockSpec((1, 1, k1, BF16_ROWS, ec), lambda bb, j, cc: (mul_group, bb, 0, j, cc)),
                pl.BlockSpec((1, ec), lambda bb, j, cc: (0, cc))]
    args = [b_re, b_im, mat, uv, pv, d_row]
    out_specs = [pl.BlockSpec((1, k1, BF16_ROWS, ec), lambda bb, j, cc: (bb, 0, j, cc))]
    out_shape = [jax.ShapeDtypeStruct((bt, k1, n2, e), jnp.bfloat16)]
    if chain:
        in_specs.append(full(tabs["a_data"]))
        args.append(tabs["a_data"])
        out_specs += [fblk, fblk]
        out_shape += [jax.ShapeDtypeStruct((bt, f1p, n2, e), jnp.bfloat16)] * 2
    res = pl.pallas_call(
        functools.partial(_fft_c_kernel, chain=chain),
        grid=(bt, n2 // BF16_ROWS, e // ec),
        in_specs=in_specs,
        out_specs=out_specs,
        out_shape=out_shape,
        compiler_params=_params("parallel", "parallel", "parallel"),
        name="fft_stage_ca" if chain else "fft_stage_c",
    )(*args)
    return (res[0].reshape(bt, l, e),) + tuple(res[1:])


def _hyena_long(proj, f_re, f_im, inv_s, d_bias, tabs, mf, mi):
    order, e = d_bias.shape
    kf_re, kf_im = _fft_stage_b_filter(f_re, f_im, inv_s, mf, tabs)
    u_arr, u_group = proj, 0
    a_re, a_im = _fft_stage_a(u_arr, u_group, tabs)
    for o in range(order):
        b_re, b_im = _fft_stage_b_conv(a_re, a_im, kf_re, kf_im, o, mf, mi, tabs)
        chain = o + 1 < order
        res = _fft_stage_c(b_re, b_im, u_arr, u_group, proj, o + 1, d_bias[o].reshape(1, e), tabs, chain)
        if chain:
            a_re, a_im = res[1:]
        u_arr, u_group = res[0][None], 0
    return res[0]


@functools.lru_cache(maxsize=None)
def _dft_tables(l):
    n = 2 * l
    nf = l + 1
    nfp = _round_up(nf, BF16_ROWS)
    c, s = _angles(np.arange(nf)[:, None] * np.arange(n)[None, :], n)
    fwd = np.zeros((2 * nfp, n))
    fwd[:nf], fwd[nfp:nfp + nf] = c, -s
    wgt = np.full((nf,), 2.0)
    wgt[0] = wgt[-1] = 1.0
    inv = np.zeros((l, 2 * nfp))
    inv[:, :nf], inv[:, nfp:nfp + nf] = (c[:, :l] * wgt[:, None]).T, (-s[:, :l] * wgt[:, None]).T
    bf = lambda a: jnp.asarray(a, jnp.float32).astype(jnp.bfloat16)
    return nfp, bf(fwd), bf(inv)


def _hyena_short_kernel(p_ref, k_ref, s_ref, d_ref, fwd_ref, inv_ref, o_ref, *, l, nfp, order):
    fwd = fwd_ref[...]
    u = p_ref[0, 0]
    for o in range(order):
        kn = (k_ref[o] * s_ref[o, 0:1, :]).astype(jnp.bfloat16)
        kf = jnp.dot(fwd, kn, preferred_element_type=jnp.float32)
        uf = jnp.dot(fwd[:, :l], u.astype(jnp.bfloat16), preferred_element_type=jnp.float32)
        k_re, k_im, u_re, u_im = kf[:nfp], kf[nfp:], uf[:nfp], uf[nfp:]
        p = jnp.concatenate([u_re * k_re - u_im * k_im, u_re * k_im + u_im * k_re], axis=0)
        y = jnp.dot(inv_ref[...], p.astype(jnp.bfloat16), preferred_element_type=jnp.float32)
        u = p_ref[o + 1, 0] * (y + u * d_ref[o:o + 1, :])
    o_ref[0] = u


def _hyena_short(proj, k, inv_s, d_bias, e):
    nt, b, l, _ = proj.shape
    order = k.shape[0]
    nfp, fwd, inv = _dft_tables(l)
    return pl.pallas_call(
        functools.partial(_hyena_short_kernel, l=l, nfp=nfp, order=order),
        grid=(b,),
        in_specs=[pl.BlockSpec((nt, 1, l, e), lambda bb: (0, bb, 0, 0)),
                  pl.BlockSpec((order, 2 * l, e), lambda bb: (0, 0, 0)),
                  pl.BlockSpec((order, SUBLANES, e), lambda bb: (0, 0, 0)),
                  pl.BlockSpec((order, e), lambda bb: (0, 0)),
                  pl.BlockSpec(fwd.shape, lambda bb: (0, 0)),
                  pl.BlockSpec(inv.shape, lambda bb: (0, 0))],
        out_specs=pl.BlockSpec((1, l, e), lambda bb: (bb, 0, 0)),
        out_shape=jax.ShapeDtypeStruct((b, l, e), jnp.float32),
        compiler_params=_params("parallel"),
        name="hyena_short",
    )(proj, k, inv_s, d_bias, fwd, inv)


def kernel(x, c, ctx, c_ctx, w_mod, b_mod, ln_g, ln_b, rg_w_in, rg_conv_w, rg_conv_b, rg_w_r, rg_b_r, rg_w_i, rg_b_i, rg_lambda, rg_w_out, hy_w_in, hy_conv_w, hy_conv_b, hy_f_w1, hy_f_b1, hy_f_w2, hy_f_b2, hy_f_w3, hy_f_freq, hy_d, hy_w_out):
    bsz, seq, d = x.shape
    depth = w_mod.shape[0]
    e = rg_w_out.shape[1]
    alpha = (2 * depth) ** 0.25
    assert bsz + 1 <= SUBLANES

    cvec = jnp.zeros((SUBLANES, d), jnp.float32).at[:bsz].set(c).at[bsz].set(c_ctx)
    mod = _modulation(cvec, w_mod, b_mod)

    def mod_rows(layer, part, for_ctx):
        v = mod[layer, :, part * d:(part + 1) * d]
        v = jnp.broadcast_to(v[bsz], (bsz, d)) if for_ctx else v[:bsz]
        return v.reshape(bsz, 1, d)

    fft = None

    def filter_hidden(occ, l):
        zz, tcol = _filter_positions(l, hy_f_w1.shape[-1])
        hid = _filter_hidden(zz, hy_f_w1[occ], hy_f_b1[occ], hy_f_w2[occ], hy_f_b2[occ], hy_f_freq[occ])
        return hid, tcol

    for i in range(depth):
        kind = i % N_MIXERS
        occ = i // N_MIXERS
        need_ctx_out = any(j % N_MIXERS == 0 for j in range(i + 1, depth))
        col_major = occ % 2 == 1
        use_ctx = kind == 0 or need_ctx_out
        shift, scale, gate = (mod_rows(i, p, False) for p in range(3))
        if use_ctx:
            shift_c, scale_c, gate_c = (mod_rows(i, p, True) for p in range(3))

        if kind == 0:
            w_in = rg_w_in[occ].astype(jnp.bfloat16)
            w_out = rg_w_out[occ].astype(jnp.bfloat16)
            proj = _inproj(x, shift, scale, w_in, rg_conv_w[occ], rg_conv_b[occ],
                           e=e, pad_l=1, col_major=col_major)
            w_in_c = w_in if need_ctx_out else w_in[:, :e]
            proj_c = _inproj(ctx, shift_c, scale_c, w_in_c, rg_conv_w[occ], rg_conv_b[occ],
                             e=e, pad_l=1, col_major=False)
            zeros = jnp.zeros((bsz, 1, e), jnp.float32)
            gates = [(_gate_weights(rg_w_r[occ, dd], rg_w_i[occ, dd]), rg_b_r[occ, dd], rg_b_i[occ, dd],
                      rg_lambda[occ, dd]) for dd in range(2)]
            hc_f, st_f = _rg_scan(proj_c, zeros, *gates[0], reverse=False, write_seq=need_ctx_out)
            h_f, _ = _rg_scan(proj, st_f, *gates[0], reverse=False)
            y_c, st_b = _rg_scan(proj_c, zeros, *gates[1], reverse=True, prev=hc_f, write_seq=need_ctx_out)
            y, _ = _rg_scan(proj, st_b, *gates[1], reverse=True, prev=h_f)
            g_group = 1
        else:
            w_in = hy_w_in[occ].astype(jnp.bfloat16)
            w_out = hy_w_out[occ].astype(jnp.bfloat16)
            proj = _inproj(x, shift, scale, w_in, hy_conv_w[occ], hy_conv_b[occ],
                           e=e, pad_l=1, col_major=col_major)
            if fft is None:
                tabs = _fft_tables(2 * seq // FFT_N2)
                fft = (tabs,) + tuple(_twiddled_dfts(tabs))
            f_re, f_im, inv_s = _filter_fft_a(*filter_hidden(occ, seq), hy_f_w3[occ], e, fft[0])
            y = _hyena_long(proj, f_re, f_im, inv_s, hy_d[occ], *fft)
            if need_ctx_out:
                proj_c = _inproj(ctx, shift_c, scale_c, w_in, hy_conv_w[occ], hy_conv_b[occ],
                                 e=e, pad_l=1, col_major=False)
                k_c, inv_s_c = _filters(*filter_hidden(occ, ctx.shape[1]), hy_f_w3[occ], e)
                y_c = _hyena_short(proj_c, k_c, inv_s_c, hy_d[occ], e)
            g_group = 3

        x = _outproj(y, proj, g_group, w_out, x, gate, ln_g[i], ln_b[i], alpha=alpha, col_major=col_major)
        if need_ctx_out:
            ctx = _outproj(y_c, proj_c, g_group, w_out, ctx, gate_c, ln_g[i], ln_b[i],
                           alpha=alpha, col_major=False)
    return x
```

```python
import functools
import math

import numpy as np
import jax
import jax.numpy as jnp
from jax import lax
from jax.experimental import pallas as pl
from jax.experimental.pallas import tpu as pltpu

GRID_W = 64
N_MIXERS = 2
RG_C = 8.0
RG_GATE_GROUP = 4
HY_EMB_BANDS = 8
HY_FAST_DECAY_PCT = 0.3
HY_SLOW_DECAY_PCT = 1.5
HY_DECAY_TARGET = 1e-2
HY_MOD_SHIFT = 0.05
LN_EPS = 1e-5

SUBLANES = 8
BF16_ROWS = 16
FFT_N2 = 256
FFT_J = SUBLANES
FFT_EC = 768
STREAM_TILE = 512
SCAN_CHUNK = 512
LANE_CHUNK = 256
VMEM_LIMIT = 56 * 1024 * 1024

_HIGHEST = lax.Precision.HIGHEST


def _round_up(a, m):
    return (a + m - 1) // m * m


def _params(*sem):
    return pltpu.CompilerParams(dimension_semantics=sem, vmem_limit_bytes=VMEM_LIMIT)


def _sigmoid(v):
    return 1.0 / (1.0 + jnp.exp(-v))


def _mod_kernel(c_ref, w_ref, b_ref, o_ref):
    cv = c_ref[...]
    act = cv * _sigmoid(cv)
    o_ref[0] = jnp.dot(act, w_ref[0], preferred_element_type=jnp.float32,
                       precision=_HIGHEST) + b_ref[0]


def _modulation(cvec, w_mod, b_mod):
    depth, d, d3 = w_mod.shape
    nblk = d3 // d
    return pl.pallas_call(
        _mod_kernel,
        grid=(depth, nblk),
        in_specs=[pl.BlockSpec((SUBLANES, d), lambda l, n: (0, 0)),
                  pl.BlockSpec((1, d, d), lambda l, n: (l, 0, n)),
                  pl.BlockSpec((1, 1, d), lambda l, n: (l, 0, n))],
        out_specs=pl.BlockSpec((1, SUBLANES, d), lambda l, n: (l, 0, n)),
        out_shape=jax.ShapeDtypeStruct((depth, SUBLANES, d3), jnp.float32),
        compiler_params=_params("parallel", "parallel"),
        name="modulation",
    )(cvec, w_mod, b_mod.reshape(depth, 1, d3))


def _stream_view(x, col_major):
    b, l, d = x.shape
    if col_major:
        return x.reshape(b, l // GRID_W, GRID_W * d)
    return x


def _stream_specs(l, d, tm, col_major, nb_grid_prefix):
    def wrap(fn):
        return lambda *g: fn(*g[nb_grid_prefix:])
    if col_major:
        rows = l // GRID_W
        ncol = tm // rows
        assert tm == rows * ncol
        main = pl.BlockSpec((1, rows, ncol * d), wrap(lambda b, i: (b, 0, i)))
        prev = pl.BlockSpec((1, SUBLANES, d),
                            wrap(lambda b, i: (b, rows // SUBLANES - 1, jnp.maximum(i * ncol - 1, 0))))
        nxt = pl.BlockSpec((1, SUBLANES, d),
                           wrap(lambda b, i: (b, 0, jnp.minimum((i + 1) * ncol, GRID_W - 1))))
    else:
        per = tm // SUBLANES
        last = l // SUBLANES - 1
        main = pl.BlockSpec((1, tm, d), wrap(lambda b, i: (b, i, 0)))
        prev = pl.BlockSpec((1, SUBLANES, d), wrap(lambda b, i: (b, jnp.maximum(i * per - 1, 0), 0)))
        nxt = pl.BlockSpec((1, SUBLANES, d), wrap(lambda b, i: (b, jnp.minimum((i + 1) * per, last), 0)))
    return main, prev, nxt


def _tile_rows(l, col_major):
    tm = min(l, STREAM_TILE)
    if col_major:
        rows = l // GRID_W
        return max(tm // rows, 1) * rows
    return tm


def _tile_tokens(ref, d):
    blk = ref[0]
    ncol = blk.shape[1] // d
    if ncol == 1:
        return blk
    return jnp.concatenate([blk[:, j * d:(j + 1) * d] for j in range(ncol)], axis=0)


def _inproj_kernel(x_ref, xp_ref, xn_ref, sh_ref, sc_ref, w_ref, cw_ref, cb_ref, o_ref,
                   *, n_conv, conv_k, pad_l, tm):
    i = pl.program_id(1)
    nt = pl.num_programs(1)
    d = w_ref.shape[0]
    e = o_ref.shape[3]
    scale = 1.0 + sc_ref[0]
    shift = sh_ref[0]
    mod = lambda v: v * scale + shift
    h_prev = jnp.where(i > 0, mod(xp_ref[0]), 0.0)
    h_next = jnp.where(i < nt - 1, mod(xn_ref[0]), 0.0)
    hh = jnp.concatenate([h_prev, mod(_tile_tokens(x_ref, d)), h_next], axis=0).astype(jnp.bfloat16)
    ext = tm + 2 * SUBLANES
    for n in range(o_ref.shape[0]):
        for c0 in range(0, e, LANE_CHUNK):
            sl = slice(c0, min(c0 + LANE_CHUNK, e))
            col = slice(n * e + sl.start, n * e + sl.stop)
            u = jnp.dot(hh, w_ref[:, col], preferred_element_type=jnp.float32)
            if n < n_conv:
                acc = cb_ref[:, col]
                for k in range(conv_k):
                    tap = u if k == pad_l else pltpu.roll(u, (pad_l - k) % ext, 0)
                    acc = acc + cw_ref[k:k + 1, col] * tap[SUBLANES:SUBLANES + tm]
            else:
                acc = u[SUBLANES:SUBLANES + tm]
            o_ref[n, 0, :, sl] = acc.astype(o_ref.dtype)


def _inproj(x, shift, scale, w_bf16, conv_w, conv_b, *, e, pad_l, col_major):
    b, l, d = x.shape
    nt_groups = w_bf16.shape[1] // e
    conv_k = conv_w.shape[0]
    n_conv = conv_w.shape[1] // e
    tm = _tile_rows(l, col_major)
    ntile = l // tm
    main, prev, nxt = _stream_specs(l, d, tm, col_major, 0)
    xv = _stream_view(x, col_major)
    kern = functools.partial(_inproj_kernel, n_conv=n_conv, conv_k=conv_k, pad_l=pad_l, tm=tm)
    const = lambda shape: pl.BlockSpec(shape, lambda bb, i: (0, 0), pipeline_mode=pl.Buffered(1))
    return pl.pallas_call(
        kern,
        grid=(b, ntile),
        in_specs=[main, prev, nxt,
                  pl.BlockSpec((1, 1, d), lambda bb, i: (bb, 0, 0)),
                  pl.BlockSpec((1, 1, d), lambda bb, i: (bb, 0, 0)),
                  const(w_bf16.shape), const(conv_w.shape), const((1, conv_w.shape[1]))],
        out_specs=pl.BlockSpec((nt_groups, 1, tm, e), lambda bb, i: (0, bb, i, 0)),
        out_shape=jax.ShapeDtypeStruct((nt_groups, b, l, e), jnp.bfloat16),
        compiler_params=_params("parallel", "parallel"),
        name="inproj_conv",
    )(xv, xv, xv, shift, scale, w_bf16, conv_w, conv_b.reshape(1, -1))


def _rg_scan_kernel(*refs, reverse, add_prev, write_seq, out_alpha, tm, gw, ngroups):
    it = iter(refs)
    xc_ref = next(it)
    prev_ref = next(it) if add_prev else None
    h0_ref, wg_ref, br_ref, bi_ref, lam_ref = next(it), next(it), next(it), next(it), next(it)
    out_refs = [next(it) for _ in range(6)] if out_alpha is not None else None
    o_ref = next(it) if write_seq else None
    st_ref = next(it)
    a_scr, b_scr, carry = next(it), next(it), next(it)
    seq_ref = next(it) if out_alpha is not None else o_ref

    i = pl.program_id(1)
    nchunk = pl.num_programs(1)

    @pl.when(i == 0)
    def _():
        carry[...] = h0_ref[0]

    lam = lam_ref[...]
    neg = -lam
    softplus = jnp.maximum(neg, 0.0) + jnp.log(1.0 + jnp.exp(-jnp.abs(neg)))
    rate = (0.5 * RG_C) * softplus
    for g in range(ngroups):
        sl = slice(g * gw, (g + 1) * gw)
        xg = xc_ref[0, 0, :, sl]
        pre = jnp.dot(xg, wg_ref[g], preferred_element_type=jnp.float32)
        t_r = jnp.tanh(pre[:, :gw] + br_ref[:, sl])
        t_i = jnp.tanh(pre[:, gw:] + bi_ref[:, sl])
        neg_log_a = rate[:, sl] * t_r + rate[:, sl]
        a = jnp.exp(-neg_log_a)
        a_scr[:, sl] = a
        s = jnp.tanh(neg_log_a) * (a * a + 1.0)
        mult = jnp.where(s > 0.0, s * lax.rsqrt(s), 0.0)
        b_scr[:, sl] = (0.5 * mult) * ((t_i + 1.0) * xg.astype(jnp.float32))

    e = a_scr.shape[1]
    row = lax.broadcasted_iota(jnp.int32, (SUBLANES, e), 0)
    nblk = tm // BF16_ROWS

    def scan8(r0, c):
        a = a_scr[pl.ds(r0, SUBLANES), :]
        h = b_scr[pl.ds(r0, SUBLANES), :]
        for dist in (1, 2, 4):
            shift = (SUBLANES - dist) if reverse else dist
            valid = (row + dist < SUBLANES) if reverse else (row >= dist)
            a_sh = jnp.where(valid, pltpu.roll(a, shift, 0), 1.0)
            h_sh = jnp.where(valid, pltpu.roll(h, shift, 0), 0.0)
            h = a * h_sh + h
            a = a * a_sh
        h = a * c + h
        return h, (h[0:1, :] if reverse else h[SUBLANES - 1:SUBLANES, :])

    def block(k, c):
        kk = (nblk - 1 - k) if reverse else k
        r0 = pl.multiple_of(kk * BF16_ROWS, BF16_ROWS)
        halves = [None, None]
        for hh in ((1, 0) if reverse else (0, 1)):
            halves[hh], c = scan8(r0 + hh * SUBLANES, c)
        if write_seq:
            h = jnp.concatenate(halves, axis=0)
            if add_prev:
                h = h + prev_ref[0, pl.ds(r0, BF16_ROWS), :].astype(jnp.float32)
            if out_alpha is None:
                seq_ref[0, pl.ds(r0, BF16_ROWS), :] = h.astype(seq_ref.dtype)
            else:
                seq_ref[pl.ds(r0, BF16_ROWS), :] = h.astype(seq_ref.dtype)
        return c

    c_fin = lax.fori_loop(0, nblk, block, carry[...], unroll=2)
    carry[...] = c_fin

    @pl.when(i == nchunk - 1)
    def _():
        st_ref[0] = c_fin

    if out_alpha is not None:
        g_ref, w_ref, x_ref, gate_ref, lg_ref, lb_ref = out_refs
        _outproj_tile(seq_ref[...], g_ref[0, 0], w_ref, x_ref, gate_ref, lg_ref, lb_ref, o_ref, out_alpha)


def _rg_scan(proj, h0, wg, b_r, b_i, lam, *, reverse, prev=None, write_seq=True, out=None):
    _, b, l, e = proj.shape
    tm = min(l, SCAN_CHUNK)
    nchunk = l // tm
    ngroups, gw, _ = wg.shape
    pos = (lambda i: nchunk - 1 - i) if reverse else (lambda i: i)
    add_prev = prev is not None
    in_specs = [pl.BlockSpec((1, 1, tm, e), lambda bb, i: (0, bb, pos(i), 0))]
    args = [proj]
    if add_prev:
        in_specs.append(pl.BlockSpec((1, tm, e), lambda bb, i: (bb, pos(i), 0)))
        args.append(prev)
    in_specs += [pl.BlockSpec((1, 1, e), lambda bb, i: (bb, 0, 0)),
                 pl.BlockSpec((ngroups, gw, 2 * gw), lambda bb, i: (0, 0, 0)),
                 pl.BlockSpec((1, e), lambda bb, i: (0, 0)),
                 pl.BlockSpec((1, e), lambda bb, i: (0, 0)),
                 pl.BlockSpec((1, e), lambda bb, i: (0, 0))]
    args += [h0, wg, 0.5 * b_r.reshape(1, e), 0.5 * b_i.reshape(1, e), lam.reshape(1, e)]
    out_specs, out_shape = [], []
    scratch = [pltpu.VMEM((tm, e), jnp.float32), pltpu.VMEM((tm, e), jnp.float32),
               pltpu.VMEM((1, e), jnp.float32)]
    out_alpha = None
    if out is not None:
        w_out, x, gate, ln_g, ln_b, out_alpha, col_major = out
        d = x.shape[-1]
        assert write_seq and tm == _tile_rows(l, col_major)
        main, _, _ = _stream_specs(l, d, tm, col_major, 0)
        tile = pl.BlockSpec(main.block_shape, lambda bb, i: main.index_map(bb, pos(i)))
        xv = _stream_view(x, col_major)
        in_specs += [pl.BlockSpec((1, 1, tm, e), lambda bb, i: (1, bb, pos(i), 0)),
                     pl.BlockSpec((e, d), lambda bb, i: (0, 0)),
                     tile,
                     pl.BlockSpec((1, 1, d), lambda bb, i: (bb, 0, 0)),
                     pl.BlockSpec((1, d), lambda bb, i: (0, 0)),
                     pl.BlockSpec((1, d), lambda bb, i: (0, 0))]
        args += [proj, w_out, xv, gate, ln_g.reshape(1, d), ln_b.reshape(1, d)]
        out_specs.append(tile)
        out_shape.append(jax.ShapeDtypeStruct(xv.shape, jnp.float32))
        scratch.append(pltpu.VMEM((tm, e), jnp.bfloat16))
    elif write_seq:
        out_specs.append(pl.BlockSpec((1, tm, e), lambda bb, i: (bb, pos(i), 0)))
        out_shape.append(jax.ShapeDtypeStruct((b, l, e), jnp.bfloat16))
    out_specs.append(pl.BlockSpec((1, 1, e), lambda bb, i: (bb, 0, 0)))
    out_shape.append(jax.ShapeDtypeStruct((b, 1, e), jnp.float32))
    kern = functools.partial(_rg_scan_kernel, reverse=reverse, add_prev=add_prev, write_seq=write_seq,
                             out_alpha=out_alpha, tm=tm, gw=gw, ngroups=ngroups)
    res = pl.pallas_call(
        kern,
        grid=(b, nchunk),
        in_specs=in_specs,
        out_specs=out_specs,
        out_shape=out_shape,
        scratch_shapes=scratch,
        compiler_params=_params("parallel", "arbitrary"),
        name="rg_scan_bwd" if reverse else "rg_scan_fwd",
    )(*args)
    if out is not None:
        return res[0].reshape(x.shape), res[1]
    if write_seq:
        return res[0], res[1]
    return None, res[0]


def _gate_weights(w_r, w_i):
    nb, bw, _ = w_r.shape
    g = RG_GATE_GROUP
    ng = nb // g
    eye = jnp.eye(g, dtype=w_r.dtype)

    def bd(w):
        w = w.reshape(ng, g, bw, bw)
        return jnp.einsum("nhij,hk->nhikj", w, eye).reshape(ng, g * bw, g * bw)

    return (0.5 * jnp.concatenate([bd(w_r), bd(w_i)], axis=-1)).astype(jnp.bfloat16)


def _outproj_tile(y, g, w_ref, x_ref, gate_ref, lg_ref, lb_ref, o_ref, alpha):
    d = w_ref.shape[1]
    hg = 0.5 * g.astype(jnp.float32)
    z = (y.astype(jnp.float32) * (hg * (jnp.tanh(hg) + 1.0))).astype(jnp.bfloat16)
    out = jnp.dot(z, w_ref[...], preferred_element_type=jnp.float32)
    r = alpha * _tile_tokens(x_ref, d) + gate_ref[0] * out
    mu = jnp.mean(r, axis=-1, keepdims=True)
    cen = r - mu
    var = jnp.mean(cen * cen, axis=-1, keepdims=True)
    res = cen * lax.rsqrt(var + LN_EPS) * lg_ref[...] + lb_ref[...]
    rows = o_ref.shape[1]
    for j in range(o_ref.shape[2] // d):
        o_ref[0, :, j * d:(j + 1) * d] = res[j * rows:(j + 1) * rows]


def _outproj_kernel(y_ref, g_ref, w_ref, x_ref, gate_ref, lg_ref, lb_ref, o_ref, *, alpha):
    _outproj_tile(y_ref[0], g_ref[0, 0], w_ref, x_ref, gate_ref, lg_ref, lb_ref, o_ref, alpha)


def _outproj(y, proj, g_group, w_bf16, x, gate, ln_g, ln_b, *, alpha, col_major):
    b, l, e = y.shape
    d = x.shape[-1]
    tm = _tile_rows(l, col_major)
    ntile = l // tm
    main, _, _ = _stream_specs(l, d, tm, col_major, 0)
    xv = _stream_view(x, col_major)
    out = pl.pallas_call(
        functools.partial(_outproj_kernel, alpha=alpha),
        grid=(b, ntile),
        in_specs=[pl.BlockSpec((1, tm, e), lambda bb, i: (bb, i, 0)),
                  pl.BlockSpec((1, 1, tm, e), lambda bb, i: (g_group, bb, i, 0)),
                  pl.BlockSpec((e, d), lambda bb, i: (0, 0)),
                  main,
                  pl.BlockSpec((1, 1, d), lambda bb, i: (bb, 0, 0)),
                  pl.BlockSpec((1, d), lambda bb, i: (0, 0)),
                  pl.BlockSpec((1, d), lambda bb, i: (0, 0))],
        out_specs=main,
        out_shape=jax.ShapeDtypeStruct(xv.shape, jnp.float32),
        compiler_params=_params("parallel", "parallel"),
        name="outproj_ln",
    )(y, proj, w_bf16, xv, gate, ln_g.reshape(1, d), ln_b.reshape(1, d))
    return out.reshape(x.shape)


def _filter_positions(l, fh):
    f32 = jnp.float32
    t = jnp.linspace(0.0, 1.0, l, dtype=f32)[:, None]
    bands = jnp.linspace(1e-4, HY_EMB_BANDS - 1, HY_EMB_BANDS, dtype=f32)
    w = (2.0 * math.pi) * jnp.arange(l, dtype=f32)[:, None] / l
    z = jnp.concatenate([t, jnp.cos(bands * w), -jnp.sin(bands * w)], axis=-1)
    z_rev = jnp.concatenate([z[:1], z[:0:-1]], axis=0)
    pad = lambda a: jnp.pad(a, ((0, 0), (0, fh - a.shape[1])))
    return jnp.concatenate([pad(z), pad(z_rev)], axis=1), jnp.stack([z[:, 0:1], z_rev[:, 0:1]])


def _hidden_kernel(z_ref, w1_ref, b1_ref, w2_ref, b2_ref, fr_ref, o_ref):
    fr = fr_ref[...]
    h = jnp.dot(z_ref[...], w1_ref[...], preferred_element_type=jnp.float32, precision=_HIGHEST)
    h = jnp.sin(fr * (h + b1_ref[...]))
    h = jnp.dot(h, w2_ref[...], preferred_element_type=jnp.float32, precision=_HIGHEST)
    o_ref[...] = jnp.sin(fr * (h + b2_ref[...]))


def _filter_hidden(zz, w1, b1, w2, b2, freq):
    n, zp = zz.shape
    fh = w1.shape[1]
    tr = min(n, 1024)
    both = lambda v: jnp.concatenate([v, v]).reshape(1, 2 * fh)
    blockdiag = lambda w: jnp.kron(jnp.eye(2, dtype=w.dtype), w)
    w1p = blockdiag(jnp.pad(w1, ((0, fh - w1.shape[0]), (0, 0))))
    full = lambda shape: pl.BlockSpec(shape, lambda i: (0,) * len(shape))
    return pl.pallas_call(
        _hidden_kernel,
        grid=(n // tr,),
        in_specs=[pl.BlockSpec((tr, zp), lambda i: (i, 0)), full((zp, zp)), full((1, zp)),
                  full((zp, zp)), full((1, zp)), full((1, zp))],
        out_specs=pl.BlockSpec((tr, zp), lambda i: (i, 0)),
        out_shape=jax.ShapeDtypeStruct((n, zp), jnp.float32),
        compiler_params=_params("parallel"),
        name="filter_hidden",
    )(zz, w1p, both(b1), blockdiag(w2), both(b2), both(freq))


def _split_bf16(a):
    hi = a.astype(jnp.bfloat16)
    return hi, (a - hi.astype(jnp.float32)).astype(jnp.bfloat16)


def _filter_kernel(hid_ref, t_ref, w3_ref, ad_ref, k_ref, s_ref, *, tr, n_total):
    d = pl.program_id(1)
    i = pl.program_id(2)
    last = (d == pl.num_programs(1) - 1) & (i == pl.num_programs(2) - 1)
    hi, lo = _split_bf16(hid_ref[...])
    h = jnp.dot(jnp.concatenate([hi, lo, hi], axis=1), w3_ref[0, 0], preferred_element_type=jnp.float32)
    window = jnp.exp(-t_ref[0] * ad_ref[...]) + HY_MOD_SHIFT
    row = lax.broadcasted_iota(jnp.int32, h.shape, 0)
    gap = (row == 0) & (d == 1) & (i == 0)
    k = jnp.where(gap, 0.0, h * window)
    k_ref[0] = k

    @pl.when((d == 0) & (i == 0))
    def _():
        s_ref[...] = jnp.zeros_like(s_ref)

    s_ref[0] += jnp.sum(jnp.abs(k).reshape(tr // SUBLANES, SUBLANES, -1), axis=0)

    @pl.when(last)
    def _():
        tot = jnp.sum(s_ref[0], axis=0, keepdims=True)
        s_ref[0] = jnp.broadcast_to(1.0 / (tot * n_total), s_ref.shape[1:])


def _filter_weights(w3, fh, e, compact):
    order = w3.shape[1] // (2 * e)
    max_decay = math.log(HY_DECAY_TARGET) / HY_FAST_DECAY_PCT
    min_decay = math.log(HY_DECAY_TARGET) / HY_SLOW_DECAY_PCT
    absdelta = jnp.abs(jnp.linspace(min_decay, max_decay, e, dtype=jnp.float32)).reshape(1, e)
    w3 = w3.reshape(fh, order, 2, e).transpose(1, 2, 0, 3)
    if compact:
        w_hi, w_lo = _split_bf16(w3)
        return jnp.concatenate([w_hi, w_hi, w_lo, jnp.zeros_like(w_hi)], axis=2), absdelta
    w3 = jnp.stack([jnp.pad(w3[:, 0], ((0, 0), (0, fh), (0, 0))),
                    jnp.pad(w3[:, 1], ((0, 0), (fh, 0), (0, 0)))], axis=1)
    w_hi, w_lo = _split_bf16(w3)
    return jnp.concatenate([w_hi, w_hi, w_lo], axis=2), absdelta


def _filter_fft_a_kernel(hid_ref, t_ref, w3_ref, ad_ref, m_ref, re_ref, im_ref, s_ref, *, n_total):
    j = pl.program_id(2)
    k1h, rows, fh2 = hid_ref.shape
    ec = ad_ref.shape[1]
    hi, lo = _split_bf16(hid_ref[...].reshape(k1h * rows, fh2))
    fh = fh2 // 2
    halves = []
    for d in range(2):
        own = slice(d * fh, (d + 1) * fh)
        stacked = jnp.concatenate([hi[:, own], lo[:, own], hi[:, own], jnp.zeros_like(hi[:, own])], axis=1)
        h = jnp.dot(stacked, w3_ref[0, d], preferred_element_type=jnp.float32)
        window = jnp.exp(-t_ref[d].reshape(k1h * rows, 1) * ad_ref[...]) + HY_MOD_SHIFT
        k = h * window
        if d == 1:
            row = lax.broadcasted_iota(jnp.int32, k.shape, 0)
            k = jnp.where((row == 0) & (j == 0), 0.0, k)
        halves.append(k)

    @pl.when(j == 0)
    def _():
        s_ref[...] = jnp.zeros_like(s_ref)

    k = jnp.concatenate(halves, axis=0)
    s_ref[0] += jnp.sum(jnp.abs(k).reshape(-1, SUBLANES, ec), axis=0)

    @pl.when(j == pl.num_programs(2) - 1)
    def _():
        tot = jnp.sum(s_ref[0], axis=0, keepdims=True)
        s_ref[0] = jnp.broadcast_to(1.0 / (tot * n_total), s_ref.shape[1:])

    k3 = k.reshape(2 * k1h, rows, ec)
    half = m_ref.shape[0] // 2
    f1p = half // FFT_J
    re_parts, im_parts = [], []
    for h in range(rows // FFT_J):
        x = k3[:, h * FFT_J:(h + 1) * FFT_J, :].reshape(2 * k1h * FFT_J, ec)
        r = jnp.dot(m_ref[...], x.astype(jnp.bfloat16), preferred_element_type=jnp.float32)
        re_parts.append(r[:half].reshape(f1p, FFT_J, ec))
        im_parts.append(r[half:].reshape(f1p, FFT_J, ec))
    re_ref[0] = jnp.concatenate(re_parts, axis=1).astype(jnp.bfloat16)
    im_ref[0] = jnp.concatenate(im_parts, axis=1).astype(jnp.bfloat16)


def _filter_fft_a(hid, tcol, w3, e, tabs):
    l, fh2 = hid.shape
    n2 = FFT_N2
    k1h = l // n2
    f1p = tabs["f1p"]
    mat = tabs["a_full"]
    w_stack, absdelta = _filter_weights(w3, fh2 // 2, e, compact=True)
    order = w_stack.shape[0]
    ec = min(e, FFT_EC)
    out_spec = pl.BlockSpec((1, f1p, BF16_ROWS, ec), lambda o, cc, j: (o, 0, j, cc))
    shape = jax.ShapeDtypeStruct((order, f1p, n2, e), jnp.bfloat16)
    return pl.pallas_call(
        functools.partial(_filter_fft_a_kernel, n_total=float(2 * l)),
        grid=(order, e // ec, n2 // BF16_ROWS),
        in_specs=[pl.BlockSpec((k1h, BF16_ROWS, fh2), lambda o, cc, j: (0, j, 0)),
                  pl.BlockSpec((2, k1h, BF16_ROWS, 1), lambda o, cc, j: (0, 0, j, 0)),
                  pl.BlockSpec((1, 2, w_stack.shape[2], ec), lambda o, cc, j: (o, 0, 0, cc)),
                  pl.BlockSpec((1, ec), lambda o, cc, j: (0, cc)),
                  pl.BlockSpec(mat.shape, lambda o, cc, j: (0, 0))],
        out_specs=[out_spec, out_spec,
                   pl.BlockSpec((1, SUBLANES, ec), lambda o, cc, j: (o, 0, cc))],
        out_shape=[shape, shape, jax.ShapeDtypeStruct((order, SUBLANES, e), jnp.float32)],
        compiler_params=_params("parallel", "parallel", "arbitrary"),
        name="filter_fft_a",
    )(hid.reshape(k1h, n2, fh2), tcol.reshape(2, k1h, n2, 1), w_stack, absdelta, mat)


def _filters(hid, tcol, w3, e):
    l, fh2 = hid.shape
    n = 2 * l
    tr = min(l, 512)
    nl = l // tr
    w_stack, absdelta = _filter_weights(w3, fh2 // 2, e, compact=False)
    order = w_stack.shape[0]
    return pl.pallas_call(
        functools.partial(_filter_kernel, tr=tr, n_total=float(n)),
        grid=(order, 2, nl),
        in_specs=[pl.BlockSpec((tr, fh2), lambda o, d, i: (i, 0)),
                  pl.BlockSpec((1, tr, 1), lambda o, d, i: (d, i, 0)),
                  pl.BlockSpec((1, 1, 3 * fh2, e), lambda o, d, i: (o, d, 0, 0)),
                  pl.BlockSpec((1, e), lambda o, d, i: (0, 0))],
        out_specs=[pl.BlockSpec((1, tr, e), lambda o, d, i: (o, d * nl + i, 0)),
                   pl.BlockSpec((1, SUBLANES, e), lambda o, d, i: (o, 0, 0))],
        out_shape=[jax.ShapeDtypeStruct((order, n, e), jnp.float32),
                   jax.ShapeDtypeStruct((order, SUBLANES, e), jnp.float32)],
        compiler_params=_params("parallel", "arbitrary", "arbitrary"),
        name="filter_gen",
    )(hid, tcol, w_stack, absdelta)


def _angles(num, den):
    ang = 2.0 * np.pi * (np.asarray(num, np.int64) % den).astype(np.float64) / den
    return np.cos(ang), np.sin(ang)


@functools.lru_cache(maxsize=None)
def _fft_tables(n1):
    n2 = FFT_N2
    n = n1 * n2
    f1 = n1 // 2 + 1
    f1p = _round_up(f1, 2)
    fr = np.arange(f1)[:, None]
    eye = np.eye(FFT_J)

    def stage_a(k1):
        c, s = _angles(fr * np.arange(k1)[None, :], n1)
        m = np.zeros((2 * f1p, k1))
        m[:f1], m[f1p:f1p + f1] = c, -s
        return np.kron(m, eye)

    nyq = n1 // 2
    wgt = np.full((nyq,), 2.0)
    wgt[0] = 1.0
    c, s = _angles(np.arange(n1 // 2)[:, None] * np.arange(nyq)[None, :], n1)
    m_c = np.concatenate([c * wgt, -s * wgt], axis=1)

    c2, s2 = _angles(np.arange(n2)[:, None] * np.arange(n2)[None, :], n2)
    cphi, sphi = _angles(fr * np.arange(n2)[None, :], n)
    phase = np.stack([cphi, sphi], axis=1)
    bf = lambda a: jnp.asarray(a, jnp.float32).astype(jnp.bfloat16)
    f32 = lambda a: jnp.asarray(a, jnp.float32)
    return dict(f1=f1, f1p=f1p, a_data=bf(stage_a(n1 // 2)), a_full=bf(stage_a(n1)),
                m_c=bf(np.kron(m_c, eye)), base_c=f32(c2), base_s=f32(s2),
                phase_row=f32(phase[:, :, None, :]), phase_col=f32(phase[..., None]))


def _twiddled_dft_kernel(c_ref, s_ref, row_ref, col_ref, mf_ref, mi_ref):
    c, s = c_ref[...], s_ref[...]
    n2 = c.shape[0]
    bf = jnp.bfloat16
    cr, sr = row_ref[0, 0], row_ref[0, 1]
    cf, sf = (c * cr - s * sr).astype(bf), (s * cr + c * sr).astype(bf)
    mf_ref[0, :n2, :n2] = cf
    mf_ref[0, :n2, n2:] = sf
    mf_ref[0, n2:, :n2] = -sf
    mf_ref[0, n2:, n2:] = cf
    cc, sc = col_ref[0, 0], col_ref[0, 1]
    ci, si = (c * cc - s * sc).astype(bf), (s * cc + c * sc).astype(bf)
    mi_ref[0, :n2, :n2] = ci
    mi_ref[0, :n2, n2:] = -si
    mi_ref[0, n2:, :n2] = si
    mi_ref[0, n2:, n2:] = ci


def _twiddled_dfts(tabs):
    f1 = tabs["f1"]
    n2 = FFT_N2
    shape = jax.ShapeDtypeStruct((f1, 2 * n2, 2 * n2), jnp.bfloat16)
    base = pl.BlockSpec((n2, n2), lambda f: (0, 0))
    out = pl.BlockSpec((1, 2 * n2, 2 * n2), lambda f: (f, 0, 0))
    return pl.pallas_call(
        _twiddled_dft_kernel,
        grid=(f1,),
        in_specs=[base, base, pl.BlockSpec((1, 2, 1, n2), lambda f: (f, 0, 0, 0)),
                  pl.BlockSpec((1, 2, n2, 1), lambda f: (f, 0, 0, 0))],
        out_specs=[out, out],
        out_shape=[shape, shape],
        compiler_params=_params("parallel"),
        name="twiddled_dft_tables",
    )(tabs["base_c"], tabs["base_s"], tabs["phase_row"], tabs["phase_col"])


def _fft_a_kernel(*refs, scaled):
    it = iter(refs)
    x_ref = next(it)
    sc_ref = next(it) if scaled else None
    m_ref, re_ref, im_ref = next(it), next(it), next(it)
    k1, ec = x_ref.shape[2], x_ref.shape[4]
    half = m_ref.shape[0] // 2
    f1p = half // FFT_J
    re_parts, im_parts = [], []
    xf = x_ref[0, 0].astype(jnp.float32)
    for h in range(x_ref.shape[3] // FFT_J):
        x = xf[:, h * FFT_J:(h + 1) * FFT_J, :].reshape(k1 * FFT_J, ec)
        if scaled:
            x = x * sc_ref[0]
        r = jnp.dot(m_ref[...], x.astype(jnp.bfloat16), preferred_element_type=jnp.float32)
        re_parts.append(r[:half].reshape(f1p, FFT_J, ec))
        im_parts.append(r[half:].reshape(f1p, FFT_J, ec))
    re_ref[0] = jnp.concatenate(re_parts, axis=1).astype(jnp.bfloat16)
    im_ref[0] = jnp.concatenate(im_parts, axis=1).astype(jnp.bfloat16)


def _fft_stage_a(arr, group, tabs, scale=None):
    _, bt, rows, e = arr.shape
    k1 = rows // FFT_N2
    f1p = tabs["f1p"]
    mat = tabs["a_data"] if k1 * FFT_J == tabs["a_data"].shape[1] else tabs["a_full"]
    assert mat.shape[1] == k1 * FFT_J
    ec = min(e, FFT_EC)
    av = arr.reshape(arr.shape[0], bt, k1, FFT_N2, e)
    in_specs = [pl.BlockSpec((1, 1, k1, BF16_ROWS, ec), lambda bb, j, cc: (group, bb, 0, j, cc))]
    args = [av]
    if scale is not None:
        in_specs.append(pl.BlockSpec((1, 1, ec), lambda bb, j, cc: (bb, 0, cc)))
        args.append(scale)
    in_specs.append(pl.BlockSpec(mat.shape, lambda bb, j, cc: (0, 0)))
    args.append(mat)
    out_spec = pl.BlockSpec((1, f1p, BF16_ROWS, ec), lambda bb, j, cc: (bb, 0, j, cc))
    shape = jax.ShapeDtypeStruct((bt, f1p, FFT_N2, e), jnp.bfloat16)
    return pl.pallas_call(
        functools.partial(_fft_a_kernel, scaled=scale is not None),
        grid=(bt, FFT_N2 // BF16_ROWS, e // ec),
        in_specs=in_specs,
        out_specs=[out_spec, out_spec],
        out_shape=[shape, shape],
        compiler_params=_params("parallel", "parallel", "parallel"),
        name="fft_stage_a",
    )(*args)


def _lane_chunks(e, width=256):
    return [slice(c0, min(c0 + width, e)) for c0 in range(0, e, width)]


def _fft_b_filter_kernel(re_ref, im_ref, sc_ref, mf_ref, kre_ref, kim_ref):
    for b in range(re_ref.shape[0]):
        for sl in _lane_chunks(re_ref.shape[3]):
            x = jnp.concatenate([re_ref[b, 0, :, sl], im_ref[b, 0, :, sl]], axis=0)
            u = jnp.dot(mf_ref[0], x, preferred_element_type=jnp.float32) * sc_ref[b, 0:1, sl]
            kre_ref[b, 0, :, sl] = u[:FFT_N2].astype(kre_ref.dtype)
            kim_ref[b, 0, :, sl] = u[FFT_N2:].astype(kim_ref.dtype)


def _fft_b_conv_kernel(re_ref, im_ref, kre_ref, kim_ref, mf_ref, mi_ref, ore_ref, oim_ref):
    for sl in _lane_chunks(re_ref.shape[3]):
        k_re = kre_ref[0, 0, :, sl].astype(jnp.float32)
        k_im = kim_ref[0, 0, :, sl].astype(jnp.float32)
        for b in range(re_ref.shape[0]):
            x = jnp.concatenate([re_ref[b, 0, :, sl], im_ref[b, 0, :, sl]], axis=0)
            u = jnp.dot(mf_ref[0], x, preferred_element_type=jnp.float32)
            u_re, u_im = u[:FFT_N2], u[FFT_N2:]
            p = jnp.concatenate([u_re * k_re - u_im * k_im, u_re * k_im + u_im * k_re], axis=0)
            v = jnp.dot(mi_ref[0], p.astype(jnp.bfloat16), preferred_element_type=jnp.float32)
            ore_ref[b, 0, :, sl] = v[:FFT_N2].astype(jnp.bfloat16)
            oim_ref[b, 0, :, sl] = v[FFT_N2:].astype(jnp.bfloat16)


def _fft_stage_b_filter(a_re, a_im, inv_s, mf, tabs):
    bt, _, n2, e = a_re.shape
    f1 = tabs["f1"]
    blk = pl.BlockSpec((bt, 1, n2, e), lambda f: (0, f, 0, 0))
    shape = jax.ShapeDtypeStruct((bt, f1, n2, e), jnp.bfloat16)
    return pl.pallas_call(
        _fft_b_filter_kernel,
        grid=(f1,),
        in_specs=[blk, blk, pl.BlockSpec(inv_s.shape, lambda f: (0, 0, 0)),
                  pl.BlockSpec((1, 2 * n2, 2 * n2), lambda f: (f, 0, 0))],
        out_specs=[blk, blk],
        out_shape=[shape, shape],
        compiler_params=_params("parallel"),
        name="fft_stage_b_filter",
    )(a_re, a_im, inv_s, mf)


def _fft_stage_b_conv(a_re, a_im, kf_re, kf_im, order_idx, mf, mi, tabs):
    bt, f1p, n2, e = a_re.shape
    f1 = tabs["f1"]
    fc = lambda f: jnp.minimum(f, f1 - 1)
    blk = pl.BlockSpec((bt, 1, n2, e), lambda f: (0, f, 0, 0))
    kblk = pl.BlockSpec((1, 1, n2, e), lambda f: (order_idx, fc(f), 0, 0))
    wblk = pl.BlockSpec((1, 2 * n2, 2 * n2), lambda f: (fc(f), 0, 0))
    shape = jax.ShapeDtypeStruct((bt, f1p, n2, e), jnp.bfloat16)
    return pl.pallas_call(
        _fft_b_conv_kernel,
        grid=(f1p,),
        in_specs=[blk, blk, kblk, kblk, wblk, wblk],
        out_specs=[blk, blk],
        out_shape=[shape, shape],
        compiler_params=_params("parallel"),
        name="fft_stage_b_conv",
    )(a_re, a_im, kf_re, kf_im, mf, mi)


def _fft_c_kernel(*refs, chain):
    re_ref, im_ref, m_ref, u_ref, mul_ref, d_ref = refs[:6]
    if chain:
        ma_ref, o_ref, are_ref, aim_ref = refs[6:]
    else:
        o_ref, = refs[6:]
    k1, ec = o_ref.shape[1], o_ref.shape[3]
    f1p = re_ref.shape[1]
    nyq = m_ref.shape[1] // (2 * FFT_J)
    re = re_ref[0].astype(jnp.float32)
    im = im_ref[0].astype(jnp.float32)
    uf = u_ref[0, 0].astype(jnp.float32)
    mf = mul_ref[0, 0].astype(jnp.float32)
    parts, re_parts, im_parts = [], [], []
    for h in range(o_ref.shape[2] // FFT_J):
        sl = slice(h * FFT_J, (h + 1) * FFT_J)
        x = jnp.concatenate([re[:nyq, sl, :].reshape(nyq * FFT_J, ec),
                             im[:nyq, sl, :].reshape(nyq * FFT_J, ec)], axis=0)
        y = jnp.dot(m_ref[...], x.astype(jnp.bfloat16), preferred_element_type=jnp.float32)
        r_nyq = re[nyq, sl, :]
        y = (y.reshape(k1 // 2, 2, FFT_J, ec) + jnp.stack([r_nyq, -r_nyq])[None]).reshape(k1 * FFT_J, ec)
        u = uf[:, sl, :].reshape(k1 * FFT_J, ec)
        mul = mf[:, sl, :].reshape(k1 * FFT_J, ec)
        z = (mul * (y + u * d_ref[...])).astype(jnp.bfloat16)
        parts.append(z.astype(jnp.float32).reshape(k1, FFT_J, ec))
        if chain:
            r = jnp.dot(ma_ref[...], z, preferred_element_type=jnp.float32)
            half = ma_ref.shape[0] // 2
            re_parts.append(r[:half].reshape(f1p, FFT_J, ec))
            im_parts.append(r[half:].reshape(f1p, FFT_J, ec))
    o_ref[0] = jnp.concatenate(parts, axis=1).astype(o_ref.dtype)
    if chain:
        are_ref[0] = jnp.concatenate(re_parts, axis=1).astype(jnp.bfloat16)
        aim_ref[0] = jnp.concatenate(im_parts, axis=1).astype(jnp.bfloat16)


def _fft_stage_c(b_re, b_im, u_arr, u_group, proj, mul_group, d_row, tabs, chain):
    bt, f1p, n2, e = b_re.shape
    l = u_arr.shape[2]
    k1 = l // n2
    ec = min(e, FFT_EC)
    uv = u_arr.reshape(u_arr.shape[0], bt, k1, n2, e)
    pv = proj.reshape(proj.shape[0], bt, k1, n2, e)
    mat = tabs["m_c"]
    fblk = pl.BlockSpec((1, f1p, BF16_ROWS, ec), lambda bb, j, cc: (bb, 0, j, cc))
    full = lambda m: pl.BlockSpec(m.shape, lambda bb, j, cc: (0, 0))
    in_specs = [fblk, fblk, full(mat),
                pl.BlockSpec((1, 1, k1, BF16_ROWS, ec), lambda bb, j, cc: (u_group, bb, 0, j, cc)),
                pl.BlockSpec((1, 1, k1, BF16_ROWS, ec), lambda bb, j, cc: (mul_group, bb, 0, j, cc)),
                pl.BlockSpec((1, ec), lambda bb, j, cc: (0, cc))]
    args = [b_re, b_im, mat, uv, pv, d_row]
    out_specs = [pl.BlockSpec((1, k1, BF16_ROWS, ec), lambda bb, j, cc: (bb, 0, j, cc))]
    out_shape = [jax.ShapeDtypeStruct((bt, k1, n2, e), jnp.bfloat16)]
    if chain:
        in_specs.append(full(tabs["a_data"]))
        args.append(tabs["a_data"])
        out_specs += [fblk, fblk]
        out_shape += [jax.ShapeDtypeStruct((bt, f1p, n2, e), jnp.bfloat16)] * 2
    res = pl.pallas_call(
        functools.partial(_fft_c_kernel, chain=chain),
        grid=(bt, n2 // BF16_ROWS, e // ec),
        in_specs=in_specs,
        out_specs=out_specs,
        out_shape=out_shape,
        compiler_params=_params("parallel", "parallel", "parallel"),
        name="fft_stage_ca" if chain else "fft_stage_c",
    )(*args)
    return (res[0].reshape(bt, l, e),) + tuple(res[1:])


def _hyena_long(proj, f_re, f_im, inv_s, d_bias, tabs, mf, mi):
    order, e = d_bias.shape
    kf_re, kf_im = _fft_stage_b_filter(f_re, f_im, inv_s, mf, tabs)
    u_arr, u_group = proj, 0
    a_re, a_im = _fft_stage_a(u_arr, u_group, tabs)
    for o in range(order):
        b_re, b_im = _fft_stage_b_conv(a_re, a_im, kf_re, kf_im, o, mf, mi, tabs)
        chain = o + 1 < order
        res = _fft_stage_c(b_re, b_im, u_arr, u_group, proj, o + 1, d_bias[o].reshape(1, e), tabs, chain)
        if chain:
            a_re, a_im = res[1:]
        u_arr, u_group = res[0][None], 0
    return res[0]


@functools.lru_cache(maxsize=None)
def _dft_tables(l):
    n = 2 * l
    nf = l + 1
    nfp = _round_up(nf, BF16_ROWS)
    c, s = _angles(np.arange(nf)[:, None] * np.arange(n)[None, :], n)
    fwd = np.zeros((2 * nfp, n))
    fwd[:nf], fwd[nfp:nfp + nf] = c, -s
    wgt = np.full((nf,), 2.0)
    wgt[0] = wgt[-1] = 1.0
    inv = np.zeros((l, 2 * nfp))
    inv[:, :nf], inv[:, nfp:nfp + nf] = (c[:, :l] * wgt[:, None]).T, (-s[:, :l] * wgt[:, None]).T
    bf = lambda a: jnp.asarray(a, jnp.float32).astype(jnp.bfloat16)
    return nfp, bf(fwd), bf(inv)


def _hyena_short_kernel(p_ref, k_ref, s_ref, d_ref, fwd_ref, inv_ref, o_ref, *, l, nfp, order):
    fwd = fwd_ref[...]
    u = p_ref[0, 0]
    for o in range(order):
        kn = (k_ref[o] * s_ref[o, 0:1, :]).astype(jnp.bfloat16)
        kf = jnp.dot(fwd, kn, preferred_element_type=jnp.float32)
        uf = jnp.dot(fwd[:, :l], u.astype(jnp.bfloat16), preferred_element_type=jnp.float32)
        k_re, k_im, u_re, u_im = kf[:nfp], kf[nfp:], uf[:nfp], uf[nfp:]
        p = jnp.concatenate([u_re * k_re - u_im * k_im, u_re * k_im + u_im * k_re], axis=0)
        y = jnp.dot(inv_ref[...], p.astype(jnp.bfloat16), preferred_element_type=jnp.float32)
        u = p_ref[o + 1, 0] * (y + u * d_ref[o:o + 1, :])
    o_ref[0] = u


def _hyena_short(proj, k, inv_s, d_bias, e):
    nt, b, l, _ = proj.shape
    order = k.shape[0]
    nfp, fwd, inv = _dft_tables(l)
    return pl.pallas_call(
        functools.partial(_hyena_short_kernel, l=l, nfp=nfp, order=order),
        grid=(b,),
        in_specs=[pl.BlockSpec((nt, 1, l, e), lambda bb: (0, bb, 0, 0)),
                  pl.BlockSpec((order, 2 * l, e), lambda bb: (0, 0, 0)),
                  pl.BlockSpec((order, SUBLANES, e), lambda bb: (0, 0, 0)),
                  pl.BlockSpec((order, e), lambda bb: (0, 0)),
                  pl.BlockSpec(fwd.shape, lambda bb: (0, 0)),
                  pl.BlockSpec(inv.shape, lambda bb: (0, 0))],
        out_specs=pl.BlockSpec((1, l, e), lambda bb: (bb, 0, 0)),
        out_shape=jax.ShapeDtypeStruct((b, l, e), jnp.float32),
        compiler_params=_params("parallel"),
        name="hyena_short",
    )(proj, k, inv_s, d_bias, fwd, inv)


def kernel(x, c, ctx, c_ctx, w_mod, b_mod, ln_g, ln_b, rg_w_in, rg_conv_w, rg_conv_b, rg_w_r, rg_b_r, rg_w_i, rg_b_i, rg_lambda, rg_w_out, hy_w_in, hy_conv_w, hy_conv_b, hy_f_w1, hy_f_b1, hy_f_w2, hy_f_b2, hy_f_w3, hy_f_freq, hy_d, hy_w_out):
    bsz, seq, d = x.shape
    depth = w_mod.shape[0]
    e = rg_w_out.shape[1]
    alpha = (2 * depth) ** 0.25
    assert bsz + 1 <= SUBLANES

    cvec = jnp.zeros((SUBLANES, d), jnp.float32).at[:bsz].set(c).at[bsz].set(c_ctx)
    mod = _modulation(cvec, w_mod, b_mod)

    def mod_rows(layer, part, for_ctx):
        v = mod[layer, :, part * d:(part + 1) * d]
        v = jnp.broadcast_to(v[bsz], (bsz, d)) if for_ctx else v[:bsz]
        return v.reshape(bsz, 1, d)

    fft = None

    def filter_hidden(occ, l):
        zz, tcol = _filter_positions(l, hy_f_w1.shape[-1])
        hid = _filter_hidden(zz, hy_f_w1[occ], hy_f_b1[occ], hy_f_w2[occ], hy_f_b2[occ], hy_f_freq[occ])
        return hid, tcol

    for i in range(depth):
        kind = i % N_MIXERS
        occ = i // N_MIXERS
        need_ctx_out = any(j % N_MIXERS == 0 for j in range(i + 1, depth))
        col_major = occ % 2 == 1
        use_ctx = kind == 0 or need_ctx_out
        shift, scale, gate = (mod_rows(i, p, False) for p in range(3))
        if use_ctx:
            shift_c, scale_c, gate_c = (mod_rows(i, p, True) for p in range(3))

        if kind == 0:
            w_in = rg_w_in[occ].astype(jnp.bfloat16)
            w_out = rg_w_out[occ].astype(jnp.bfloat16)
            proj = _inproj(x, shift, scale, w_in, rg_conv_w[occ], rg_conv_b[occ],
                           e=e, pad_l=1, col_major=col_major)
            w_in_c = w_in if need_ctx_out else w_in[:, :e]
            proj_c = _inproj(ctx, shift_c, scale_c, w_in_c, rg_conv_w[occ], rg_conv_b[occ],
                             e=e, pad_l=1, col_major=False)
            zeros = jnp.zeros((bsz, 1, e), jnp.float32)
            gates = [(_gate_weights(rg_w_r[occ, dd], rg_w_i[occ, dd]), rg_b_r[occ, dd], rg_b_i[occ, dd],
                      rg_lambda[occ, dd]) for dd in range(2)]
            hc_f, st_f = _rg_scan(proj_c, zeros, *gates[0], reverse=False, write_seq=need_ctx_out)
            h_f, _ = _rg_scan(proj, st_f, *gates[0], reverse=False)
            y_c, st_b = _rg_scan(proj_c, zeros, *gates[1], reverse=True, prev=hc_f, write_seq=need_ctx_out)
            x_new, _ = _rg_scan(proj, st_b, *gates[1], reverse=True, prev=h_f,
                                out=(w_out, x, gate, ln_g[i], ln_b[i], alpha, col_major))
            g_group = 1
        else:
            w_in = hy_w_in[occ].astype(jnp.bfloat16)
            w_out = hy_w_out[occ].astype(jnp.bfloat16)
            proj = _inproj(x, shift, scale, w_in, hy_conv_w[occ], hy_conv_b[occ],
                           e=e, pad_l=1, col_major=col_major)
            if fft is None:
                tabs = _fft_tables(2 * seq // FFT_N2)
                fft = (tabs,) + tuple(_twiddled_dfts(tabs))
            f_re, f_im, inv_s = _filter_fft_a(*filter_hidden(occ, seq), hy_f_w3[occ], e, fft[0])
            y = _hyena_long(proj, f_re, f_im, inv_s, hy_d[occ], *fft)
            if need_ctx_out:
                proj_c = _inproj(ctx, shift_c, scale_c, w_in, hy_conv_w[occ], hy_conv_b[occ],
                                 e=e, pad_l=1, col_major=False)
                k_c, inv_s_c = _filters(*filter_hidden(occ, ctx.shape[1]), hy_f_w3[occ], e)
                y_c = _hyena_short(proj_c, k_c, inv_s_c, hy_d[occ], e)
            g_group = 3
            x_new = _outproj(y, proj, g_group, w_out, x, gate, ln_g[i], ln_b[i],
                             alpha=alpha, col_major=col_major)

        x = x_new
        if need_ctx_out:
            ctx = _outproj(y_c, proj_c, g_group, w_out, ctx, gate_c, ln_g[i], ln_b[i],
                           alpha=alpha, col_major=False)
    return x
```

```python
import functools
import math

import numpy as np
import jax
import jax.numpy as jnp
from jax import lax
from jax.experimental import pallas as pl
from jax.experimental.pallas import tpu as pltpu

GRID_W = 64
N_MIXERS = 2
RG_C = 8.0
RG_GATE_GROUP = 4
HY_EMB_BANDS = 8
HY_FAST_DECAY_PCT = 0.3
HY_SLOW_DECAY_PCT = 1.5
HY_DECAY_TARGET = 1e-2
HY_MOD_SHIFT = 0.05
LN_EPS = 1e-5

SUBLANES = 8
LANES = 128
BF16_ROWS = 16
FFT_N2 = 256
FFT_J = SUBLANES
FFT_EC = 768
STREAM_TILE = 512
SCAN_CHUNK = 512
LANE_CHUNK = 256
VMEM_LIMIT = 56 * 1024 * 1024

_HIGHEST = lax.Precision.HIGHEST


def _round_up(a, m):
    return (a + m - 1) // m * m


def _params(*sem):
    return pltpu.CompilerParams(dimension_semantics=sem, vmem_limit_bytes=VMEM_LIMIT)


def _sigmoid(v):
    return 1.0 / (1.0 + jnp.exp(-v))


def _mod_kernel(c_ref, w_ref, b_ref, o_ref):
    cv = c_ref[...]
    act = cv * _sigmoid(cv)
    o_ref[0] = jnp.dot(act, w_ref[0], preferred_element_type=jnp.float32,
                       precision=_HIGHEST) + b_ref[0]


def _modulation(cvec, w_mod, b_mod):
    depth, d, d3 = w_mod.shape
    nblk = d3 // d
    return pl.pallas_call(
        _mod_kernel,
        grid=(depth, nblk),
        in_specs=[pl.BlockSpec((SUBLANES, d), lambda l, n: (0, 0)),
                  pl.BlockSpec((1, d, d), lambda l, n: (l, 0, n)),
                  pl.BlockSpec((1, 1, d), lambda l, n: (l, 0, n))],
        out_specs=pl.BlockSpec((1, SUBLANES, d), lambda l, n: (l, 0, n)),
        out_shape=jax.ShapeDtypeStruct((depth, SUBLANES, d3), jnp.float32),
        compiler_params=_params("parallel", "parallel"),
        name="modulation",
    )(cvec, w_mod, b_mod.reshape(depth, 1, d3))


def _stream_view(x, col_major):
    b, l, d = x.shape
    if col_major:
        return x.reshape(b, l // GRID_W, GRID_W * d)
    return x


def _stream_specs(l, d, tm, col_major, nb_grid_prefix):
    def wrap(fn):
        return lambda *g: fn(*g[nb_grid_prefix:])
    if col_major:
        rows = l // GRID_W
        ncol = tm // rows
        assert tm == rows * ncol
        main = pl.BlockSpec((1, rows, ncol * d), wrap(lambda b, i: (b, 0, i)))
        prev = pl.BlockSpec((1, SUBLANES, d),
                            wrap(lambda b, i: (b, rows // SUBLANES - 1, jnp.maximum(i * ncol - 1, 0))))
        nxt = pl.BlockSpec((1, SUBLANES, d),
                           wrap(lambda b, i: (b, 0, jnp.minimum((i + 1) * ncol, GRID_W - 1))))
    else:
        per = tm // SUBLANES
        last = l // SUBLANES - 1
        main = pl.BlockSpec((1, tm, d), wrap(lambda b, i: (b, i, 0)))
        prev = pl.BlockSpec((1, SUBLANES, d), wrap(lambda b, i: (b, jnp.maximum(i * per - 1, 0), 0)))
        nxt = pl.BlockSpec((1, SUBLANES, d), wrap(lambda b, i: (b, jnp.minimum((i + 1) * per, last), 0)))
    return main, prev, nxt


def _tile_rows(l, col_major):
    tm = min(l, STREAM_TILE)
    if col_major:
        rows = l // GRID_W
        return max(tm // rows, 1) * rows
    return tm


def _tile_tokens(ref, d):
    blk = ref[0]
    ncol = blk.shape[1] // d
    if ncol == 1:
        return blk
    return jnp.concatenate([blk[:, j * d:(j + 1) * d] for j in range(ncol)], axis=0)


def _inproj_kernel(x_ref, xp_ref, xn_ref, sh_ref, sc_ref, w_ref, cw_ref, cb_ref, o_ref, u_scr,
                   *, n_conv, conv_k, pad_l, tm):
    i = pl.program_id(1)
    nt = pl.num_programs(1)
    d = w_ref.shape[0]
    e = o_ref.shape[3]
    scale = 1.0 + sc_ref[0]
    shift = sh_ref[0]
    mod = lambda v: v * scale + shift
    h_prev = jnp.where(i > 0, mod(xp_ref[0]), 0.0)
    h_next = jnp.where(i < nt - 1, mod(xn_ref[0]), 0.0)
    hh = jnp.concatenate([h_prev, mod(_tile_tokens(x_ref, d)), h_next], axis=0).astype(jnp.bfloat16)
    for n in range(o_ref.shape[0]):
        for c0 in range(0, e, LANE_CHUNK):
            sl = slice(c0, min(c0 + LANE_CHUNK, e))
            u = jnp.dot(hh, w_ref[:, n * e + sl.start:n * e + sl.stop], preferred_element_type=jnp.float32)
            if n >= n_conv:
                o_ref[n, 0, :, sl] = u[SUBLANES:SUBLANES + tm].astype(o_ref.dtype)
                continue
            for t0 in range(sl.start, sl.stop, LANES):
                slab = t0 // LANES
                u_scr[slab] = u[:, t0 - sl.start:t0 - sl.start + LANES]
                col = slice(n * e + t0, n * e + t0 + LANES)
                acc = cb_ref[:, col]
                for k in range(conv_k):
                    tap = u_scr[slab, pl.ds(SUBLANES - pad_l + k, tm, stride=1), :]
                    acc = acc + cw_ref[k:k + 1, col] * tap
                o_ref[n, 0, :, t0:t0 + LANES] = acc.astype(o_ref.dtype)


def _inproj(x, shift, scale, w_bf16, conv_w, conv_b, *, e, pad_l, col_major):
    b, l, d = x.shape
    nt_groups = w_bf16.shape[1] // e
    conv_k = conv_w.shape[0]
    n_conv = conv_w.shape[1] // e
    tm = _tile_rows(l, col_major)
    ntile = l // tm
    main, prev, nxt = _stream_specs(l, d, tm, col_major, 0)
    xv = _stream_view(x, col_major)
    kern = functools.partial(_inproj_kernel, n_conv=n_conv, conv_k=conv_k, pad_l=pad_l, tm=tm)
    const = lambda shape: pl.BlockSpec(shape, lambda bb, i: (0, 0), pipeline_mode=pl.Buffered(1))
    return pl.pallas_call(
        kern,
        grid=(b, ntile),
        in_specs=[main, prev, nxt,
                  pl.BlockSpec((1, 1, d), lambda bb, i: (bb, 0, 0)),
                  pl.BlockSpec((1, 1, d), lambda bb, i: (bb, 0, 0)),
                  const(w_bf16.shape), const(conv_w.shape), const((1, conv_w.shape[1]))],
        out_specs=pl.BlockSpec((nt_groups, 1, tm, e), lambda bb, i: (0, bb, i, 0)),
        out_shape=jax.ShapeDtypeStruct((nt_groups, b, l, e), jnp.bfloat16),
        scratch_shapes=[pltpu.VMEM((e // LANES, tm + 2 * SUBLANES, LANES), jnp.float32)],
        compiler_params=_params("parallel", "parallel"),
        name="inproj_conv",
    )(xv, xv, xv, shift, scale, w_bf16, conv_w, conv_b.reshape(1, -1))


def _rg_scan_kernel(*refs, reverse, add_prev, write_seq, out_alpha, tm, gw, ngroups):
    it = iter(refs)
    xc_ref = next(it)
    prev_ref = next(it) if add_prev else None
    h0_ref, wg_ref, br_ref, bi_ref, lam_ref = next(it), next(it), next(it), next(it), next(it)
    out_refs = [next(it) for _ in range(6)] if out_alpha is not None else None
    o_ref = next(it) if write_seq else None
    st_ref = next(it)
    a_scr, b_scr, carry = next(it), next(it), next(it)
    seq_ref = next(it) if out_alpha is not None else o_ref

    i = pl.program_id(1)
    nchunk = pl.num_programs(1)

    @pl.when(i == 0)
    def _():
        carry[...] = h0_ref[0]

    lam = lam_ref[...]
    neg = -lam
    softplus = jnp.maximum(neg, 0.0) + jnp.log(1.0 + jnp.exp(-jnp.abs(neg)))
    rate = (0.5 * RG_C) * softplus
    for g in range(ngroups):
        sl = slice(g * gw, (g + 1) * gw)
        xg = xc_ref[0, 0, :, sl]
        pre = jnp.dot(xg, wg_ref[g], preferred_element_type=jnp.float32)
        t_r = jnp.tanh(pre[:, :gw] + br_ref[:, sl])
        t_i = jnp.tanh(pre[:, gw:] + bi_ref[:, sl])
        neg_log_a = rate[:, sl] * t_r + rate[:, sl]
        a = jnp.exp(-neg_log_a)
        a_scr[:, sl] = a
        s = jnp.tanh(neg_log_a) * (a * a + 1.0)
        mult = jnp.where(s > 0.0, s * lax.rsqrt(s), 0.0)
        b_scr[:, sl] = (0.5 * mult) * ((t_i + 1.0) * xg.astype(jnp.float32))

    e = a_scr.shape[1]
    row = lax.broadcasted_iota(jnp.int32, (SUBLANES, e), 0)
    nblk = tm // BF16_ROWS

    def scan8(r0, c):
        a = a_scr[pl.ds(r0, SUBLANES), :]
        h = b_scr[pl.ds(r0, SUBLANES), :]
        for dist in (1, 2, 4):
            shift = (SUBLANES - dist) if reverse else dist
            valid = (row + dist < SUBLANES) if reverse else (row >= dist)
            a_sh = jnp.where(valid, pltpu.roll(a, shift, 0), 1.0)
            h_sh = jnp.where(valid, pltpu.roll(h, shift, 0), 0.0)
            h = a * h_sh + h
            a = a * a_sh
        h = a * c + h
        return h, (h[0:1, :] if reverse else h[SUBLANES - 1:SUBLANES, :])

    def block(k, c):
        kk = (nblk - 1 - k) if reverse else k
        r0 = pl.multiple_of(kk * BF16_ROWS, BF16_ROWS)
        halves = [None, None]
        for hh in ((1, 0) if reverse else (0, 1)):
            halves[hh], c = scan8(r0 + hh * SUBLANES, c)
        if write_seq:
            h = jnp.concatenate(halves, axis=0)
            if add_prev:
                h = h + prev_ref[0, pl.ds(r0, BF16_ROWS), :].astype(jnp.float32)
            if out_alpha is None:
                seq_ref[0, pl.ds(r0, BF16_ROWS), :] = h.astype(seq_ref.dtype)
            else:
                seq_ref[pl.ds(r0, BF16_ROWS), :] = h.astype(seq_ref.dtype)
        return c

    c_fin = lax.fori_loop(0, nblk, block, carry[...], unroll=2)
    carry[...] = c_fin

    @pl.when(i == nchunk - 1)
    def _():
        st_ref[0] = c_fin

    if out_alpha is not None:
        g_ref, w_ref, x_ref, gate_ref, lg_ref, lb_ref = out_refs
        _outproj_tile(seq_ref[...], g_ref[0, 0], w_ref, x_ref, gate_ref, lg_ref, lb_ref, o_ref, out_alpha)


def _rg_scan(proj, h0, wg, b_r, b_i, lam, *, reverse, prev=None, write_seq=True, out=None):
    _, b, l, e = proj.shape
    tm = min(l, SCAN_CHUNK)
    nchunk = l // tm
    ngroups, gw, _ = wg.shape
    pos = (lambda i: nchunk - 1 - i) if reverse else (lambda i: i)
    add_prev = prev is not None
    in_specs = [pl.BlockSpec((1, 1, tm, e), lambda bb, i: (0, bb, pos(i), 0))]
    args = [proj]
    if add_prev:
        in_specs.append(pl.BlockSpec((1, tm, e), lambda bb, i: (bb, pos(i), 0)))
        args.append(prev)
    in_specs += [pl.BlockSpec((1, 1, e), lambda bb, i: (bb, 0, 0)),
                 pl.BlockSpec((ngroups, gw, 2 * gw), lambda bb, i: (0, 0, 0)),
                 pl.BlockSpec((1, e), lambda bb, i: (0, 0)),
                 pl.BlockSpec((1, e), lambda bb, i: (0, 0)),
                 pl.BlockSpec((1, e), lambda bb, i: (0, 0))]
    args += [h0, wg, 0.5 * b_r.reshape(1, e), 0.5 * b_i.reshape(1, e), lam.reshape(1, e)]
    out_specs, out_shape = [], []
    scratch = [pltpu.VMEM((tm, e), jnp.float32), pltpu.VMEM((tm, e), jnp.float32),
               pltpu.VMEM((1, e), jnp.float32)]
    out_alpha = None
    if out is not None:
        w_out, x, gate, ln_g, ln_b, out_alpha, col_major = out
        d = x.shape[-1]
        assert write_seq and tm == _tile_rows(l, col_major)
        main, _, _ = _stream_specs(l, d, tm, col_major, 0)
        tile = pl.BlockSpec(main.block_shape, lambda bb, i: main.index_map(bb, pos(i)))
        xv = _stream_view(x, col_major)
        in_specs += [pl.BlockSpec((1, 1, tm, e), lambda bb, i: (1, bb, pos(i), 0)),
                     pl.BlockSpec((e, d), lambda bb, i: (0, 0)),
                     tile,
                     pl.BlockSpec((1, 1, d), lambda bb, i: (bb, 0, 0)),
                     pl.BlockSpec((1, d), lambda bb, i: (0, 0)),
                     pl.BlockSpec((1, d), lambda bb, i: (0, 0))]
        args += [proj, w_out, xv, gate, ln_g.reshape(1, d), ln_b.reshape(1, d)]
        out_specs.append(tile)
        out_shape.append(jax.ShapeDtypeStruct(xv.shape, jnp.float32))
        scratch.append(pltpu.VMEM((tm, e), jnp.bfloat16))
    elif write_seq:
        out_specs.append(pl.BlockSpec((1, tm, e), lambda bb, i: (bb, pos(i), 0)))
        out_shape.append(jax.ShapeDtypeStruct((b, l, e), jnp.bfloat16))
    out_specs.append(pl.BlockSpec((1, 1, e), lambda bb, i: (bb, 0, 0)))
    out_shape.append(jax.ShapeDtypeStruct((b, 1, e), jnp.float32))
    kern = functools.partial(_rg_scan_kernel, reverse=reverse, add_prev=add_prev, write_seq=write_seq,
                             out_alpha=out_alpha, tm=tm, gw=gw, ngroups=ngroups)
    res = pl.pallas_call(
        kern,
        grid=(b, nchunk),
        in_specs=in_specs,
        out_specs=out_specs,
        out_shape=out_shape,
        scratch_shapes=scratch,
        compiler_params=_params("parallel", "arbitrary"),
        name="rg_scan_bwd" if reverse else "rg_scan_fwd",
    )(*args)
    if out is not None:
        return res[0].reshape(x.shape), res[1]
    if write_seq:
        return res[0], res[1]
    return None, res[0]


def _gate_weights(w_r, w_i):
    nb, bw, _ = w_r.shape
    g = RG_GATE_GROUP
    ng = nb // g
    eye = jnp.eye(g, dtype=w_r.dtype)

    def bd(w):
        w = w.reshape(ng, g, bw, bw)
        return jnp.einsum("nhij,hk->nhikj", w, eye).reshape(ng, g * bw, g * bw)

    return (0.5 * jnp.concatenate([bd(w_r), bd(w_i)], axis=-1)).astype(jnp.bfloat16)


def _outproj_tile(y, g, w_ref, x_ref, gate_ref, lg_ref, lb_ref, o_ref, alpha):
    d = w_ref.shape[1]
    hg = 0.5 * g.astype(jnp.float32)
    z = (y.astype(jnp.float32) * (hg * (jnp.tanh(hg) + 1.0))).astype(jnp.bfloat16)
    out = jnp.dot(z, w_ref[...], preferred_element_type=jnp.float32)
    r = alpha * _tile_tokens(x_ref, d) + gate_ref[0] * out
    mu = jnp.mean(r, axis=-1, keepdims=True)
    cen = r - mu
    var = jnp.mean(cen * cen, axis=-1, keepdims=True)
    res = cen * lax.rsqrt(var + LN_EPS) * lg_ref[...] + lb_ref[...]
    rows = o_ref.shape[1]
    for j in range(o_ref.shape[2] // d):
        o_ref[0, :, j * d:(j + 1) * d] = res[j * rows:(j + 1) * rows]


def _outproj_kernel(y_ref, g_ref, w_ref, x_ref, gate_ref, lg_ref, lb_ref, o_ref, *, alpha):
    _outproj_tile(y_ref[0], g_ref[0, 0], w_ref, x_ref, gate_ref, lg_ref, lb_ref, o_ref, alpha)


def _outproj(y, proj, g_group, w_bf16, x, gate, ln_g, ln_b, *, alpha, col_major):
    b, l, e = y.shape
    d = x.shape[-1]
    tm = _tile_rows(l, col_major)
    ntile = l // tm
    main, _, _ = _stream_specs(l, d, tm, col_major, 0)
    xv = _stream_view(x, col_major)
    out = pl.pallas_call(
        functools.partial(_outproj_kernel, alpha=alpha),
        grid=(b, ntile),
        in_specs=[pl.BlockSpec((1, tm, e), lambda bb, i: (bb, i, 0)),
                  pl.BlockSpec((1, 1, tm, e), lambda bb, i: (g_group, bb, i, 0)),
                  pl.BlockSpec((e, d), lambda bb, i: (0, 0)),
                  main,
                  pl.BlockSpec((1, 1, d), lambda bb, i: (bb, 0, 0)),
                  pl.BlockSpec((1, d), lambda bb, i: (0, 0)),
                  pl.BlockSpec((1, d), lambda bb, i: (0, 0))],
        out_specs=main,
        out_shape=jax.ShapeDtypeStruct(xv.shape, jnp.float32),
        compiler_params=_params("parallel", "parallel"),
        name="outproj_ln",
    )(y, proj, w_bf16, xv, gate, ln_g.reshape(1, d), ln_b.reshape(1, d))
    return out.reshape(x.shape)


def _filter_positions(l, fh):
    f32 = jnp.float32
    t = jnp.linspace(0.0, 1.0, l, dtype=f32)[:, None]
    bands = jnp.linspace(1e-4, HY_EMB_BANDS - 1, HY_EMB_BANDS, dtype=f32)
    w = (2.0 * math.pi) * jnp.arange(l, dtype=f32)[:, None] / l
    z = jnp.concatenate([t, jnp.cos(bands * w), -jnp.sin(bands * w)], axis=-1)
    z_rev = jnp.concatenate([z[:1], z[:0:-1]], axis=0)
    pad = lambda a: jnp.pad(a, ((0, 0), (0, fh - a.shape[1])))
    return jnp.concatenate([pad(z), pad(z_rev)], axis=1), jnp.stack([z[:, 0:1], z_rev[:, 0:1]])


def _hidden_kernel(z_ref, w1_ref, b1_ref, w2_ref, b2_ref, fr_ref, o_ref):
    fr = fr_ref[...]
    h = jnp.dot(z_ref[...], w1_ref[...], preferred_element_type=jnp.float32, precision=_HIGHEST)
    h = jnp.sin(fr * (h + b1_ref[...]))
    h = jnp.dot(h, w2_ref[...], preferred_element_type=jnp.float32, precision=_HIGHEST)
    o_ref[...] = jnp.sin(fr * (h + b2_ref[...]))


def _filter_hidden(zz, w1, b1, w2, b2, freq):
    n, zp = zz.shape
    fh = w1.shape[1]
    tr = min(n, 1024)
    both = lambda v: jnp.concatenate([v, v]).reshape(1, 2 * fh)
    blockdiag = lambda w: jnp.kron(jnp.eye(2, dtype=w.dtype), w)
    w1p = blockdiag(jnp.pad(w1, ((0, fh - w1.shape[0]), (0, 0))))
    full = lambda shape: pl.BlockSpec(shape, lambda i: (0,) * len(shape))
    return pl.pallas_call(
        _hidden_kernel,
        grid=(n // tr,),
        in_specs=[pl.BlockSpec((tr, zp), lambda i: (i, 0)), full((zp, zp)), full((1, zp)),
                  full((zp, zp)), full((1, zp)), full((1, zp))],
        out_specs=pl.BlockSpec((tr, zp), lambda i: (i, 0)),
        out_shape=jax.ShapeDtypeStruct((n, zp), jnp.float32),
        compiler_params=_params("parallel"),
        name="filter_hidden",
    )(zz, w1p, both(b1), blockdiag(w2), both(b2), both(freq))


def _split_bf16(a):
    hi = a.astype(jnp.bfloat16)
    return hi, (a - hi.astype(jnp.float32)).astype(jnp.bfloat16)


def _filter_kernel(hid_ref, t_ref, w3_ref, ad_ref, k_ref, s_ref, *, tr, n_total):
    d = pl.program_id(1)
    i = pl.program_id(2)
    last = (d == pl.num_programs(1) - 1) & (i == pl.num_programs(2) - 1)
    hi, lo = _split_bf16(hid_ref[...])
    h = jnp.dot(jnp.concatenate([hi, lo, hi], axis=1), w3_ref[0, 0], preferred_element_type=jnp.float32)
    window = jnp.exp(-t_ref[0] * ad_ref[...]) + HY_MOD_SHIFT
    row = lax.broadcasted_iota(jnp.int32, h.shape, 0)
    gap = (row == 0) & (d == 1) & (i == 0)
    k = jnp.where(gap, 0.0, h * window)
    k_ref[0] = k

    @pl.when((d == 0) & (i == 0))
    def _():
        s_ref[...] = jnp.zeros_like(s_ref)

    s_ref[0] += jnp.sum(jnp.abs(k).reshape(tr // SUBLANES, SUBLANES, -1), axis=0)

    @pl.when(last)
    def _():
        tot = jnp.sum(s_ref[0], axis=0, keepdims=True)
        s_ref[0] = jnp.broadcast_to(1.0 / (tot * n_total), s_ref.shape[1:])


def _filter_weights(w3, fh, e, compact):
    order = w3.shape[1] // (2 * e)
    max_decay = math.log(HY_DECAY_TARGET) / HY_FAST_DECAY_PCT
    min_decay = math.log(HY_DECAY_TARGET) / HY_SLOW_DECAY_PCT
    absdelta = jnp.abs(jnp.linspace(min_decay, max_decay, e, dtype=jnp.float32)).reshape(1, e)
    w3 = w3.reshape(fh, order, 2, e).transpose(1, 2, 0, 3)
    if compact:
        w_hi, w_lo = _split_bf16(w3)
        return jnp.concatenate([w_hi, w_hi, w_lo, jnp.zeros_like(w_hi)], axis=2), absdelta
    w3 = jnp.stack([jnp.pad(w3[:, 0], ((0, 0), (0, fh), (0, 0))),
                    jnp.pad(w3[:, 1], ((0, 0), (fh, 0), (0, 0)))], axis=1)
    w_hi, w_lo = _split_bf16(w3)
    return jnp.concatenate([w_hi, w_hi, w_lo], axis=2), absdelta


def _filter_fft_a_kernel(hid_ref, t_ref, w3_ref, ad_ref, m_ref, re_ref, im_ref, s_ref, *, n_total):
    j = pl.program_id(2)
    k1h, rows, fh2 = hid_ref.shape
    ec = ad_ref.shape[1]
    hi, lo = _split_bf16(hid_ref[...].reshape(k1h * rows, fh2))
    fh = fh2 // 2
    halves = []
    for d in range(2):
        own = slice(d * fh, (d + 1) * fh)
        stacked = jnp.concatenate([hi[:, own], lo[:, own], hi[:, own], jnp.zeros_like(hi[:, own])], axis=1)
        h = jnp.dot(stacked, w3_ref[0, d], preferred_element_type=jnp.float32)
        window = jnp.exp(-t_ref[d].reshape(k1h * rows, 1) * ad_ref[...]) + HY_MOD_SHIFT
        k = h * window
        if d == 1:
            row = lax.broadcasted_iota(jnp.int32, k.shape, 0)
            k = jnp.where((row == 0) & (j == 0), 0.0, k)
        halves.append(k)

    @pl.when(j == 0)
    def _():
        s_ref[...] = jnp.zeros_like(s_ref)

    k = jnp.concatenate(halves, axis=0)
    s_ref[0] += jnp.sum(jnp.abs(k).reshape(-1, SUBLANES, ec), axis=0)

    @pl.when(j == pl.num_programs(2) - 1)
    def _():
        tot = jnp.sum(s_ref[0], axis=0, keepdims=True)
        s_ref[0] = jnp.broadcast_to(1.0 / (tot * n_total), s_ref.shape[1:])

    k3 = k.reshape(2 * k1h, rows, ec)
    half = m_ref.shape[0] // 2
    f1p = half // FFT_J
    re_parts, im_parts = [], []
    for h in range(rows // FFT_J):
        x = k3[:, h * FFT_J:(h + 1) * FFT_J, :].reshape(2 * k1h * FFT_J, ec)
        r = jnp.dot(m_ref[...], x.astype(jnp.bfloat16), preferred_element_type=jnp.float32)
        re_parts.append(r[:half].reshape(f1p, FFT_J, ec))
        im_parts.append(r[half:].reshape(f1p, FFT_J, ec))
    re_ref[0] = jnp.concatenate(re_parts, axis=1).astype(jnp.bfloat16)
    im_ref[0] = jnp.concatenate(im_parts, axis=1).astype(jnp.bfloat16)


def _filter_fft_a(hid, tcol, w3, e, tabs):
    l, fh2 = hid.shape
    n2 = FFT_N2
    k1h = l // n2
    f1p = tabs["f1p"]
    mat = tabs["a_full"]
    w_stack, absdelta = _filter_weights(w3, fh2 // 2, e, compact=True)
    order = w_stack.shape[0]
    ec = min(e, FFT_EC)
    out_spec = pl.BlockSpec((1, f1p, BF16_ROWS, ec), lambda o, cc, j: (o, 0, j, cc))
    shape = jax.ShapeDtypeStruct((order, f1p, n2, e), jnp.bfloat16)
    return pl.pallas_call(
        functools.partial(_filter_fft_a_kernel, n_total=float(2 * l)),
        grid=(order, e // ec, n2 // BF16_ROWS),
        in_specs=[pl.BlockSpec((k1h, BF16_ROWS, fh2), lambda o, cc, j: (0, j, 0)),
                  pl.BlockSpec((2, k1h, BF16_ROWS, 1), lambda o, cc, j: (0, 0, j, 0)),
                  pl.BlockSpec((1, 2, w_stack.shape[2], ec), lambda o, cc, j: (o, 0, 0, cc)),
                  pl.BlockSpec((1, ec), lambda o, cc, j: (0, cc)),
                  pl.BlockSpec(mat.shape, lambda o, cc, j: (0, 0))],
        out_specs=[out_spec, out_spec,
                   pl.BlockSpec((1, SUBLANES, ec), lambda o, cc, j: (o, 0, cc))],
        out_shape=[shape, shape, jax.ShapeDtypeStruct((order, SUBLANES, e), jnp.float32)],
        compiler_params=_params("parallel", "parallel", "arbitrary"),
        name="filter_fft_a",
    )(hid.reshape(k1h, n2, fh2), tcol.reshape(2, k1h, n2, 1), w_stack, absdelta, mat)


def _filters(hid, tcol, w3, e):
    l, fh2 = hid.shape
    n = 2 * l
    tr = min(l, 512)
    nl = l // tr
    w_stack, absdelta = _filter_weights(w3, fh2 // 2, e, compact=False)
    order = w_stack.shape[0]
    return pl.pallas_call(
        functools.partial(_filter_kernel, tr=tr, n_total=float(n)),
        grid=(order, 2, nl),
        in_specs=[pl.BlockSpec((tr, fh2), lambda o, d, i: (i, 0)),
                  pl.BlockSpec((1, tr, 1), lambda o, d, i: (d, i, 0)),
                  pl.BlockSpec((1, 1, 3 * fh2, e), lambda o, d, i: (o, d, 0, 0)),
                  pl.BlockSpec((1, e), lambda o, d, i: (0, 0))],
        out_specs=[pl.BlockSpec((1, tr, e), lambda o, d, i: (o, d * nl + i, 0)),
                   pl.BlockSpec((1, SUBLANES, e), lambda o, d, i: (o, 0, 0))],
        out_shape=[jax.ShapeDtypeStruct((order, n, e), jnp.float32),
                   jax.ShapeDtypeStruct((order, SUBLANES, e), jnp.float32)],
        compiler_params=_params("parallel", "arbitrary", "arbitrary"),
        name="filter_gen",
    )(hid, tcol, w_stack, absdelta)


def _angles(num, den):
    ang = 2.0 * np.pi * (np.asarray(num, np.int64) % den).astype(np.float64) / den
    return np.cos(ang), np.sin(ang)


@functools.lru_cache(maxsize=None)
def _fft_tables(n1):
    n2 = FFT_N2
    n = n1 * n2
    f1 = n1 // 2 + 1
    f1p = _round_up(f1, 2)
    fr = np.arange(f1)[:, None]
    eye = np.eye(FFT_J)

    def stage_a(k1):
        c, s = _angles(fr * np.arange(k1)[None, :], n1)
        m = np.zeros((2 * f1p, k1))
        m[:f1], m[f1p:f1p + f1] = c, -s
        return np.kron(m, eye)

    nyq = n1 // 2
    wgt = np.full((nyq,), 2.0)
    wgt[0] = 1.0
    c, s = _angles(np.arange(n1 // 2)[:, None] * np.arange(nyq)[None, :], n1)
    m_c = np.concatenate([c * wgt, -s * wgt], axis=1)

    c2, s2 = _angles(np.arange(n2)[:, None] * np.arange(n2)[None, :], n2)
    cphi, sphi = _angles(fr * np.arange(n2)[None, :], n)
    phase = np.stack([cphi, sphi], axis=1)
    bf = lambda a: jnp.asarray(a, jnp.float32).astype(jnp.bfloat16)
    f32 = lambda a: jnp.asarray(a, jnp.float32)
    return dict(f1=f1, f1p=f1p, a_data=bf(stage_a(n1 // 2)), a_full=bf(stage_a(n1)),
                m_c=bf(np.kron(m_c, eye)), base_c=f32(c2), base_s=f32(s2),
                phase_row=f32(phase[:, :, None, :]), phase_col=f32(phase[..., None]))


def _twiddled_dft_kernel(c_ref, s_ref, row_ref, col_ref, mf_ref, mi_ref):
    c, s = c_ref[...], s_ref[...]
    n2 = c.shape[0]
    bf = jnp.bfloat16
    cr, sr = row_ref[0, 0], row_ref[0, 1]
    cf, sf = (c * cr - s * sr).astype(bf), (s * cr + c * sr).astype(bf)
    mf_ref[0, :n2, :n2] = cf
    mf_ref[0, :n2, n2:] = sf
    mf_ref[0, n2:, :n2] = -sf
    mf_ref[0, n2:, n2:] = cf
    cc, sc = col_ref[0, 0], col_ref[0, 1]
    ci, si = (c * cc - s * sc).astype(bf), (s * cc + c * sc).astype(bf)
    mi_ref[0, :n2, :n2] = ci
    mi_ref[0, :n2, n2:] = -si
    mi_ref[0, n2:, :n2] = si
    mi_ref[0, n2:, n2:] = ci


def _twiddled_dfts(tabs):
    f1 = tabs["f1"]
    n2 = FFT_N2
    shape = jax.ShapeDtypeStruct((f1, 2 * n2, 2 * n2), jnp.bfloat16)
    base = pl.BlockSpec((n2, n2), lambda f: (0, 0))
    out = pl.BlockSpec((1, 2 * n2, 2 * n2), lambda f: (f, 0, 0))
    return pl.pallas_call(
        _twiddled_dft_kernel,
        grid=(f1,),
        in_specs=[base, base, pl.BlockSpec((1, 2, 1, n2), lambda f: (f, 0, 0, 0)),
                  pl.BlockSpec((1, 2, n2, 1), lambda f: (f, 0, 0, 0))],
        out_specs=[out, out],
        out_shape=[shape, shape],
        compiler_params=_params("parallel"),
        name="twiddled_dft_tables",
    )(tabs["base_c"], tabs["base_s"], tabs["phase_row"], tabs["phase_col"])


def _fft_a_kernel(x_ref, m_ref, re_ref, im_ref):
    k1, ec = x_ref.shape[2], x_ref.shape[4]
    half = m_ref.shape[0] // 2
    f1p = half // FFT_J
    re_parts, im_parts = [], []
    xf = x_ref[0, 0].astype(jnp.float32)
    for h in range(x_ref.shape[3] // FFT_J):
        x = xf[:, h * FFT_J:(h + 1) * FFT_J, :].reshape(k1 * FFT_J, ec)
        r = jnp.dot(m_ref[...], x.astype(jnp.bfloat16), preferred_element_type=jnp.float32)
        re_parts.append(r[:half].reshape(f1p, FFT_J, ec))
        im_parts.append(r[half:].reshape(f1p, FFT_J, ec))
    re_ref[0] = jnp.concatenate(re_parts, axis=1).astype(jnp.bfloat16)
    im_ref[0] = jnp.concatenate(im_parts, axis=1).astype(jnp.bfloat16)


def _fft_stage_a(arr, group, tabs):
    _, bt, rows, e = arr.shape
    k1 = rows // FFT_N2
    f1p = tabs["f1p"]
    mat = tabs["a_data"]
    assert mat.shape[1] == k1 * FFT_J
    ec = min(e, FFT_EC)
    av = arr.reshape(arr.shape[0], bt, k1, FFT_N2, e)
    out_spec = pl.BlockSpec((1, f1p, BF16_ROWS, ec), lambda bb, j, cc: (bb, 0, j, cc))
    shape = jax.ShapeDtypeStruct((bt, f1p, FFT_N2, e), jnp.bfloat16)
    return pl.pallas_call(
        _fft_a_kernel,
        grid=(bt, FFT_N2 // BF16_ROWS, e // ec),
        in_specs=[pl.BlockSpec((1, 1, k1, BF16_ROWS, ec), lambda bb, j, cc: (group, bb, 0, j, cc)),
                  pl.BlockSpec(mat.shape, lambda bb, j, cc: (0, 0))],
        out_specs=[out_spec, out_spec],
        out_shape=[shape, shape],
        compiler_params=_params("parallel", "parallel", "parallel"),
        name="fft_stage_a",
    )(av, mat)


def _lane_chunks(e, width=256):
    return [slice(c0, min(c0 + width, e)) for c0 in range(0, e, width)]


def _fft_b_filter_kernel(re_ref, im_ref, sc_ref, mf_ref, kre_ref, kim_ref):
    for b in range(re_ref.shape[0]):
        for sl in _lane_chunks(re_ref.shape[3]):
            x = jnp.concatenate([re_ref[b, 0, :, sl], im_ref[b, 0, :, sl]], axis=0)
            u = jnp.dot(mf_ref[0], x, preferred_element_type=jnp.float32) * sc_ref[b, 0:1, sl]
            kre_ref[b, 0, :, sl] = u[:FFT_N2].astype(kre_ref.dtype)
            kim_ref[b, 0, :, sl] = u[FFT_N2:].astype(kim_ref.dtype)


def _fft_b_conv_kernel(re_ref, im_ref, kre_ref, kim_ref, mf_ref, mi_ref, ore_ref, oim_ref):
    for sl in _lane_chunks(re_ref.shape[3]):
        k_re = kre_ref[0, 0, :, sl].astype(jnp.float32)
        k_im = kim_ref[0, 0, :, sl].astype(jnp.float32)
        for b in range(re_ref.shape[0]):
            x = jnp.concatenate([re_ref[b, 0, :, sl], im_ref[b, 0, :, sl]], axis=0)
            u = jnp.dot(mf_ref[0], x, preferred_element_type=jnp.float32)
            u_re, u_im = u[:FFT_N2], u[FFT_N2:]
            p = jnp.concatenate([u_re * k_re - u_im * k_im, u_re * k_im + u_im * k_re], axis=0)
            v = jnp.dot(mi_ref[0], p.astype(jnp.bfloat16), preferred_element_type=jnp.float32)
            ore_ref[b, 0, :, sl] = v[:FFT_N2].astype(jnp.bfloat16)
            oim_ref[b, 0, :, sl] = v[FFT_N2:].astype(jnp.bfloat16)


def _fft_stage_b_filter(a_re, a_im, inv_s, mf, tabs):
    bt, _, n2, e = a_re.shape
    f1 = tabs["f1"]
    blk = pl.BlockSpec((bt, 1, n2, e), lambda f: (0, f, 0, 0))
    shape = jax.ShapeDtypeStruct((bt, f1, n2, e), jnp.bfloat16)
    return pl.pallas_call(
        _fft_b_filter_kernel,
        grid=(f1,),
        in_specs=[blk, blk, pl.BlockSpec(inv_s.shape, lambda f: (0, 0, 0)),
                  pl.BlockSpec((1, 2 * n2, 2 * n2), lambda f: (f, 0, 0))],
        out_specs=[blk, blk],
        out_shape=[shape, shape],
        compiler_params=_params("parallel"),
        name="fft_stage_b_filter",
    )(a_re, a_im, inv_s, mf)


def _fft_stage_b_conv(a_re, a_im, kf_re, kf_im, order_idx, mf, mi, tabs):
    bt, f1p, n2, e = a_re.shape
    f1 = tabs["f1"]
    fc = lambda f: jnp.minimum(f, f1 - 1)
    blk = pl.BlockSpec((bt, 1, n2, e), lambda f: (0, f, 0, 0))
    kblk = pl.BlockSpec((1, 1, n2, e), lambda f: (order_idx, fc(f), 0, 0))
    wblk = pl.BlockSpec((1, 2 * n2, 2 * n2), lambda f: (fc(f), 0, 0))
    shape = jax.ShapeDtypeStruct((bt, f1p, n2, e), jnp.bfloat16)
    return pl.pallas_call(
        _fft_b_conv_kernel,
        grid=(f1p,),
        in_specs=[blk, blk, kblk, kblk, wblk, wblk],
        out_specs=[blk, blk],
        out_shape=[shape, shape],
        compiler_params=_params("parallel"),
        name="fft_stage_b_conv",
    )(a_re, a_im, kf_re, kf_im, mf, mi)


def _fft_c_kernel(*refs, chain):
    re_ref, im_ref, m_ref, u_ref, mul_ref, d_ref = refs[:6]
    if chain:
        ma_ref, o_ref, are_ref, aim_ref = refs[6:]
    else:
        o_ref, = refs[6:]
    k1, ec = o_ref.shape[1], o_ref.shape[3]
    f1p = re_ref.shape[1]
    nyq = m_ref.shape[1] // (2 * FFT_J)
    re = re_ref[0].astype(jnp.float32)
    im = im_ref[0].astype(jnp.float32)
    uf = u_ref[0, 0].astype(jnp.float32)
    mf = mul_ref[0, 0].astype(jnp.float32)
    parts, re_parts, im_parts = [], [], []
    for h in range(o_ref.shape[2] // FFT_J):
        sl = slice(h * FFT_J, (h + 1) * FFT_J)
        x = jnp.concatenate([re[:nyq, sl, :].reshape(nyq * FFT_J, ec),
                             im[:nyq, sl, :].reshape(nyq * FFT_J, ec)], axis=0)
        y = jnp.dot(m_ref[...], x.astype(jnp.bfloat16), preferred_element_type=jnp.float32)
        r_nyq = re[nyq, sl, :]
        y = (y.reshape(k1 // 2, 2, FFT_J, ec) + jnp.stack([r_nyq, -r_nyq])[None]).reshape(k1 * FFT_J, ec)
        u = uf[:, sl, :].reshape(k1 * FFT_J, ec)
        mul = mf[:, sl, :].reshape(k1 * FFT_J, ec)
        z = (mul * (y + u * d_ref[...])).astype(jnp.bfloat16)
        parts.append(z.astype(jnp.float32).reshape(k1, FFT_J, ec))
        if chain:
            r = jnp.dot(ma_ref[...], z, preferred_element_type=jnp.float32)
            half = ma_ref.shape[0] // 2
            re_parts.append(r[:half].reshape(f1p, FFT_J, ec))
            im_parts.append(r[half:].reshape(f1p, FFT_J, ec))
    o_ref[0] = jnp.concatenate(parts, axis=1).astype(o_ref.dtype)
    if chain:
        are_ref[0] = jnp.concatenate(re_parts, axis=1).astype(jnp.bfloat16)
        aim_ref[0] = jnp.concatenate(im_parts, axis=1).astype(jnp.bfloat16)


def _fft_stage_c(b_re, b_im, u_arr, u_group, proj, mul_group, d_row, tabs, chain):
    bt, f1p, n2, e = b_re.shape
    l = u_arr.shape[2]
    k1 = l // n2
    ec = min(e, FFT_EC)
    uv = u_arr.reshape(u_arr.shape[0], bt, k1, n2, e)
    pv = proj.reshape(proj.shape[0], bt, k1, n2, e)
    mat = tabs["m_c"]
    fblk = pl.BlockSpec((1, f1p, BF16_ROWS, ec), lambda bb, j, cc: (bb, 0, j, cc))
    full = lambda m: pl.BlockSpec(m.shape, lambda bb, j, cc: (0, 0))
    in_specs = [fblk, fblk, full(mat),
                pl.BlockSpec((1, 1, k1, BF16_ROWS, ec), lambda bb, j, cc: (u_group, bb, 0, j, cc)),
                pl.BlockSpec((1, 1, k1, BF16_ROWS, ec), lambda bb, j, cc: (mul_group, bb, 0, j, cc)),
                pl.BlockSpec((1, ec), lambda bb, j, cc: (0, cc))]
    args = [b_re, b_im, mat, uv, pv, d_row]
    out_specs = [pl.BlockSpec((1, k1, BF16_ROWS, ec), lambda bb, j, cc: (bb, 0, j, cc))]
    out_shape = [jax.ShapeDtypeStruct((bt, k1, n2, e), jnp.bfloat16)]
    if chain:
        in_specs.append(full(tabs["a_data"]))
        args.append(tabs["a_data"])
        out_specs += [fblk, fblk]
        out_shape += [jax.ShapeDtypeStruct((bt, f1p, n2, e), jnp.bfloat16)] * 2
    res = pl.pallas_call(
        functools.partial(_fft_c_kernel, chain=chain),
        grid=(bt, n2 // BF16_ROWS, e // ec),
        in_specs=in_specs,
        out_specs=out_specs,
        out_shape=out_shape,
        compiler_params=_params("parallel", "parallel", "parallel"),
        name="fft_stage_ca" if chain else "fft_stage_c",
    )(*args)
    return (res[0].reshape(bt, l, e),) + tuple(res[1:])


def _hyena_long(proj, f_re, f_im, inv_s, d_bias, tabs, mf, mi):
    order, e = d_bias.shape
    kf_re, kf_im = _fft_stage_b_filter(f_re, f_im, inv_s, mf, tabs)
    u_arr, u_group = proj, 0
    a_re, a_im = _fft_stage_a(u_arr, u_group, tabs)
    for o in range(order):
        b_re, b_im = _fft_stage_b_conv(a_re, a_im, kf_re, kf_im, o, mf, mi, tabs)
        chain = o + 1 < order
        res = _fft_stage_c(b_re, b_im, u_arr, u_group, proj, o + 1, d_bias[o].reshape(1, e), tabs, chain)
        if chain:
            a_re, a_im = res[1:]
        u_arr, u_group = res[0][None], 0
    return res[0]


@functools.lru_cache(maxsize=None)
def _dft_tables(l):
    n = 2 * l
    nf = l + 1
    nfp = _round_up(nf, BF16_ROWS)
    c, s = _angles(np.arange(nf)[:, None] * np.arange(n)[None, :], n)
    fwd = np.zeros((2 * nfp, n))
    fwd[:nf], fwd[nfp:nfp + nf] = c, -s
    wgt = np.full((nf,), 2.0)
    wgt[0] = wgt[-1] = 1.0
    inv = np.zeros((l, 2 * nfp))
    inv[:, :nf], inv[:, nfp:nfp + nf] = (c[:, :l] * wgt[:, None]).T, (-s[:, :l] * wgt[:, None]).T
    bf = lambda a: jnp.asarray(a, jnp.float32).astype(jnp.bfloat16)
    return nfp, bf(fwd), bf(inv)


def _hyena_short_kernel(p_ref, k_ref, s_ref, d_ref, fwd_ref, inv_ref, o_ref, *, l, nfp, order):
    fwd = fwd_ref[...]
    u = p_ref[0, 0]
    for o in range(order):
        kn = (k_ref[o] * s_ref[o, 0:1, :]).astype(jnp.bfloat16)
        kf = jnp.dot(fwd, kn, preferred_element_type=jnp.float32)
        uf = jnp.dot(fwd[:, :l], u.astype(jnp.bfloat16), preferred_element_type=jnp.float32)
        k_re, k_im, u_re, u_im = kf[:nfp], kf[nfp:], uf[:nfp], uf[nfp:]
        p = jnp.concatenate([u_re * k_re - u_im * k_im, u_re * k_im + u_im * k_re], axis=0)
        y = jnp.dot(inv_ref[...], p.astype(jnp.bfloat16), preferred_element_type=jnp.float32)
        u = p_ref[o + 1, 0] * (y + u * d_ref[o:o + 1, :])
    o_ref[0] = u


def _hyena_short(proj, k, inv_s, d_bias, e):
    nt, b, l, _ = proj.shape
    order = k.shape[0]
    nfp, fwd, inv = _dft_tables(l)
    return pl.pallas_call(
        functools.partial(_hyena_short_kernel, l=l, nfp=nfp, order=order),
        grid=(b,),
        in_specs=[pl.BlockSpec((nt, 1, l, e), lambda bb: (0, bb, 0, 0)),
                  pl.BlockSpec((order, 2 * l, e), lambda bb: (0, 0, 0)),
                  pl.BlockSpec((order, SUBLANES, e), lambda bb: (0, 0, 0)),
                  pl.BlockSpec((order, e), lambda bb: (0, 0)),
                  pl.BlockSpec(fwd.shape, lambda bb: (0, 0)),
                  pl.BlockSpec(inv.shape, lambda bb: (0, 0))],
        out_specs=pl.BlockSpec((1, l, e), lambda bb: (bb, 0, 0)),
        out_shape=jax.ShapeDtypeStruct((b, l, e), jnp.float32),
        compiler_params=_params("parallel"),
        name="hyena_short",
    )(proj, k, inv_s, d_bias, fwd, inv)


def kernel(x, c, ctx, c_ctx, w_mod, b_mod, ln_g, ln_b, rg_w_in, rg_conv_w, rg_conv_b, rg_w_r, rg_b_r, rg_w_i, rg_b_i, rg_lambda, rg_w_out, hy_w_in, hy_conv_w, hy_conv_b, hy_f_w1, hy_f_b1, hy_f_w2, hy_f_b2, hy_f_w3, hy_f_freq, hy_d, hy_w_out):
    bsz, seq, d = x.shape
    depth = w_mod.shape[0]
    e = rg_w_out.shape[1]
    alpha = (2 * depth) ** 0.25
    assert bsz + 1 <= SUBLANES

    cvec = jnp.zeros((SUBLANES, d), jnp.float32).at[:bsz].set(c).at[bsz].set(c_ctx)
    mod = _modulation(cvec, w_mod, b_mod)

    def mod_rows(layer, part, for_ctx):
        v = mod[layer, :, part * d:(part + 1) * d]
        v = jnp.broadcast_to(v[bsz], (bsz, d)) if for_ctx else v[:bsz]
        return v.reshape(bsz, 1, d)

    fft = None

    def filter_hidden(occ, l):
        zz, tcol = _filter_positions(l, hy_f_w1.shape[-1])
        hid = _filter_hidden(zz, hy_f_w1[occ], hy_f_b1[occ], hy_f_w2[occ], hy_f_b2[occ], hy_f_freq[occ])
        return hid, tcol

    for i in range(depth):
        kind = i % N_MIXERS
        occ = i // N_MIXERS
        need_ctx_out = any(j % N_MIXERS == 0 for j in range(i + 1, depth))
        col_major = occ % 2 == 1
        use_ctx = kind == 0 or need_ctx_out
        shift, scale, gate = (mod_rows(i, p, False) for p in range(3))
        if use_ctx:
            shift_c, scale_c, gate_c = (mod_rows(i, p, True) for p in range(3))

        if kind == 0:
            w_in = rg_w_in[occ].astype(jnp.bfloat16)
            w_out = rg_w_out[occ].astype(jnp.bfloat16)
            proj = _inproj(x, shift, scale, w_in, rg_conv_w[occ], rg_conv_b[occ],
                           e=e, pad_l=1, col_major=col_major)
            w_in_c = w_in if need_ctx_out else w_in[:, :e]
            proj_c = _inproj(ctx, shift_c, scale_c, w_in_c, rg_conv_w[occ], rg_conv_b[occ],
                             e=e, pad_l=1, col_major=False)
            zeros = jnp.zeros((bsz, 1, e), jnp.float32)
            gates = [(_gate_weights(rg_w_r[occ, dd], rg_w_i[occ, dd]), rg_b_r[occ, dd], rg_b_i[occ, dd],
                      rg_lambda[occ, dd]) for dd in range(2)]
            hc_f, st_f = _rg_scan(proj_c, zeros, *gates[0], reverse=False, write_seq=need_ctx_out)
            h_f, _ = _rg_scan(proj, st_f, *gates[0], reverse=False)
            y_c, st_b = _rg_scan(proj_c, zeros, *gates[1], reverse=True, prev=hc_f, write_seq=need_ctx_out)
            x_new, _ = _rg_scan(proj, st_b, *gates[1], reverse=True, prev=h_f,
                                out=(w_out, x, gate, ln_g[i], ln_b[i], alpha, col_major))
            g_group = 1
        else:
            w_in = hy_w_in[occ].astype(jnp.bfloat16)
            w_out = hy_w_out[occ].astype(jnp.bfloat16)
            proj = _inproj(x, shift, scale, w_in, hy_conv_w[occ], hy_conv_b[occ],
                           e=e, pad_l=1, col_major=col_major)
            if fft is None:
                tabs = _fft_tables(2 * seq // FFT_N2)
                fft = (tabs,) + tuple(_twiddled_dfts(tabs))
            f_re, f_im, inv_s = _filter_fft_a(*filter_hidden(occ, seq), hy_f_w3[occ], e, fft[0])
            y = _hyena_long(proj, f_re, f_im, inv_s, hy_d[occ], *fft)
            if need_ctx_out:
                proj_c = _inproj(ctx, shift_c, scale_c, w_in, hy_conv_w[occ], hy_conv_b[occ],
                                 e=e, pad_l=1, col_major=False)
                k_c, inv_s_c = _filters(*filter_hidden(occ, ctx.shape[1]), hy_f_w3[occ], e)
                y_c = _hyena_short(proj_c, k_c, inv_s_c, hy_d[occ], e)
            g_group = 3
            x_new = _outproj(y, proj, g_group, w_out, x, gate, ln_g[i], ln_b[i],
                             alpha=alpha, col_major=col_major)

        x = x_new
        if need_ctx_out:
            ctx = _outproj(y_c, proj_c, g_group, w_out, ctx, gate_c, ln_g[i], ln_b[i],
                           alpha=alpha, col_major=False)
    return x
```

```python
import functools
import math

import numpy as np
import jax
import jax.numpy as jnp
from jax import lax
from jax.experimental import pallas as pl
from jax.experimental.pallas import tpu as pltpu

GRID_W = 64
N_MIXERS = 2
RG_C = 8.0
RG_GATE_GROUP = 4
HY_EMB_BANDS = 8
HY_FAST_DECAY_PCT = 0.3
HY_SLOW_DECAY_PCT = 1.5
HY_DECAY_TARGET = 1e-2
HY_MOD_SHIFT = 0.05
LN_EPS = 1e-5

SUBLANES = 8
LANES = 128
BF16_ROWS = 16
FFT_N2 = 256
FFT_J = SUBLANES
FFT_EC = 768
STREAM_TILE = 512
SCAN_CHUNK = 512
SCAN_SLABS = 4
LANE_CHUNK = 256
VMEM_LIMIT = 56 * 1024 * 1024

_HIGHEST = lax.Precision.HIGHEST


def _round_up(a, m):
    return (a + m - 1) // m * m


def _params(*sem):
    return pltpu.CompilerParams(dimension_semantics=sem, vmem_limit_bytes=VMEM_LIMIT)


def _sigmoid(v):
    return 1.0 / (1.0 + jnp.exp(-v))


def _mod_kernel(c_ref, w_ref, b_ref, o_ref):
    cv = c_ref[...]
    act = cv * _sigmoid(cv)
    o_ref[0] = jnp.dot(act, w_ref[0], preferred_element_type=jnp.float32,
                       precision=_HIGHEST) + b_ref[0]


def _modulation(cvec, w_mod, b_mod):
    depth, d, d3 = w_mod.shape
    nblk = d3 // d
    return pl.pallas_call(
        _mod_kernel,
        grid=(depth, nblk),
        in_specs=[pl.BlockSpec((SUBLANES, d), lambda l, n: (0, 0)),
                  pl.BlockSpec((1, d, d), lambda l, n: (l, 0, n)),
                  pl.BlockSpec((1, 1, d), lambda l, n: (l, 0, n))],
        out_specs=pl.BlockSpec((1, SUBLANES, d), lambda l, n: (l, 0, n)),
        out_shape=jax.ShapeDtypeStruct((depth, SUBLANES, d3), jnp.float32),
        compiler_params=_params("parallel", "parallel"),
        name="modulation",
    )(cvec, w_mod, b_mod.reshape(depth, 1, d3))


def _stream_view(x, col_major):
    b, l, d = x.shape
    if col_major:
        return x.reshape(b, l // GRID_W, GRID_W * d)
    return x


def _stream_specs(l, d, tm, col_major, nb_grid_prefix):
    def wrap(fn):
        return lambda *g: fn(*g[nb_grid_prefix:])
    if col_major:
        rows = l // GRID_W
        ncol = tm // rows
        assert tm == rows * ncol
        main = pl.BlockSpec((1, rows, ncol * d), wrap(lambda b, i: (b, 0, i)))
        prev = pl.BlockSpec((1, SUBLANES, d),
                            wrap(lambda b, i: (b, rows // SUBLANES - 1, jnp.maximum(i * ncol - 1, 0))))
        nxt = pl.BlockSpec((1, SUBLANES, d),
                           wrap(lambda b, i: (b, 0, jnp.minimum((i + 1) * ncol, GRID_W - 1))))
    else:
        per = tm // SUBLANES
        last = l // SUBLANES - 1
        main = pl.BlockSpec((1, tm, d), wrap(lambda b, i: (b, i, 0)))
        prev = pl.BlockSpec((1, SUBLANES, d), wrap(lambda b, i: (b, jnp.maximum(i * per - 1, 0), 0)))
        nxt = pl.BlockSpec((1, SUBLANES, d), wrap(lambda b, i: (b, jnp.minimum((i + 1) * per, last), 0)))
    return main, prev, nxt


def _tile_rows(l, col_major):
    tm = min(l, STREAM_TILE)
    if col_major:
        rows = l // GRID_W
        return max(tm // rows, 1) * rows
    return tm


def _tile_tokens(ref, d):
    blk = ref[0]
    ncol = blk.shape[1] // d
    if ncol == 1:
        return blk
    return jnp.concatenate([blk[:, j * d:(j + 1) * d] for j in range(ncol)], axis=0)


def _inproj_kernel(x_ref, xp_ref, xn_ref, sh_ref, sc_ref, w_ref, cw_ref, cb_ref, o_ref, u_scr,
                   *, n_conv, conv_k, pad_l, tm):
    i = pl.program_id(1)
    nt = pl.num_programs(1)
    d = w_ref.shape[0]
    e = o_ref.shape[3]
    scale = 1.0 + sc_ref[0]
    shift = sh_ref[0]
    mod = lambda v: v * scale + shift
    h_prev = jnp.where(i > 0, mod(xp_ref[0]), 0.0)
    h_next = jnp.where(i < nt - 1, mod(xn_ref[0]), 0.0)
    hh = jnp.concatenate([h_prev, mod(_tile_tokens(x_ref, d)), h_next], axis=0).astype(jnp.bfloat16)
    for n in range(o_ref.shape[0]):
        for c0 in range(0, e, LANE_CHUNK):
            sl = slice(c0, min(c0 + LANE_CHUNK, e))
            u = jnp.dot(hh, w_ref[:, n * e + sl.start:n * e + sl.stop], preferred_element_type=jnp.float32)
            if n >= n_conv:
                o_ref[n, 0, :, sl] = u[SUBLANES:SUBLANES + tm].astype(o_ref.dtype)
                continue
            for t0 in range(sl.start, sl.stop, LANES):
                slab = t0 // LANES
                u_scr[slab] = u[:, t0 - sl.start:t0 - sl.start + LANES]
                col = slice(n * e + t0, n * e + t0 + LANES)
                acc = cb_ref[:, col]
                for k in range(conv_k):
                    tap = u_scr[slab, pl.ds(SUBLANES - pad_l + k, tm, stride=1), :]
                    acc = acc + cw_ref[k:k + 1, col] * tap
                o_ref[n, 0, :, t0:t0 + LANES] = acc.astype(o_ref.dtype)


def _inproj(x, shift, scale, w_bf16, conv_w, conv_b, *, e, pad_l, col_major):
    b, l, d = x.shape
    nt_groups = w_bf16.shape[1] // e
    conv_k = conv_w.shape[0]
    n_conv = conv_w.shape[1] // e
    tm = _tile_rows(l, col_major)
    ntile = l // tm
    main, prev, nxt = _stream_specs(l, d, tm, col_major, 0)
    xv = _stream_view(x, col_major)
    kern = functools.partial(_inproj_kernel, n_conv=n_conv, conv_k=conv_k, pad_l=pad_l, tm=tm)
    const = lambda shape: pl.BlockSpec(shape, lambda bb, i: (0, 0), pipeline_mode=pl.Buffered(1))
    return pl.pallas_call(
        kern,
        grid=(b, ntile),
        in_specs=[main, prev, nxt,
                  pl.BlockSpec((1, 1, d), lambda bb, i: (bb, 0, 0)),
                  pl.BlockSpec((1, 1, d), lambda bb, i: (bb, 0, 0)),
                  const(w_bf16.shape), const(conv_w.shape), const((1, conv_w.shape[1]))],
        out_specs=pl.BlockSpec((nt_groups, 1, tm, e), lambda bb, i: (0, bb, i, 0)),
        out_shape=jax.ShapeDtypeStruct((nt_groups, b, l, e), jnp.bfloat16),
        scratch_shapes=[pltpu.VMEM((e // LANES, tm + 2 * SUBLANES, LANES), jnp.float32)],
        compiler_params=_params("parallel", "parallel"),
        name="inproj_conv",
    )(xv, xv, xv, shift, scale, w_bf16, conv_w, conv_b.reshape(1, -1))


def _rg_scan_kernel(*refs, reverse, add_prev, write_seq, out_alpha, tm, gw, ngroups):
    it = iter(refs)
    xc_ref = next(it)
    prev_ref = next(it) if add_prev else None
    h0_ref, wg_ref, br_ref, bi_ref, lam_ref = next(it), next(it), next(it), next(it), next(it)
    out_refs = [next(it) for _ in range(6)] if out_alpha is not None else None
    o_ref = next(it) if write_seq else None
    st_ref = next(it)
    a_scr, b_scr, carry = next(it), next(it), next(it)
    seq_ref = next(it) if out_alpha is not None else o_ref

    i = pl.program_id(1)
    nchunk = pl.num_programs(1)
    nslab, e = a_scr.shape[0], carry.shape[2]
    base = 0 if reverse else SUBLANES
    edge = tm if reverse else 0

    @pl.when(i == 0)
    def _():
        ones = jnp.ones((SUBLANES, e), jnp.float32)
        zeros = jnp.zeros((SUBLANES, e), jnp.float32)
        carry[0], carry[1], carry[2], carry[3] = ones, zeros, ones, zeros
        carry[4] = jnp.broadcast_to(h0_ref[0], (SUBLANES, e))
        a_scr[:, edge:edge + SUBLANES, :] = jnp.ones((nslab, SUBLANES, LANES), jnp.float32)
        b_scr[:, edge:edge + SUBLANES, :] = jnp.zeros((nslab, SUBLANES, LANES), jnp.float32)

    lam = lam_ref[...]
    neg = -lam
    softplus = jnp.maximum(neg, 0.0) + jnp.log(1.0 + jnp.exp(-jnp.abs(neg)))
    rate = (0.5 * RG_C) * softplus
    for g in range(ngroups):
        sl = slice(g * gw, (g + 1) * gw)
        xg = xc_ref[0, 0, :, sl]
        pre = jnp.dot(xg, wg_ref[g], preferred_element_type=jnp.float32)
        t_r = jnp.tanh(pre[:, :gw] + br_ref[:, sl])
        t_i = jnp.tanh(pre[:, gw:] + bi_ref[:, sl])
        neg_log_a = rate[:, sl] * t_r + rate[:, sl]
        a = jnp.exp(-neg_log_a)
        s = jnp.tanh(neg_log_a) * (a * a + 1.0)
        mult = jnp.where(s > 0.0, s * lax.rsqrt(s), 0.0)
        bx = (0.5 * mult) * ((t_i + 1.0) * xg.astype(jnp.float32))
        for j in range(gw // LANES):
            slab = g * (gw // LANES) + j
            a_scr[slab, base:base + tm, :] = a[:, j * LANES:(j + 1) * LANES]
            b_scr[slab, base:base + tm, :] = bx[:, j * LANES:(j + 1) * LANES]

    nblk = tm // BF16_ROWS
    step = 1 if reverse else -1
    per = min(nslab, SCAN_SLABS)
    row = lax.broadcasted_iota(jnp.int32, (SUBLANES, per * LANES), 0)

    def from_behind(cur, prev_rolled, dist):
        rolled = pltpu.roll(cur, (SUBLANES - dist) if reverse else dist, 0)
        inside = (row + dist < SUBLANES) if reverse else (row >= dist)
        return jnp.where(inside, rolled, prev_rolled), rolled

    for s0 in range(0, nslab, per):
        lanes = slice(s0 * LANES, (s0 + per) * LANES)

        def rows(ref, start):
            return jnp.concatenate([ref[s, pl.ds(start, SUBLANES, stride=1), :]
                                    for s in range(s0, s0 + per)], axis=1)

        def scan8(r0, st):
            ra1, rh1, ra2, rh2, h_prev = st
            a, b = rows(a_scr, base + r0), rows(b_scr, base + r0)
            a1 = a * rows(a_scr, base + r0 + step)
            h1 = b + a * rows(b_scr, base + r0 + step)
            a1s, ra1 = from_behind(a1, ra1, 2)
            h1s, rh1 = from_behind(h1, rh1, 2)
            a2, h2 = a1 * a1s, h1 + a1 * h1s
            a2s, ra2 = from_behind(a2, ra2, 4)
            h2s, rh2 = from_behind(h2, rh2, 4)
            h = (h2 + a2 * h2s) + (a2 * a2s) * h_prev
            return h, (ra1, rh1, ra2, rh2, h)

        def block(k, c):
            kk = (nblk - 1 - k) if reverse else k
            r0 = pl.multiple_of(kk * BF16_ROWS, BF16_ROWS)
            halves = [None, None]
            for hh in ((1, 0) if reverse else (0, 1)):
                halves[hh], c = scan8(r0 + hh * SUBLANES, c)
            if write_seq:
                h = jnp.concatenate(halves, axis=0)
                if add_prev:
                    h = h + prev_ref[0, pl.ds(r0, BF16_ROWS), lanes].astype(jnp.float32)
                if out_alpha is None:
                    seq_ref[0, pl.ds(r0, BF16_ROWS), lanes] = h.astype(seq_ref.dtype)
                else:
                    seq_ref[pl.ds(r0, BF16_ROWS), lanes] = h.astype(seq_ref.dtype)
            return c

        c_fin = lax.fori_loop(0, nblk, block, tuple(carry[j, :, lanes] for j in range(5)), unroll=2)
        for j in range(5):
            carry[j, :, lanes] = c_fin[j]

    last = base + (0 if reverse else tm - SUBLANES)
    a_scr[:, edge:edge + SUBLANES, :] = a_scr[:, last:last + SUBLANES, :]
    b_scr[:, edge:edge + SUBLANES, :] = b_scr[:, last:last + SUBLANES, :]

    @pl.when(i == nchunk - 1)
    def _():
        st_ref[0] = carry[4, 0:1, :] if reverse else carry[4, SUBLANES - 1:SUBLANES, :]

    if out_alpha is not None:
        g_ref, w_ref, x_ref, gate_ref, lg_ref, lb_ref = out_refs
        _outproj_tile(seq_ref[...], g_ref[0, 0], w_ref, x_ref, gate_ref, lg_ref, lb_ref, o_ref, out_alpha)


def _rg_scan(proj, h0, wg, b_r, b_i, lam, *, reverse, prev=None, write_seq=True, out=None):
    _, b, l, e = proj.shape
    tm = min(l, SCAN_CHUNK)
    nchunk = l // tm
    ngroups, gw, _ = wg.shape
    pos = (lambda i: nchunk - 1 - i) if reverse else (lambda i: i)
    add_prev = prev is not None
    in_specs = [pl.BlockSpec((1, 1, tm, e), lambda bb, i: (0, bb, pos(i), 0))]
    args = [proj]
    if add_prev:
        in_specs.append(pl.BlockSpec((1, tm, e), lambda bb, i: (bb, pos(i), 0)))
        args.append(prev)
    in_specs += [pl.BlockSpec((1, 1, e), lambda bb, i: (bb, 0, 0)),
                 pl.BlockSpec((ngroups, gw, 2 * gw), lambda bb, i: (0, 0, 0)),
                 pl.BlockSpec((1, e), lambda bb, i: (0, 0)),
                 pl.BlockSpec((1, e), lambda bb, i: (0, 0)),
                 pl.BlockSpec((1, e), lambda bb, i: (0, 0))]
    args += [h0, wg, 0.5 * b_r.reshape(1, e), 0.5 * b_i.reshape(1, e), lam.reshape(1, e)]
    out_specs, out_shape = [], []
    slabs = (e // LANES, tm + SUBLANES, LANES)
    scratch = [pltpu.VMEM(slabs, jnp.float32), pltpu.VMEM(slabs, jnp.float32),
               pltpu.VMEM((5, SUBLANES, e), jnp.float32)]
    out_alpha = None
    if out is not None:
        w_out, x, gate, ln_g, ln_b, out_alpha, col_major = out
        d = x.shape[-1]
        assert write_seq and tm == _tile_rows(l, col_major)
        main, _, _ = _stream_specs(l, d, tm, col_major, 0)
        tile = pl.BlockSpec(main.block_shape, lambda bb, i: main.index_map(bb, pos(i)))
        xv = _stream_view(x, col_major)
        in_specs += [pl.BlockSpec((1, 1, tm, e), lambda bb, i: (1, bb, pos(i), 0)),
                     pl.BlockSpec((e, d), lambda bb, i: (0, 0)),
                     tile,
                     pl.BlockSpec((1, 1, d), lambda bb, i: (bb, 0, 0)),
                     pl.BlockSpec((1, d), lambda bb, i: (0, 0)),
                     pl.BlockSpec((1, d), lambda bb, i: (0, 0))]
        args += [proj, w_out, xv, gate, ln_g.reshape(1, d), ln_b.reshape(1, d)]
        out_specs.append(tile)
        out_shape.append(jax.ShapeDtypeStruct(xv.shape, jnp.float32))
        scratch.append(pltpu.VMEM((tm, e), jnp.bfloat16))
    elif write_seq:
        out_specs.append(pl.BlockSpec((1, tm, e), lambda bb, i: (bb, pos(i), 0)))
        out_shape.append(jax.ShapeDtypeStruct((b, l, e), jnp.bfloat16))
    out_specs.append(pl.BlockSpec((1, 1, e), lambda bb, i: (bb, 0, 0)))
    out_shape.append(jax.ShapeDtypeStruct((b, 1, e), jnp.float32))
    kern = functools.partial(_rg_scan_kernel, reverse=reverse, add_prev=add_prev, write_seq=write_seq,
                             out_alpha=out_alpha, tm=tm, gw=gw, ngroups=ngroups)
    res = pl.pallas_call(
        kern,
        grid=(b, nchunk),
        in_specs=in_specs,
        out_specs=out_specs,
        out_shape=out_shape,
        scratch_shapes=scratch,
        compiler_params=_params("parallel", "arbitrary"),
        name="rg_scan_bwd" if reverse else "rg_scan_fwd",
    )(*args)
    if out is not None:
        return res[0].reshape(x.shape), res[1]
    if write_seq:
        return res[0], res[1]
    return None, res[0]


def _gate_weights(w_r, w_i):
    nb, bw, _ = w_r.shape
    g = RG_GATE_GROUP
    ng = nb // g
    eye = jnp.eye(g, dtype=w_r.dtype)

    def bd(w):
        w = w.reshape(ng, g, bw, bw)
        return jnp.einsum("nhij,hk->nhikj", w, eye).reshape(ng, g * bw, g * bw)

    return (0.5 * jnp.concatenate([bd(w_r), bd(w_i)], axis=-1)).astype(jnp.bfloat16)


def _outproj_tile(y, g, w_ref, x_ref, gate_ref, lg_ref, lb_ref, o_ref, alpha):
    d = w_ref.shape[1]
    hg = 0.5 * g.astype(jnp.float32)
    z = (y.astype(jnp.float32) * (hg * (jnp.tanh(hg) + 1.0))).astype(jnp.bfloat16)
    out = jnp.dot(z, w_ref[...], preferred_element_type=jnp.float32)
    r = alpha * _tile_tokens(x_ref, d) + gate_ref[0] * out
    mu = jnp.mean(r, axis=-1, keepdims=True)
    cen = r - mu
    var = jnp.mean(cen * cen, axis=-1, keepdims=True)
    res = cen * lax.rsqrt(var + LN_EPS) * lg_ref[...] + lb_ref[...]
    rows = o_ref.shape[1]
    for j in range(o_ref.shape[2] // d):
        o_ref[0, :, j * d:(j + 1) * d] = res[j * rows:(j + 1) * rows]


def _outproj_kernel(y_ref, g_ref, w_ref, x_ref, gate_ref, lg_ref, lb_ref, o_ref, *, alpha):
    _outproj_tile(y_ref[0], g_ref[0, 0], w_ref, x_ref, gate_ref, lg_ref, lb_ref, o_ref, alpha)


def _outproj(y, proj, g_group, w_bf16, x, gate, ln_g, ln_b, *, alpha, col_major):
    b, l, e = y.shape
    d = x.shape[-1]
    tm = _tile_rows(l, col_major)
    ntile = l // tm
    main, _, _ = _stream_specs(l, d, tm, col_major, 0)
    xv = _stream_view(x, col_major)
    out = pl.pallas_call(
        functools.partial(_outproj_kernel, alpha=alpha),
        grid=(b, ntile),
        in_specs=[pl.BlockSpec((1, tm, e), lambda bb, i: (bb, i, 0)),
                  pl.BlockSpec((1, 1, tm, e), lambda bb, i: (g_group, bb, i, 0)),
                  pl.BlockSpec((e, d), lambda bb, i: (0, 0)),
                  main,
                  pl.BlockSpec((1, 1, d), lambda bb, i: (bb, 0, 0)),
                  pl.BlockSpec((1, d), lambda bb, i: (0, 0)),
                  pl.BlockSpec((1, d), lambda bb, i: (0, 0))],
        out_specs=main,
        out_shape=jax.ShapeDtypeStruct(xv.shape, jnp.float32),
        compiler_params=_params("parallel", "parallel"),
        name="outproj_ln",
    )(y, proj, w_bf16, xv, gate, ln_g.reshape(1, d), ln_b.reshape(1, d))
    return out.reshape(x.shape)


def _filter_positions(l, fh):
    f32 = jnp.float32
    t = jnp.linspace(0.0, 1.0, l, dtype=f32)[:, None]
    bands = jnp.linspace(1e-4, HY_EMB_BANDS - 1, HY_EMB_BANDS, dtype=f32)
    w = (2.0 * math.pi) * jnp.arange(l, dtype=f32)[:, None] / l
    z = jnp.concatenate([t, jnp.cos(bands * w), -jnp.sin(bands * w)], axis=-1)
    z_rev = jnp.concatenate([z[:1], z[:0:-1]], axis=0)
    pad = lambda a: jnp.pad(a, ((0, 0), (0, fh - a.shape[1])))
    return jnp.concatenate([pad(z), pad(z_rev)], axis=1), jnp.stack([z[:, 0:1], z_rev[:, 0:1]])


def _hidden_kernel(z_ref, w1_ref, b1_ref, w2_ref, b2_ref, fr_ref, o_ref):
    fr = fr_ref[...]
    h = jnp.dot(z_ref[...], w1_ref[...], preferred_element_type=jnp.float32, precision=_HIGHEST)
    h = jnp.sin(fr * (h + b1_ref[...]))
    h = jnp.dot(h, w2_ref[...], preferred_element_type=jnp.float32, precision=_HIGHEST)
    o_ref[...] = jnp.sin(fr * (h + b2_ref[...]))


def _filter_hidden(zz, w1, b1, w2, b2, freq):
    n, zp = zz.shape
    fh = w1.shape[1]
    tr = min(n, 1024)
    both = lambda v: jnp.concatenate([v, v]).reshape(1, 2 * fh)
    blockdiag = lambda w: jnp.kron(jnp.eye(2, dtype=w.dtype), w)
    w1p = blockdiag(jnp.pad(w1, ((0, fh - w1.shape[0]), (0, 0))))
    full = lambda shape: pl.BlockSpec(shape, lambda i: (0,) * len(shape))
    return pl.pallas_call(
        _hidden_kernel,
        grid=(n // tr,),
        in_specs=[pl.BlockSpec((tr, zp), lambda i: (i, 0)), full((zp, zp)), full((1, zp)),
                  full((zp, zp)), full((1, zp)), full((1, zp))],
        out_specs=pl.BlockSpec((tr, zp), lambda i: (i, 0)),
        out_shape=jax.ShapeDtypeStruct((n, zp), jnp.float32),
        compiler_params=_params("parallel"),
        name="filter_hidden",
    )(zz, w1p, both(b1), blockdiag(w2), both(b2), both(freq))


def _split_bf16(a):
    hi = a.astype(jnp.bfloat16)
    return hi, (a - hi.astype(jnp.float32)).astype(jnp.bfloat16)


def _filter_kernel(hid_ref, t_ref, w3_ref, ad_ref, k_ref, s_ref, *, tr, n_total):
    d = pl.program_id(1)
    i = pl.program_id(2)
    last = (d == pl.num_programs(1) - 1) & (i == pl.num_programs(2) - 1)
    hi, lo = _split_bf16(hid_ref[...])
    h = jnp.dot(jnp.concatenate([hi, lo, hi], axis=1), w3_ref[0, 0], preferred_element_type=jnp.float32)
    window = jnp.exp(-t_ref[0] * ad_ref[...]) + HY_MOD_SHIFT
    row = lax.broadcasted_iota(jnp.int32, h.shape, 0)
    gap = (row == 0) & (d == 1) & (i == 0)
    k = jnp.where(gap, 0.0, h * window)
    k_ref[0] = k

    @pl.when((d == 0) & (i == 0))
    def _():
        s_ref[...] = jnp.zeros_like(s_ref)

    s_ref[0] += jnp.sum(jnp.abs(k).reshape(tr // SUBLANES, SUBLANES, -1), axis=0)

    @pl.when(last)
    def _():
        tot = jnp.sum(s_ref[0], axis=0, keepdims=True)
        s_ref[0] = jnp.broadcast_to(1.0 / (tot * n_total), s_ref.shape[1:])


def _filter_weights(w3, fh, e, compact):
    order = w3.shape[1] // (2 * e)
    max_decay = math.log(HY_DECAY_TARGET) / HY_FAST_DECAY_PCT
    min_decay = math.log(HY_DECAY_TARGET) / HY_SLOW_DECAY_PCT
    absdelta = jnp.abs(jnp.linspace(min_decay, max_decay, e, dtype=jnp.float32)).reshape(1, e)
    w3 = w3.reshape(fh, order, 2, e).transpose(1, 2, 0, 3)
    if compact:
        w_hi, w_lo = _split_bf16(w3)
        return jnp.concatenate([w_hi, w_hi, w_lo, jnp.zeros_like(w_hi)], axis=2), absdelta
    w3 = jnp.stack([jnp.pad(w3[:, 0], ((0, 0), (0, fh), (0, 0))),
                    jnp.pad(w3[:, 1], ((0, 0), (fh, 0), (0, 0)))], axis=1)
    w_hi, w_lo = _split_bf16(w3)
    return jnp.concatenate([w_hi, w_hi, w_lo], axis=2), absdelta


def _filter_fft_a_kernel(hid_ref, t_ref, w3_ref, ad_ref, m_ref, re_ref, im_ref, s_ref, *, n_total):
    j = pl.program_id(2)
    k1h, rows, fh2 = hid_ref.shape
    ec = ad_ref.shape[1]
    hi, lo = _split_bf16(hid_ref[...].reshape(k1h * rows, fh2))
    fh = fh2 // 2
    halves = []
    for d in range(2):
        own = slice(d * fh, (d + 1) * fh)
        stacked = jnp.concatenate([hi[:, own], lo[:, own], hi[:, own], jnp.zeros_like(hi[:, own])], axis=1)
        h = jnp.dot(stacked, w3_ref[0, d], preferred_element_type=jnp.float32)
        window = jnp.exp(-t_ref[d].reshape(k1h * rows, 1) * ad_ref[...]) + HY_MOD_SHIFT
        k = h * window
        if d == 1:
            row = lax.broadcasted_iota(jnp.int32, k.shape, 0)
            k = jnp.where((row == 0) & (j == 0), 0.0, k)
        halves.append(k)

    @pl.when(j == 0)
    def _():
        s_ref[...] = jnp.zeros_like(s_ref)

    k = jnp.concatenate(halves, axis=0)
    s_ref[0] += jnp.sum(jnp.abs(k).reshape(-1, SUBLANES, ec), axis=0)

    @pl.when(j == pl.num_programs(2) - 1)
    def _():
        tot = jnp.sum(s_ref[0], axis=0, keepdims=True)
        s_ref[0] = jnp.broadcast_to(1.0 / (tot * n_total), s_ref.shape[1:])

    k3 = k.reshape(2 * k1h, rows, ec)
    half = m_ref.shape[0] // 2
    f1p = half // FFT_J
    re_parts, im_parts = [], []
    for h in range(rows // FFT_J):
        x = k3[:, h * FFT_J:(h + 1) * FFT_J, :].reshape(2 * k1h * FFT_J, ec)
        r = jnp.dot(m_ref[...], x.astype(jnp.bfloat16), preferred_element_type=jnp.float32)
        re_parts.append(r[:half].reshape(f1p, FFT_J, ec))
        im_parts.append(r[half:].reshape(f1p, FFT_J, ec))
    re_ref[0] = jnp.concatenate(re_parts, axis=1).astype(jnp.bfloat16)
    im_ref[0] = jnp.concatenate(im_parts, axis=1).astype(jnp.bfloat16)


def _filter_fft_a(hid, tcol, w3, e, tabs):
    l, fh2 = hid.shape
    n2 = FFT_N2
    k1h = l // n2
    f1p = tabs["f1p"]
    mat = tabs["a_full"]
    w_stack, absdelta = _filter_weights(w3, fh2 // 2, e, compact=True)
    order = w_stack.shape[0]
    ec = min(e, FFT_EC)
    out_spec = pl.BlockSpec((1, f1p, BF16_ROWS, ec), lambda o, cc, j: (o, 0, j, cc))
    shape = jax.ShapeDtypeStruct((order, f1p, n2, e), jnp.bfloat16)
    return pl.pallas_call(
        functools.partial(_filter_fft_a_kernel, n_total=float(2 * l)),
        grid=(order, e // ec, n2 // BF16_ROWS),
        in_specs=[pl.BlockSpec((k1h, BF16_ROWS, fh2), lambda o, cc, j: (0, j, 0)),
                  pl.BlockSpec((2, k1h, BF16_ROWS, 1), lambda o, cc, j: (0, 0, j, 0)),
                  pl.BlockSpec((1, 2, w_stack.shape[2], ec), lambda o, cc, j: (o, 0, 0, cc)),
                  pl.BlockSpec((1, ec), lambda o, cc, j: (0, cc)),
                  pl.BlockSpec(mat.shape, lambda o, cc, j: (0, 0))],
        out_specs=[out_spec, out_spec,
                   pl.BlockSpec((1, SUBLANES, ec), lambda o, cc, j: (o, 0, cc))],
        out_shape=[shape, shape, jax.ShapeDtypeStruct((order, SUBLANES, e), jnp.float32)],
        compiler_params=_params("parallel", "parallel", "arbitrary"),
        name="filter_fft_a",
    )(hid.reshape(k1h, n2, fh2), tcol.reshape(2, k1h, n2, 1), w_stack, absdelta, mat)


def _filters(hid, tcol, w3, e):
    l, fh2 = hid.shape
    n = 2 * l
    tr = min(l, 512)
    nl = l // tr
    w_stack, absdelta = _filter_weights(w3, fh2 // 2, e, compact=False)
    order = w_stack.shape[0]
    return pl.pallas_call(
        functools.partial(_filter_kernel, tr=tr, n_total=float(n)),
        grid=(order, 2, nl),
        in_specs=[pl.BlockSpec((tr, fh2), lambda o, d, i: (i, 0)),
                  pl.BlockSpec((1, tr, 1), lambda o, d, i: (d, i, 0)),
                  pl.BlockSpec((1, 1, 3 * fh2, e), lambda o, d, i: (o, d, 0, 0)),
                  pl.BlockSpec((1, e), lambda o, d, i: (0, 0))],
        out_specs=[pl.BlockSpec((1, tr, e), lambda o, d, i: (o, d * nl + i, 0)),
                   pl.BlockSpec((1, SUBLANES, e), lambda o, d, i: (o, 0, 0))],
        out_shape=[jax.ShapeDtypeStruct((order, n, e), jnp.float32),
                   jax.ShapeDtypeStruct((order, SUBLANES, e), jnp.float32)],
        compiler_params=_params("parallel", "arbitrary", "arbitrary"),
        name="filter_gen",
    )(hid, tcol, w_stack, absdelta)


def _angles(num, den):
    ang = 2.0 * np.pi * (np.asarray(num, np.int64) % den).astype(np.float64) / den
    return np.cos(ang), np.sin(ang)


@functools.lru_cache(maxsize=None)
def _fft_tables(n1):
    n2 = FFT_N2
    n = n1 * n2
    f1 = n1 // 2 + 1
    f1p = _round_up(f1, 2)
    fr = np.arange(f1)[:, None]
    eye = np.eye(FFT_J)

    def stage_a(k1):
        c, s = _angles(fr * np.arange(k1)[None, :], n1)
        m = np.zeros((2 * f1p, k1))
        m[:f1], m[f1p:f1p + f1] = c, -s
        return np.kron(m, eye)

    nyq = n1 // 2
    wgt = np.full((nyq,), 2.0)
    wgt[0] = 1.0
    c, s = _angles(np.arange(n1 // 2)[:, None] * np.arange(nyq)[None, :], n1)
    m_c = np.concatenate([c * wgt, -s * wgt], axis=1)

    c2, s2 = _angles(np.arange(n2)[:, None] * np.arange(n2)[None, :], n2)
    cphi, sphi = _angles(fr * np.arange(n2)[None, :], n)
    phase = np.stack([cphi, sphi], axis=1)
    bf = lambda a: jnp.asarray(a, jnp.float32).astype(jnp.bfloat16)
    f32 = lambda a: jnp.asarray(a, jnp.float32)
    return dict(f1=f1, f1p=f1p, a_data=bf(stage_a(n1 // 2)), a_full=bf(stage_a(n1)),
                m_c=bf(np.kron(m_c, eye)), base_c=f32(c2), base_s=f32(s2),
                phase_row=f32(phase[:, :, None, :]), phase_col=f32(phase[..., None]))


def _twiddled_dft_kernel(c_ref, s_ref, row_ref, col_ref, mf_ref, mi_ref):
    c, s = c_ref[...], s_ref[...]
    n2 = c.shape[0]
    bf = jnp.bfloat16
    cr, sr = row_ref[0, 0], row_ref[0, 1]
    cf, sf = (c * cr - s * sr).astype(bf), (s * cr + c * sr).astype(bf)
    mf_ref[0, :n2, :n2] = cf
    mf_ref[0, :n2, n2:] = sf
    mf_ref[0, n2:, :n2] = -sf
    mf_ref[0, n2:, n2:] = cf
    cc, sc = col_ref[0, 0], col_ref[0, 1]
    ci, si = (c * cc - s * sc).astype(bf), (s * cc + c * sc).astype(bf)
    mi_ref[0, :n2, :n2] = ci
    mi_ref[0, :n2, n2:] = -si
    mi_ref[0, n2:, :n2] = si
    mi_ref[0, n2:, n2:] = ci


def _twiddled_dfts(tabs):
    f1 = tabs["f1"]
    n2 = FFT_N2
    shape = jax.ShapeDtypeStruct((f1, 2 * n2, 2 * n2), jnp.bfloat16)
    base = pl.BlockSpec((n2, n2), lambda f: (0, 0))
    out = pl.BlockSpec((1, 2 * n2, 2 * n2), lambda f: (f, 0, 0))
    return pl.pallas_call(
        _twiddled_dft_kernel,
        grid=(f1,),
        in_specs=[base, base, pl.BlockSpec((1, 2, 1, n2), lambda f: (f, 0, 0, 0)),
                  pl.BlockSpec((1, 2, n2, 1), lambda f: (f, 0, 0, 0))],
        out_specs=[out, out],
        out_shape=[shape, shape],
        compiler_params=_params("parallel"),
        name="twiddled_dft_tables",
    )(tabs["base_c"], tabs["base_s"], tabs["phase_row"], tabs["phase_col"])


def _fft_a_kernel(x_ref, m_ref, re_ref, im_ref):
    k1, ec = x_ref.shape[2], x_ref.shape[4]
    half = m_ref.shape[0] // 2
    f1p = half // FFT_J
    re_parts, im_parts = [], []
    xf = x_ref[0, 0].astype(jnp.float32)
    for h in range(x_ref.shape[3] // FFT_J):
        x = xf[:, h * FFT_J:(h + 1) * FFT_J, :].reshape(k1 * FFT_J, ec)
        r = jnp.dot(m_ref[...], x.astype(jnp.bfloat16), preferred_element_type=jnp.float32)
        re_parts.append(r[:half].reshape(f1p, FFT_J, ec))
        im_parts.append(r[half:].reshape(f1p, FFT_J, ec))
    re_ref[0] = jnp.concatenate(re_parts, axis=1).astype(jnp.bfloat16)
    im_ref[0] = jnp.concatenate(im_parts, axis=1).astype(jnp.bfloat16)


def _fft_stage_a(arr, group, tabs):
    _, bt, rows, e = arr.shape
    k1 = rows // FFT_N2
    f1p = tabs["f1p"]
    mat = tabs["a_data"]
    assert mat.shape[1] == k1 * FFT_J
    ec = min(e, FFT_EC)
    av = arr.reshape(arr.shape[0], bt, k1, FFT_N2, e)
    out_spec = pl.BlockSpec((1, f1p, BF16_ROWS, ec), lambda bb, j, cc: (bb, 0, j, cc))
    shape = jax.ShapeDtypeStruct((bt, f1p, FFT_N2, e), jnp.bfloat16)
    return pl.pallas_call(
        _fft_a_kernel,
        grid=(bt, FFT_N2 // BF16_ROWS, e // ec),
        in_specs=[pl.BlockSpec((1, 1, k1, BF16_ROWS, ec), lambda bb, j, cc: (group, bb, 0, j, cc)),
                  pl.BlockSpec(mat.shape, lambda bb, j, cc: (0, 0))],
        out_specs=[out_spec, out_spec],
        out_shape=[shape, shape],
        compiler_params=_params("parallel", "parallel", "parallel"),
        name="fft_stage_a",
    )(av, mat)


def _lane_chunks(e, width=256):
    return [slice(c0, min(c0 + width, e)) for c0 in range(0, e, width)]


def _fft_b_filter_kernel(re_ref, im_ref, sc_ref, mf_ref, kre_ref, kim_ref):
    for b in range(re_ref.shape[0]):
        for sl in _lane_chunks(re_ref.shape[3]):
            x = jnp.concatenate([re_ref[b, 0, :, sl], im_ref[b, 0, :, sl]], axis=0)
            u = jnp.dot(mf_ref[0], x, preferred_element_type=jnp.float32) * sc_ref[b, 0:1, sl]
            kre_ref[b, 0, :, sl] = u[:FFT_N2].astype(kre_ref.dtype)
            kim_ref[b, 0, :, sl] = u[FFT_N2:].astype(kim_ref.dtype)


def _fft_b_conv_kernel(re_ref, im_ref, kre_ref, kim_ref, mf_ref, mi_ref, ore_ref, oim_ref):
    for sl in _lane_chunks(re_ref.shape[3]):
        k_re = kre_ref[0, 0, :, sl].astype(jnp.float32)
        k_im = kim_ref[0, 0, :, sl].astype(jnp.float32)
        for b in range(re_ref.shape[0]):
            x = jnp.concatenate([re_ref[b, 0, :, sl], im_ref[b, 0, :, sl]], axis=0)
            u = jnp.dot(mf_ref[0], x, preferred_element_type=jnp.float32)
            u_re, u_im = u[:FFT_N2], u[FFT_N2:]
            p = jnp.concatenate([u_re * k_re - u_im * k_im, u_re * k_im + u_im * k_re], axis=0)
            v = jnp.dot(mi_ref[0], p.astype(jnp.bfloat16), preferred_element_type=jnp.float32)
            ore_ref[b, 0, :, sl] = v[:FFT_N2].astype(jnp.bfloat16)
            oim_ref[b, 0, :, sl] = v[FFT_N2:].astype(jnp.bfloat16)


def _fft_stage_b_filter(a_re, a_im, inv_s, mf, tabs):
    bt, _, n2, e = a_re.shape
    f1 = tabs["f1"]
    blk = pl.BlockSpec((bt, 1, n2, e), lambda f: (0, f, 0, 0))
    shape = jax.ShapeDtypeStruct((bt, f1, n2, e), jnp.bfloat16)
    return pl.pallas_call(
        _fft_b_filter_kernel,
        grid=(f1,),
        in_specs=[blk, blk, pl.BlockSpec(inv_s.shape, lambda f: (0, 0, 0)),
                  pl.BlockSpec((1, 2 * n2, 2 * n2), lambda f: (f, 0, 0))],
        out_specs=[blk, blk],
        out_shape=[shape, shape],
        compiler_params=_params("parallel"),
        name="fft_stage_b_filter",
    )(a_re, a_im, inv_s, mf)


def _fft_stage_b_conv(a_re, a_im, kf_re, kf_im, order_idx, mf, mi, tabs):
    bt, f1p, n2, e = a_re.shape
    f1 = tabs["f1"]
    fc = lambda f: jnp.minimum(f, f1 - 1)
    blk = pl.BlockSpec((bt, 1, n2, e), lambda f: (0, f, 0, 0))
    kblk = pl.BlockSpec((1, 1, n2, e), lambda f: (order_idx, fc(f), 0, 0))
    wblk = pl.BlockSpec((1, 2 * n2, 2 * n2), lambda f: (fc(f), 0, 0))
    shape = jax.ShapeDtypeStruct((bt, f1p, n2, e), jnp.bfloat16)
    return pl.pallas_call(
        _fft_b_conv_kernel,
        grid=(f1p,),
        in_specs=[blk, blk, kblk, kblk, wblk, wblk],
        out_specs=[blk, blk],
        out_shape=[shape, shape],
        compiler_params=_params("parallel"),
        name="fft_stage_b_conv",
    )(a_re, a_im, kf_re, kf_im, mf, mi)


def _fft_c_kernel(*refs, chain):
    re_ref, im_ref, m_ref, u_ref, mul_ref, d_ref = refs[:6]
    if chain:
        ma_ref, o_ref, are_ref, aim_ref = refs[6:]
    else:
        o_ref, = refs[6:]
    k1, ec = o_ref.shape[1], o_ref.shape[3]
    f1p = re_ref.shape[1]
    nyq = m_ref.shape[1] // (2 * FFT_J)
    re = re_ref[0].astype(jnp.float32)
    im = im_ref[0].astype(jnp.float32)
    uf = u_ref[0, 0].astype(jnp.float32)
    mf = mul_ref[0, 0].astype(jnp.float32)
    parts, re_parts, im_parts = [], [], []
    for h in range(o_ref.shape[2] // FFT_J):
        sl = slice(h * FFT_J, (h + 1) * FFT_J)
        x = jnp.concatenate([re[:nyq, sl, :].reshape(nyq * FFT_J, ec),
                             im[:nyq, sl, :].reshape(nyq * FFT_J, ec)], axis=0)
        y = jnp.dot(m_ref[...], x.astype(jnp.bfloat16), preferred_element_type=jnp.float32)
        r_nyq = re[nyq, sl, :]
        y = (y.reshape(k1 // 2, 2, FFT_J, ec) + jnp.stack([r_nyq, -r_nyq])[None]).reshape(k1 * FFT_J, ec)
        u = uf[:, sl, :].reshape(k1 * FFT_J, ec)
        mul = mf[:, sl, :].reshape(k1 * FFT_J, ec)
        z = (mul * (y + u * d_ref[...])).astype(jnp.bfloat16)
        parts.append(z.astype(jnp.float32).reshape(k1, FFT_J, ec))
        if chain:
            r = jnp.dot(ma_ref[...], z, preferred_element_type=jnp.float32)
            half = ma_ref.shape[0] // 2
            re_parts.append(r[:half].reshape(f1p, FFT_J, ec))
            im_parts.append(r[half:].reshape(f1p, FFT_J, ec))
    o_ref[0] = jnp.concatenate(parts, axis=1).astype(o_ref.dtype)
    if chain:
        are_ref[0] = jnp.concatenate(re_parts, axis=1).astype(jnp.bfloat16)
        aim_ref[0] = jnp.concatenate(im_parts, axis=1).astype(jnp.bfloat16)


def _fft_stage_c(b_re, b_im, u_arr, u_group, proj, mul_group, d_row, tabs, chain):
    bt, f1p, n2, e = b_re.shape
    l = u_arr.shape[2]
    k1 = l // n2
    ec = min(e, FFT_EC)
    uv = u_arr.reshape(u_arr.shape[0], bt, k1, n2, e)
    pv = proj.reshape(proj.shape[0], bt, k1, n2, e)
    mat = tabs["m_c"]
    fblk = pl.BlockSpec((1, f1p, BF16_ROWS, ec), lambda bb, j, cc: (bb, 0, j, cc))
    full = lambda m: pl.BlockSpec(m.shape, lambda bb, j, cc: (0, 0))
    in_specs = [fblk, fblk, full(mat),
                pl.BlockSpec((1, 1, k1, BF16_ROWS, ec), lambda bb, j, cc: (u_group, bb, 0, j, cc)),
                pl.BlockSpec((1, 1, k1, BF16_ROWS, ec), lambda bb, j, cc: (mul_group, bb, 0, j, cc)),
                pl.BlockSpec((1, ec), lambda bb, j, cc: (0, cc))]
    args = [b_re, b_im, mat, uv, pv, d_row]
    out_specs = [pl.BlockSpec((1, k1, BF16_ROWS, ec), lambda bb, j, cc: (bb, 0, j, cc))]
    out_shape = [jax.ShapeDtypeStruct((bt, k1, n2, e), jnp.bfloat16)]
    if chain:
        in_specs.append(full(tabs["a_data"]))
        args.append(tabs["a_data"])
        out_specs += [fblk, fblk]
        out_shape += [jax.ShapeDtypeStruct((bt, f1p, n2, e), jnp.bfloat16)] * 2
    res = pl.pallas_call(
        functools.partial(_fft_c_kernel, chain=chain),
        grid=(bt, n2 // BF16_ROWS, e // ec),
        in_specs=in_specs,
        out_specs=out_specs,
        out_shape=out_shape,
        compiler_params=_params("parallel", "parallel", "parallel"),
        name="fft_stage_ca" if chain else "fft_stage_c",
    )(*args)
    return (res[0].reshape(bt, l, e),) + tuple(res[1:])


def _hyena_long(proj, f_re, f_im, inv_s, d_bias, tabs, mf, mi):
    order, e = d_bias.shape
    kf_re, kf_im = _fft_stage_b_filter(f_re, f_im, inv_s, mf, tabs)
    u_arr, u_group = proj, 0
    a_re, a_im = _fft_stage_a(u_arr, u_group, tabs)
    for o in range(order):
        b_re, b_im = _fft_stage_b_conv(a_re, a_im, kf_re, kf_im, o, mf, mi, tabs)
        chain = o + 1 < order
        res = _fft_stage_c(b_re, b_im, u_arr, u_group, proj, o + 1, d_bias[o].reshape(1, e), tabs, chain)
        if chain:
            a_re, a_im = res[1:]
        u_arr, u_group = res[0][None], 0
    return res[0]


@functools.lru_cache(maxsize=None)
def _dft_tables(l):
    n = 2 * l
    nf = l + 1
    nfp = _round_up(nf, BF16_ROWS)
    c, s = _angles(np.arange(nf)[:, None] * np.arange(n)[None, :], n)
    fwd = np.zeros((2 * nfp, n))
    fwd[:nf], fwd[nfp:nfp + nf] = c, -s
    wgt = np.full((nf,), 2.0)
    wgt[0] = wgt[-1] = 1.0
    inv = np.zeros((l, 2 * nfp))
    inv[:, :nf], inv[:, nfp:nfp + nf] = (c[:, :l] * wgt[:, None]).T, (-s[:, :l] * wgt[:, None]).T
    bf = lambda a: jnp.asarray(a, jnp.float32).astype(jnp.bfloat16)
    return nfp, bf(fwd), bf(inv)


def _hyena_short_kernel(p_ref, k_ref, s_ref, d_ref, fwd_ref, inv_ref, o_ref, *, l, nfp, order):
    fwd = fwd_ref[...]
    u = p_ref[0, 0]
    for o in range(order):
        kn = (k_ref[o] * s_ref[o, 0:1, :]).astype(jnp.bfloat16)
        kf = jnp.dot(fwd, kn, preferred_element_type=jnp.float32)
        uf = jnp.dot(fwd[:, :l], u.astype(jnp.bfloat16), preferred_element_type=jnp.float32)
        k_re, k_im, u_re, u_im = kf[:nfp], kf[nfp:], uf[:nfp], uf[nfp:]
        p = jnp.concatenate([u_re * k_re - u_im * k_im, u_re * k_im + u_im * k_re], axis=0)
        y = jnp.dot(inv_ref[...], p.astype(jnp.bfloat16), preferred_element_type=jnp.float32)
        u = p_ref[o + 1, 0] * (y + u * d_ref[o:o + 1, :])
    o_ref[0] = u


def _hyena_short(proj, k, inv_s, d_bias, e):
    nt, b, l, _ = proj.shape
    order = k.shape[0]
    nfp, fwd, inv = _dft_tables(l)
    return pl.pallas_call(
        functools.partial(_hyena_short_kernel, l=l, nfp=nfp, order=order),
        grid=(b,),
        in_specs=[pl.BlockSpec((nt, 1, l, e), lambda bb: (0, bb, 0, 0)),
                  pl.BlockSpec((order, 2 * l, e), lambda bb: (0, 0, 0)),
                  pl.BlockSpec((order, SUBLANES, e), lambda bb: (0, 0, 0)),
                  pl.BlockSpec((order, e), lambda bb: (0, 0)),
                  pl.BlockSpec(fwd.shape, lambda bb: (0, 0)),
                  pl.BlockSpec(inv.shape, lambda bb: (0, 0))],
        out_specs=pl.BlockSpec((1, l, e), lambda bb: (bb, 0, 0)),
        out_shape=jax.ShapeDtypeStruct((b, l, e), jnp.float32),
        compiler_params=_params("parallel"),
        name="hyena_short",
    )(proj, k, inv_s, d_bias, fwd, inv)


def kernel(x, c, ctx, c_ctx, w_mod, b_mod, ln_g, ln_b, rg_w_in, rg_conv_w, rg_conv_b, rg_w_r, rg_b_r, rg_w_i, rg_b_i, rg_lambda, rg_w_out, hy_w_in, hy_conv_w, hy_conv_b, hy_f_w1, hy_f_b1, hy_f_w2, hy_f_b2, hy_f_w3, hy_f_freq, hy_d, hy_w_out):
    bsz, seq, d = x.shape
    depth = w_mod.shape[0]
    e = rg_w_out.shape[1]
    alpha = (2 * depth) ** 0.25
    assert bsz + 1 <= SUBLANES

    cvec = jnp.zeros((SUBLANES, d), jnp.float32).at[:bsz].set(c).at[bsz].set(c_ctx)
    mod = _modulation(cvec, w_mod, b_mod)

    def mod_rows(layer, part, for_ctx):
        v = mod[layer, :, part * d:(part + 1) * d]
        v = jnp.broadcast_to(v[bsz], (bsz, d)) if for_ctx else v[:bsz]
        return v.reshape(bsz, 1, d)

    fft = None

    def filter_hidden(occ, l):
        zz, tcol = _filter_positions(l, hy_f_w1.shape[-1])
        hid = _filter_hidden(zz, hy_f_w1[occ], hy_f_b1[occ], hy_f_w2[occ], hy_f_b2[occ], hy_f_freq[occ])
        return hid, tcol

    for i in range(depth):
        kind = i % N_MIXERS
        occ = i // N_MIXERS
        need_ctx_out = any(j % N_MIXERS == 0 for j in range(i + 1, depth))
        col_major = occ % 2 == 1
        use_ctx = kind == 0 or need_ctx_out
        shift, scale, gate = (mod_rows(i, p, False) for p in range(3))
        if use_ctx:
            shift_c, scale_c, gate_c = (mod_rows(i, p, True) for p in range(3))

        if kind == 0:
            w_in = rg_w_in[occ].astype(jnp.bfloat16)
            w_out = rg_w_out[occ].astype(jnp.bfloat16)
            proj = _inproj(x, shift, scale, w_in, rg_conv_w[occ], rg_conv_b[occ],
                           e=e, pad_l=1, col_major=col_major)
            w_in_c = w_in if need_ctx_out else w_in[:, :e]
            proj_c = _inproj(ctx, shift_c, scale_c, w_in_c, rg_conv_w[occ], rg_conv_b[occ],
                             e=e, pad_l=1, col_major=False)
            zeros = jnp.zeros((bsz, 1, e), jnp.float32)
            gates = [(_gate_weights(rg_w_r[occ, dd], rg_w_i[occ, dd]), rg_b_r[occ, dd], rg_b_i[occ, dd],
                      rg_lambda[occ, dd]) for dd in range(2)]
            hc_f, st_f = _rg_scan(proj_c, zeros, *gates[0], reverse=False, write_seq=need_ctx_out)
            h_f, _ = _rg_scan(proj, st_f, *gates[0], reverse=False)
            y_c, st_b = _rg_scan(proj_c, zeros, *gates[1], reverse=True, prev=hc_f, write_seq=need_ctx_out)
            x_new, _ = _rg_scan(proj, st_b, *gates[1], reverse=True, prev=h_f,
                                out=(w_out, x, gate, ln_g[i], ln_b[i], alpha, col_major))
            g_group = 1
        else:
            w_in = hy_w_in[occ].astype(jnp.bfloat16)
            w_out = hy_w_out[occ].astype(jnp.bfloat16)
            proj = _inproj(x, shift, scale, w_in, hy_conv_w[occ], hy_conv_b[occ],
                           e=e, pad_l=1, col_major=col_major)
            if fft is None:
                tabs = _fft_tables(2 * seq // FFT_N2)
                fft = (tabs,) + tuple(_twiddled_dfts(tabs))
            f_re, f_im, inv_s = _filter_fft_a(*filter_hidden(occ, seq), hy_f_w3[occ], e, fft[0])
            y = _hyena_long(proj, f_re, f_im, inv_s, hy_d[occ], *fft)
            if need_ctx_out:
                proj_c = _inproj(ctx, shift_c, scale_c, w_in, hy_conv_w[occ], hy_conv_b[occ],
                                 e=e, pad_l=1, col_major=False)
                k_c, inv_s_c = _filters(*filter_hidden(occ, ctx.shape[1]), hy_f_w3[occ], e)
                y_c = _hyena_short(proj_c, k_c, inv_s_c, hy_d[occ], e)
            g_group = 3
            x_new = _outproj(y, proj, g_group, w_out, x, gate, ln_g[i], ln_b[i],
                             alpha=alpha, col_major=col_major)

        x = x_new
        if need_ctx_out:
            ctx = _outproj(y_c, proj_c, g_group, w_out, ctx, gate_c, ln_g[i], ln_b[i],
                           alpha=alpha, col_major=False)
    return x
```

```python
import functools
import math

import numpy as np
import jax
import jax.numpy as jnp
from jax import lax
from jax.experimental import pallas as pl
from jax.experimental.pallas import tpu as pltpu

GRID_W = 64
N_MIXERS = 2
RG_C = 8.0
RG_GATE_GROUP = 4
HY_EMB_BANDS = 8
HY_FAST_DECAY_PCT = 0.3
HY_SLOW_DECAY_PCT = 1.5
HY_DECAY_TARGET = 1e-2
HY_MOD_SHIFT = 0.05
LN_EPS = 1e-5

SUBLANES = 8
LANES = 128
BF16_ROWS = 16
FFT_N2 = 256
FFT_J = SUBLANES
FFT_EC = 768
STREAM_TILE = 512
SCAN_CHUNK = 512
OUT_ROWS = 512
SCAN_SLABS = 4
LANE_CHUNK = 256
VMEM_LIMIT = 56 * 1024 * 1024

_HIGHEST = lax.Precision.HIGHEST


def _round_up(a, m):
    return (a + m - 1) // m * m


def _params(*sem):
    return pltpu.CompilerParams(dimension_semantics=sem, vmem_limit_bytes=VMEM_LIMIT)


def _sigmoid(v):
    return 1.0 / (1.0 + jnp.exp(-v))


def _mod_kernel(c_ref, w_ref, b_ref, o_ref):
    cv = c_ref[...]
    act = cv * _sigmoid(cv)
    o_ref[0] = jnp.dot(act, w_ref[0], preferred_element_type=jnp.float32,
                       precision=_HIGHEST) + b_ref[0]


def _modulation(cvec, w_mod, b_mod):
    depth, d, d3 = w_mod.shape
    nblk = d3 // d
    return pl.pallas_call(
        _mod_kernel,
        grid=(depth, nblk),
        in_specs=[pl.BlockSpec((SUBLANES, d), lambda l, n: (0, 0)),
                  pl.BlockSpec((1, d, d), lambda l, n: (l, 0, n)),
                  pl.BlockSpec((1, 1, d), lambda l, n: (l, 0, n))],
        out_specs=pl.BlockSpec((1, SUBLANES, d), lambda l, n: (l, 0, n)),
        out_shape=jax.ShapeDtypeStruct((depth, SUBLANES, d3), jnp.float32),
        compiler_params=_params("parallel", "parallel"),
        name="modulation",
    )(cvec, w_mod, b_mod.reshape(depth, 1, d3))


def _stream_view(x, col_major):
    b, l, d = x.shape
    if col_major:
        return x.reshape(b, l // GRID_W, GRID_W * d)
    return x


def _stream_specs(l, d, tm, col_major, nb_grid_prefix):
    def wrap(fn):
        return lambda *g: fn(*g[nb_grid_prefix:])
    if col_major:
        rows = l // GRID_W
        ncol = tm // rows
        assert tm == rows * ncol
        main = pl.BlockSpec((1, rows, ncol * d), wrap(lambda b, i: (b, 0, i)))
        prev = pl.BlockSpec((1, SUBLANES, d),
                            wrap(lambda b, i: (b, rows // SUBLANES - 1, jnp.maximum(i * ncol - 1, 0))))
        nxt = pl.BlockSpec((1, SUBLANES, d),
                           wrap(lambda b, i: (b, 0, jnp.minimum((i + 1) * ncol, GRID_W - 1))))
    else:
        per = tm // SUBLANES
        last = l // SUBLANES - 1
        main = pl.BlockSpec((1, tm, d), wrap(lambda b, i: (b, i, 0)))
        prev = pl.BlockSpec((1, SUBLANES, d), wrap(lambda b, i: (b, jnp.maximum(i * per - 1, 0), 0)))
        nxt = pl.BlockSpec((1, SUBLANES, d), wrap(lambda b, i: (b, jnp.minimum((i + 1) * per, last), 0)))
    return main, prev, nxt


def _tile_rows(l, col_major):
    tm = min(l, STREAM_TILE)
    if col_major:
        rows = l // GRID_W
        return max(tm // rows, 1) * rows
    return tm


def _tile_tokens(ref, d):
    blk = ref[0]
    ncol = blk.shape[1] // d
    if ncol == 1:
        return blk
    return jnp.concatenate([blk[:, j * d:(j + 1) * d] for j in range(ncol)], axis=0)


def _inproj_kernel(x_ref, xp_ref, xn_ref, sh_ref, sc_ref, w_ref, cw_ref, cb_ref, o_ref, u_scr,
                   *, n_conv, conv_k, pad_l, tm):
    i = pl.program_id(1)
    nt = pl.num_programs(1)
    d = w_ref.shape[0]
    e = o_ref.shape[3]
    scale = 1.0 + sc_ref[0]
    shift = sh_ref[0]
    mod = lambda v: v * scale + shift
    h_prev = jnp.where(i > 0, mod(xp_ref[0]), 0.0)
    h_next = jnp.where(i < nt - 1, mod(xn_ref[0]), 0.0)
    hh = jnp.concatenate([h_prev, mod(_tile_tokens(x_ref, d)), h_next], axis=0).astype(jnp.bfloat16)
    for n in range(o_ref.shape[0]):
        for c0 in range(0, e, LANE_CHUNK):
            sl = slice(c0, min(c0 + LANE_CHUNK, e))
            u = jnp.dot(hh, w_ref[:, n * e + sl.start:n * e + sl.stop], preferred_element_type=jnp.float32)
            if n >= n_conv:
                o_ref[n, 0, :, sl] = u[SUBLANES:SUBLANES + tm].astype(o_ref.dtype)
                continue
            for t0 in range(sl.start, sl.stop, LANES):
                slab = t0 // LANES
                u_scr[slab] = u[:, t0 - sl.start:t0 - sl.start + LANES]
                col = slice(n * e + t0, n * e + t0 + LANES)
                acc = cb_ref[:, col]
                for k in range(conv_k):
                    tap = u_scr[slab, pl.ds(SUBLANES - pad_l + k, tm, stride=1), :]
                    acc = acc + cw_ref[k:k + 1, col] * tap
                o_ref[n, 0, :, t0:t0 + LANES] = acc.astype(o_ref.dtype)


def _inproj(x, shift, scale, w_bf16, conv_w, conv_b, *, e, pad_l, col_major):
    b, l, d = x.shape
    nt_groups = w_bf16.shape[1] // e
    conv_k = conv_w.shape[0]
    n_conv = conv_w.shape[1] // e
    tm = _tile_rows(l, col_major)
    ntile = l // tm
    main, prev, nxt = _stream_specs(l, d, tm, col_major, 0)
    xv = _stream_view(x, col_major)
    kern = functools.partial(_inproj_kernel, n_conv=n_conv, conv_k=conv_k, pad_l=pad_l, tm=tm)
    const = lambda shape: pl.BlockSpec(shape, lambda bb, i: (0, 0), pipeline_mode=pl.Buffered(1))
    return pl.pallas_call(
        kern,
        grid=(b, ntile),
        in_specs=[main, prev, nxt,
                  pl.BlockSpec((1, 1, d), lambda bb, i: (bb, 0, 0)),
                  pl.BlockSpec((1, 1, d), lambda bb, i: (bb, 0, 0)),
                  const(w_bf16.shape), const(conv_w.shape), const((1, conv_w.shape[1]))],
        out_specs=pl.BlockSpec((nt_groups, 1, tm, e), lambda bb, i: (0, bb, i, 0)),
        out_shape=jax.ShapeDtypeStruct((nt_groups, b, l, e), jnp.bfloat16),
        scratch_shapes=[pltpu.VMEM((e // LANES, tm + 2 * SUBLANES, LANES), jnp.float32)],
        compiler_params=_params("parallel", "parallel"),
        name="inproj_conv",
    )(xv, xv, xv, shift, scale, w_bf16, conv_w, conv_b.reshape(1, -1))


def _rg_scan_kernel(*refs, reverse, add_prev, write_seq, out_alpha, tm, gw, ngroups):
    it = iter(refs)
    xc_ref = next(it)
    prev_ref = next(it) if add_prev else None
    h0_ref, wg_ref, br_ref, bi_ref, lam_ref = next(it), next(it), next(it), next(it), next(it)
    out_refs = [next(it) for _ in range(6)] if out_alpha is not None else None
    o_ref = next(it) if write_seq else None
    st_ref = next(it)
    a_scr, b_scr, carry = next(it), next(it), next(it)
    seq_ref = next(it) if out_alpha is not None else o_ref

    i = pl.program_id(1)
    nchunk = pl.num_programs(1)
    nslab, e = a_scr.shape[0], carry.shape[2]
    base = 0 if reverse else SUBLANES
    edge = tm if reverse else 0

    @pl.when(i == 0)
    def _():
        ones = jnp.ones((SUBLANES, e), jnp.float32)
        zeros = jnp.zeros((SUBLANES, e), jnp.float32)
        carry[0], carry[1], carry[2], carry[3] = ones, zeros, ones, zeros
        carry[4] = jnp.broadcast_to(h0_ref[0], (SUBLANES, e))
        a_scr[:, edge:edge + SUBLANES, :] = jnp.ones((nslab, SUBLANES, LANES), jnp.float32)
        b_scr[:, edge:edge + SUBLANES, :] = jnp.zeros((nslab, SUBLANES, LANES), jnp.float32)

    lam = lam_ref[...]
    neg = -lam
    softplus = jnp.maximum(neg, 0.0) + jnp.log(1.0 + jnp.exp(-jnp.abs(neg)))
    rate = (0.5 * RG_C) * softplus
    for g in range(ngroups):
        sl = slice(g * gw, (g + 1) * gw)
        xg = xc_ref[0, 0, :, sl]
        pre = jnp.dot(xg, wg_ref[g], preferred_element_type=jnp.float32)
        t_r = jnp.tanh(pre[:, :gw] + br_ref[:, sl])
        t_i = jnp.tanh(pre[:, gw:] + bi_ref[:, sl])
        neg_log_a = rate[:, sl] * t_r + rate[:, sl]
        a = jnp.exp(-neg_log_a)
        s = jnp.tanh(neg_log_a) * (a * a + 1.0)
        mult = jnp.where(s > 0.0, s * lax.rsqrt(s), 0.0)
        bx = (0.5 * mult) * ((t_i + 1.0) * xg.astype(jnp.float32))
        for j in range(gw // LANES):
            slab = g * (gw // LANES) + j
            a_scr[slab, base:base + tm, :] = a[:, j * LANES:(j + 1) * LANES]
            b_scr[slab, base:base + tm, :] = bx[:, j * LANES:(j + 1) * LANES]

    nblk = tm // BF16_ROWS
    step = 1 if reverse else -1
    per = min(nslab, SCAN_SLABS)
    row = lax.broadcasted_iota(jnp.int32, (SUBLANES, per * LANES), 0)

    def from_behind(cur, prev_rolled, dist):
        rolled = pltpu.roll(cur, (SUBLANES - dist) if reverse else dist, 0)
        inside = (row + dist < SUBLANES) if reverse else (row >= dist)
        return jnp.where(inside, rolled, prev_rolled), rolled

    for s0 in range(0, nslab, per):
        lanes = slice(s0 * LANES, (s0 + per) * LANES)

        def rows(ref, start):
            return jnp.concatenate([ref[s, pl.ds(start, SUBLANES, stride=1), :]
                                    for s in range(s0, s0 + per)], axis=1)

        def scan8(r0, st):
            ra1, rh1, ra2, rh2, h_prev = st
            a, b = rows(a_scr, base + r0), rows(b_scr, base + r0)
            a1 = a * rows(a_scr, base + r0 + step)
            h1 = b + a * rows(b_scr, base + r0 + step)
            a1s, ra1 = from_behind(a1, ra1, 2)
            h1s, rh1 = from_behind(h1, rh1, 2)
            a2, h2 = a1 * a1s, h1 + a1 * h1s
            a2s, ra2 = from_behind(a2, ra2, 4)
            h2s, rh2 = from_behind(h2, rh2, 4)
            h = (h2 + a2 * h2s) + (a2 * a2s) * h_prev
            return h, (ra1, rh1, ra2, rh2, h)

        def block(k, c):
            kk = (nblk - 1 - k) if reverse else k
            r0 = pl.multiple_of(kk * BF16_ROWS, BF16_ROWS)
            halves = [None, None]
            for hh in ((1, 0) if reverse else (0, 1)):
                halves[hh], c = scan8(r0 + hh * SUBLANES, c)
            if write_seq:
                h = jnp.concatenate(halves, axis=0)
                if add_prev:
                    h = h + prev_ref[0, pl.ds(r0, BF16_ROWS), lanes].astype(jnp.float32)
                if out_alpha is None:
                    seq_ref[0, pl.ds(r0, BF16_ROWS), lanes] = h.astype(seq_ref.dtype)
                else:
                    seq_ref[pl.ds(r0, BF16_ROWS), lanes] = h.astype(seq_ref.dtype)
            return c

        c_fin = lax.fori_loop(0, nblk, block, tuple(carry[j, :, lanes] for j in range(5)), unroll=2)
        for j in range(5):
            carry[j, :, lanes] = c_fin[j]

    last = base + (0 if reverse else tm - SUBLANES)
    a_scr[:, edge:edge + SUBLANES, :] = a_scr[:, last:last + SUBLANES, :]
    b_scr[:, edge:edge + SUBLANES, :] = b_scr[:, last:last + SUBLANES, :]

    @pl.when(i == nchunk - 1)
    def _():
        st_ref[0] = carry[4, 0:1, :] if reverse else carry[4, SUBLANES - 1:SUBLANES, :]

    if out_alpha is not None:
        g_ref, w_ref, x_ref, gate_ref, lg_ref, lb_ref = out_refs
        _outproj_tile(seq_ref[...], g_ref[0, 0], w_ref, x_ref, gate_ref, lg_ref, lb_ref, o_ref, out_alpha)


def _rg_scan(proj, h0, wg, b_r, b_i, lam, *, reverse, prev=None, write_seq=True, out=None):
    _, b, l, e = proj.shape
    tm = min(l, SCAN_CHUNK)
    nchunk = l // tm
    ngroups, gw, _ = wg.shape
    pos = (lambda i: nchunk - 1 - i) if reverse else (lambda i: i)
    add_prev = prev is not None
    in_specs = [pl.BlockSpec((1, 1, tm, e), lambda bb, i: (0, bb, pos(i), 0))]
    args = [proj]
    if add_prev:
        in_specs.append(pl.BlockSpec((1, tm, e), lambda bb, i: (bb, pos(i), 0)))
        args.append(prev)
    in_specs += [pl.BlockSpec((1, 1, e), lambda bb, i: (bb, 0, 0)),
                 pl.BlockSpec((ngroups, gw, 2 * gw), lambda bb, i: (0, 0, 0)),
                 pl.BlockSpec((1, e), lambda bb, i: (0, 0)),
                 pl.BlockSpec((1, e), lambda bb, i: (0, 0)),
                 pl.BlockSpec((1, e), lambda bb, i: (0, 0))]
    args += [h0, wg, 0.5 * b_r.reshape(1, e), 0.5 * b_i.reshape(1, e), lam.reshape(1, e)]
    out_specs, out_shape = [], []
    slabs = (e // LANES, tm + SUBLANES, LANES)
    scratch = [pltpu.VMEM(slabs, jnp.float32), pltpu.VMEM(slabs, jnp.float32),
               pltpu.VMEM((5, SUBLANES, e), jnp.float32)]
    out_alpha = None
    if out is not None:
        w_out, x, gate, ln_g, ln_b, out_alpha, col_major = out
        d = x.shape[-1]
        assert write_seq and tm == _tile_rows(l, col_major)
        main, _, _ = _stream_specs(l, d, tm, col_major, 0)
        tile = pl.BlockSpec(main.block_shape, lambda bb, i: main.index_map(bb, pos(i)))
        xv = _stream_view(x, col_major)
        in_specs += [pl.BlockSpec((1, 1, tm, e), lambda bb, i: (1, bb, pos(i), 0)),
                     pl.BlockSpec((e, d), lambda bb, i: (0, 0)),
                     tile,
                     pl.BlockSpec((1, 1, d), lambda bb, i: (bb, 0, 0)),
                     pl.BlockSpec((1, d), lambda bb, i: (0, 0)),
                     pl.BlockSpec((1, d), lambda bb, i: (0, 0))]
        args += [proj, w_out, xv, gate, ln_g.reshape(1, d), ln_b.reshape(1, d)]
        out_specs.append(tile)
        out_shape.append(jax.ShapeDtypeStruct(xv.shape, jnp.float32))
        scratch.append(pltpu.VMEM((tm, e), jnp.bfloat16))
    elif write_seq:
        out_specs.append(pl.BlockSpec((1, tm, e), lambda bb, i: (bb, pos(i), 0)))
        out_shape.append(jax.ShapeDtypeStruct((b, l, e), jnp.bfloat16))
    out_specs.append(pl.BlockSpec((1, 1, e), lambda bb, i: (bb, 0, 0)))
    out_shape.append(jax.ShapeDtypeStruct((b, 1, e), jnp.float32))
    kern = functools.partial(_rg_scan_kernel, reverse=reverse, add_prev=add_prev, write_seq=write_seq,
                             out_alpha=out_alpha, tm=tm, gw=gw, ngroups=ngroups)
    res = pl.pallas_call(
        kern,
        grid=(b, nchunk),
        in_specs=in_specs,
        out_specs=out_specs,
        out_shape=out_shape,
        scratch_shapes=scratch,
        compiler_params=_params("parallel", "arbitrary"),
        name="rg_scan_bwd" if reverse else "rg_scan_fwd",
    )(*args)
    if out is not None:
        return res[0].reshape(x.shape), res[1]
    if write_seq:
        return res[0], res[1]
    return None, res[0]


def _gate_weights(w_r, w_i):
    nb, bw, _ = w_r.shape
    g = RG_GATE_GROUP
    ng = nb // g
    eye = jnp.eye(g, dtype=w_r.dtype)

    def bd(w):
        w = w.reshape(ng, g, bw, bw)
        return jnp.einsum("nhij,hk->nhikj", w, eye).reshape(ng, g * bw, g * bw)

    return (0.5 * jnp.concatenate([bd(w_r), bd(w_i)], axis=-1)).astype(jnp.bfloat16)


def _outproj_tile(y, g, w_ref, x_ref, gate_ref, lg_ref, lb_ref, o_ref, alpha):
    d = w_ref.shape[1]
    rows = o_ref.shape[1]
    chunk = min(rows, OUT_ROWS)
    for j in range(o_ref.shape[2] // d):
        cols = slice(j * d, (j + 1) * d)
        for r0 in range(0, rows, chunk):
            tok = slice(j * rows + r0, j * rows + r0 + chunk)
            hg = 0.5 * g[tok].astype(jnp.float32)
            z = (y[tok].astype(jnp.float32) * (hg * (jnp.tanh(hg) + 1.0))).astype(jnp.bfloat16)
            out = jnp.dot(z, w_ref[...], preferred_element_type=jnp.float32)
            r = alpha * x_ref[0, r0:r0 + chunk, cols] + gate_ref[0] * out
            mu = jnp.mean(r, axis=-1, keepdims=True)
            cen = r - mu
            var = jnp.mean(cen * cen, axis=-1, keepdims=True)
            o_ref[0, r0:r0 + chunk, cols] = cen * lax.rsqrt(var + LN_EPS) * lg_ref[...] + lb_ref[...]


def _outproj_kernel(y_ref, g_ref, w_ref, x_ref, gate_ref, lg_ref, lb_ref, o_ref, *, alpha):
    _outproj_tile(y_ref[0], g_ref[0, 0], w_ref, x_ref, gate_ref, lg_ref, lb_ref, o_ref, alpha)


def _outproj(y, proj, g_group, w_bf16, x, gate, ln_g, ln_b, *, alpha, col_major):
    b, l, e = y.shape
    d = x.shape[-1]
    tm = _tile_rows(l, col_major)
    ntile = l // tm
    main, _, _ = _stream_specs(l, d, tm, col_major, 0)
    xv = _stream_view(x, col_major)
    out = pl.pallas_call(
        functools.partial(_outproj_kernel, alpha=alpha),
        grid=(b, ntile),
        in_specs=[pl.BlockSpec((1, tm, e), lambda bb, i: (bb, i, 0)),
                  pl.BlockSpec((1, 1, tm, e), lambda bb, i: (g_group, bb, i, 0)),
                  pl.BlockSpec((e, d), lambda bb, i: (0, 0)),
                  main,
                  pl.BlockSpec((1, 1, d), lambda bb, i: (bb, 0, 0)),
                  pl.BlockSpec((1, d), lambda bb, i: (0, 0)),
                  pl.BlockSpec((1, d), lambda bb, i: (0, 0))],
        out_specs=main,
        out_shape=jax.ShapeDtypeStruct(xv.shape, jnp.float32),
        compiler_params=_params("parallel", "parallel"),
        name="outproj_ln",
    )(y, proj, w_bf16, xv, gate, ln_g.reshape(1, d), ln_b.reshape(1, d))
    return out.reshape(x.shape)


def _filter_positions(l, fh):
    f32 = jnp.float32
    t = jnp.linspace(0.0, 1.0, l, dtype=f32)[:, None]
    bands = jnp.linspace(1e-4, HY_EMB_BANDS - 1, HY_EMB_BANDS, dtype=f32)
    w = (2.0 * math.pi) * jnp.arange(l, dtype=f32)[:, None] / l
    z = jnp.concatenate([t, jnp.cos(bands * w), -jnp.sin(bands * w)], axis=-1)
    z_rev = jnp.concatenate([z[:1], z[:0:-1]], axis=0)
    pad = lambda a: jnp.pad(a, ((0, 0), (0, fh - a.shape[1])))
    return jnp.concatenate([pad(z), pad(z_rev)], axis=1), jnp.stack([z[:, 0:1], z_rev[:, 0:1]])


def _hidden_kernel(z_ref, w1_ref, b1_ref, w2_ref, b2_ref, fr_ref, o_ref):
    fr = fr_ref[...]
    h = jnp.dot(z_ref[...], w1_ref[...], preferred_element_type=jnp.float32, precision=_HIGHEST)
    h = jnp.sin(fr * (h + b1_ref[...]))
    h = jnp.dot(h, w2_ref[...], preferred_element_type=jnp.float32, precision=_HIGHEST)
    o_ref[...] = jnp.sin(fr * (h + b2_ref[...]))


def _filter_hidden(zz, w1, b1, w2, b2, freq):
    n, zp = zz.shape
    fh = w1.shape[1]
    tr = min(n, 1024)
    both = lambda v: jnp.concatenate([v, v]).reshape(1, 2 * fh)
    blockdiag = lambda w: jnp.kron(jnp.eye(2, dtype=w.dtype), w)
    w1p = blockdiag(jnp.pad(w1, ((0, fh - w1.shape[0]), (0, 0))))
    full = lambda shape: pl.BlockSpec(shape, lambda i: (0,) * len(shape))
    return pl.pallas_call(
        _hidden_kernel,
        grid=(n // tr,),
        in_specs=[pl.BlockSpec((tr, zp), lambda i: (i, 0)), full((zp, zp)), full((1, zp)),
                  full((zp, zp)), full((1, zp)), full((1, zp))],
        out_specs=pl.BlockSpec((tr, zp), lambda i: (i, 0)),
        out_shape=jax.ShapeDtypeStruct((n, zp), jnp.float32),
        compiler_params=_params("parallel"),
        name="filter_hidden",
    )(zz, w1p, both(b1), blockdiag(w2), both(b2), both(freq))


def _split_bf16(a):
    hi = a.astype(jnp.bfloat16)
    return hi, (a - hi.astype(jnp.float32)).astype(jnp.bfloat16)


def _filter_kernel(hid_ref, t_ref, w3_ref, ad_ref, k_ref, s_ref, *, tr, n_total):
    d = pl.program_id(1)
    i = pl.program_id(2)
    last = (d == pl.num_programs(1) - 1) & (i == pl.num_programs(2) - 1)
    hi, lo = _split_bf16(hid_ref[...])
    h = jnp.dot(jnp.concatenate([hi, lo, hi], axis=1), w3_ref[0, 0], preferred_element_type=jnp.float32)
    window = jnp.exp(-t_ref[0] * ad_ref[...]) + HY_MOD_SHIFT
    row = lax.broadcasted_iota(jnp.int32, h.shape, 0)
    gap = (row == 0) & (d == 1) & (i == 0)
    k = jnp.where(gap, 0.0, h * window)
    k_ref[0] = k

    @pl.when((d == 0) & (i == 0))
    def _():
        s_ref[...] = jnp.zeros_like(s_ref)

    s_ref[0] += jnp.sum(jnp.abs(k).reshape(tr // SUBLANES, SUBLANES, -1), axis=0)

    @pl.when(last)
    def _():
        tot = jnp.sum(s_ref[0], axis=0, keepdims=True)
        s_ref[0] = jnp.broadcast_to(1.0 / (tot * n_total), s_ref.shape[1:])


def _filter_weights(w3, fh, e, compact):
    order = w3.shape[1] // (2 * e)
    max_decay = math.log(HY_DECAY_TARGET) / HY_FAST_DECAY_PCT
    min_decay = math.log(HY_DECAY_TARGET) / HY_SLOW_DECAY_PCT
    absdelta = jnp.abs(jnp.linspace(min_decay, max_decay, e, dtype=jnp.float32)).reshape(1, e)
    w3 = w3.reshape(fh, order, 2, e).transpose(1, 2, 0, 3)
    if compact:
        w_hi, w_lo = _split_bf16(w3)
        return jnp.concatenate([w_hi, w_hi, w_lo, jnp.zeros_like(w_hi)], axis=2), absdelta
    w3 = jnp.stack([jnp.pad(w3[:, 0], ((0, 0), (0, fh), (0, 0))),
                    jnp.pad(w3[:, 1], ((0, 0), (fh, 0), (0, 0)))], axis=1)
    w_hi, w_lo = _split_bf16(w3)
    return jnp.concatenate([w_hi, w_hi, w_lo], axis=2), absdelta


def _filter_fft_a_kernel(hid_ref, t_ref, w3_ref, ad_ref, m_ref, re_ref, im_ref, s_ref, *, n_total):
    j = pl.program_id(2)
    k1h, rows, fh2 = hid_ref.shape
    ec = ad_ref.shape[1]
    hi, lo = _split_bf16(hid_ref[...].reshape(k1h * rows, fh2))
    fh = fh2 // 2
    stacked = [jnp.concatenate([hi[:, own], lo[:, own], hi[:, own], jnp.zeros_like(hi[:, own])], axis=1)
               for own in (slice(0, fh), slice(fh, fh2))]
    tcol = [t_ref[d].reshape(k1h * rows, 1) for d in range(2)]
    half = m_ref.shape[0] // 2
    f1p = half // FFT_J

    @pl.when(j == 0)
    def _():
        s_ref[...] = jnp.zeros_like(s_ref)

    abs_sums = []
    for sl in _lane_chunks(ec):
        halves = []
        for d in range(2):
            h = jnp.dot(stacked[d], w3_ref[0, d, :, sl], preferred_element_type=jnp.float32)
            k = h * (jnp.exp(-tcol[d] * ad_ref[:, sl]) + HY_MOD_SHIFT)
            if d == 1:
                row = lax.broadcasted_iota(jnp.int32, k.shape, 0)
                k = jnp.where((row == 0) & (j == 0), 0.0, k)
            halves.append(k)
        k = jnp.concatenate(halves, axis=0)
        width = k.shape[1]
        abs_sums.append(jnp.sum(jnp.abs(k).reshape(-1, SUBLANES, width), axis=0))
        k3 = k.reshape(2 * k1h, rows, width)
        re_parts, im_parts = [], []
        for h in range(rows // FFT_J):
            x = k3[:, h * FFT_J:(h + 1) * FFT_J, :].reshape(2 * k1h * FFT_J, width)
            r = jnp.dot(m_ref[...], x.astype(jnp.bfloat16), preferred_element_type=jnp.float32)
            re_parts.append(r[:half].reshape(f1p, FFT_J, width))
            im_parts.append(r[half:].reshape(f1p, FFT_J, width))
        re_ref[0, :, :, sl] = jnp.concatenate(re_parts, axis=1).astype(jnp.bfloat16)
        im_ref[0, :, :, sl] = jnp.concatenate(im_parts, axis=1).astype(jnp.bfloat16)
    s_ref[0] += jnp.concatenate(abs_sums, axis=1)

    @pl.when(j == pl.num_programs(2) - 1)
    def _():
        tot = jnp.sum(s_ref[0], axis=0, keepdims=True)
        s_ref[0] = jnp.broadcast_to(1.0 / (tot * n_total), s_ref.shape[1:])


def _filter_fft_a(hid, tcol, w3, e, tabs):
    l, fh2 = hid.shape
    n2 = FFT_N2
    k1h = l // n2
    f1p = tabs["f1p"]
    mat = tabs["a_full"]
    w_stack, absdelta = _filter_weights(w3, fh2 // 2, e, compact=True)
    order = w_stack.shape[0]
    ec = min(e, FFT_EC)
    out_spec = pl.BlockSpec((1, f1p, BF16_ROWS, ec), lambda o, cc, j: (o, 0, j, cc))
    shape = jax.ShapeDtypeStruct((order, f1p, n2, e), jnp.bfloat16)
    return pl.pallas_call(
        functools.partial(_filter_fft_a_kernel, n_total=float(2 * l)),
        grid=(order, e // ec, n2 // BF16_ROWS),
        in_specs=[pl.BlockSpec((k1h, BF16_ROWS, fh2), lambda o, cc, j: (0, j, 0)),
                  pl.BlockSpec((2, k1h, BF16_ROWS, 1), lambda o, cc, j: (0, 0, j, 0)),
                  pl.BlockSpec((1, 2, w_stack.shape[2], ec), lambda o, cc, j: (o, 0, 0, cc)),
                  pl.BlockSpec((1, ec), lambda o, cc, j: (0, cc)),
                  pl.BlockSpec(mat.shape, lambda o, cc, j: (0, 0))],
        out_specs=[out_spec, out_spec,
                   pl.BlockSpec((1, SUBLANES, ec), lambda o, cc, j: (o, 0, cc))],
        out_shape=[shape, shape, jax.ShapeDtypeStruct((order, SUBLANES, e), jnp.float32)],
        compiler_params=_params("parallel", "parallel", "arbitrary"),
        name="filter_fft_a",
    )(hid.reshape(k1h, n2, fh2), tcol.reshape(2, k1h, n2, 1), w_stack, absdelta, mat)


def _filters(hid, tcol, w3, e):
    l, fh2 = hid.shape
    n = 2 * l
    tr = min(l, 512)
    nl = l // tr
    w_stack, absdelta = _filter_weights(w3, fh2 // 2, e, compact=False)
    order = w_stack.shape[0]
    return pl.pallas_call(
        functools.partial(_filter_kernel, tr=tr, n_total=float(n)),
        grid=(order, 2, nl),
        in_specs=[pl.BlockSpec((tr, fh2), lambda o, d, i: (i, 0)),
                  pl.BlockSpec((1, tr, 1), lambda o, d, i: (d, i, 0)),
                  pl.BlockSpec((1, 1, 3 * fh2, e), lambda o, d, i: (o, d, 0, 0)),
                  pl.BlockSpec((1, e), lambda o, d, i: (0, 0))],
        out_specs=[pl.BlockSpec((1, tr, e), lambda o, d, i: (o, d * nl + i, 0)),
                   pl.BlockSpec((1, SUBLANES, e), lambda o, d, i: (o, 0, 0))],
        out_shape=[jax.ShapeDtypeStruct((order, n, e), jnp.float32),
                   jax.ShapeDtypeStruct((order, SUBLANES, e), jnp.float32)],
        compiler_params=_params("parallel", "arbitrary", "arbitrary"),
        name="filter_gen",
    )(hid, tcol, w_stack, absdelta)


def _angles(num, den):
    ang = 2.0 * np.pi * (np.asarray(num, np.int64) % den).astype(np.float64) / den
    return np.cos(ang), np.sin(ang)


@functools.lru_cache(maxsize=None)
def _fft_tables(n1):
    n2 = FFT_N2
    n = n1 * n2
    f1 = n1 // 2 + 1
    f1p = _round_up(f1, 2)
    fr = np.arange(f1)[:, None]
    eye = np.eye(FFT_J)

    def stage_a(k1):
        c, s = _angles(fr * np.arange(k1)[None, :], n1)
        m = np.zeros((2 * f1p, k1))
        m[:f1], m[f1p:f1p + f1] = c, -s
        return np.kron(m, eye)

    nyq = n1 // 2
    wgt = np.full((nyq,), 2.0)
    wgt[0] = 1.0
    c, s = _angles(np.arange(n1 // 2)[:, None] * np.arange(nyq)[None, :], n1)
    m_c = np.concatenate([c * wgt, -s * wgt], axis=1)

    c2, s2 = _angles(np.arange(n2)[:, None] * np.arange(n2)[None, :], n2)
    cphi, sphi = _angles(fr * np.arange(n2)[None, :], n)
    phase = np.stack([cphi, sphi], axis=1)
    bf = lambda a: jnp.asarray(a, jnp.float32).astype(jnp.bfloat16)
    f32 = lambda a: jnp.asarray(a, jnp.float32)
    return dict(f1=f1, f1p=f1p, a_data=bf(stage_a(n1 // 2)), a_full=bf(stage_a(n1)),
                m_c=bf(np.kron(m_c, eye)), base_c=f32(c2), base_s=f32(s2),
                phase_row=f32(phase[:, :, None, :]), phase_col=f32(phase[..., None]))


def _twiddled_dft_kernel(c_ref, s_ref, row_ref, col_ref, mf_ref, mi_ref):
    c, s = c_ref[...], s_ref[...]
    n2 = c.shape[0]
    bf = jnp.bfloat16
    cr, sr = row_ref[0, 0], row_ref[0, 1]
    cf, sf = (c * cr - s * sr).astype(bf), (s * cr + c * sr).astype(bf)
    mf_ref[0, :n2, :n2] = cf
    mf_ref[0, :n2, n2:] = sf
    mf_ref[0, n2:, :n2] = -sf
    mf_ref[0, n2:, n2:] = cf
    cc, sc = col_ref[0, 0], col_ref[0, 1]
    ci, si = (c * cc - s * sc).astype(bf), (s * cc + c * sc).astype(bf)
    mi_ref[0, :n2, :n2] = ci
    mi_ref[0, :n2, n2:] = -si
    mi_ref[0, n2:, :n2] = si
    mi_ref[0, n2:, n2:] = ci


def _twiddled_dfts(tabs):
    f1 = tabs["f1"]
    n2 = FFT_N2
    shape = jax.ShapeDtypeStruct((f1, 2 * n2, 2 * n2), jnp.bfloat16)
    base = pl.BlockSpec((n2, n2), lambda f: (0, 0))
    out = pl.BlockSpec((1, 2 * n2, 2 * n2), lambda f: (f, 0, 0))
    return pl.pallas_call(
        _twiddled_dft_kernel,
        grid=(f1,),
        in_specs=[base, base, pl.BlockSpec((1, 2, 1, n2), lambda f: (f, 0, 0, 0)),
                  pl.BlockSpec((1, 2, n2, 1), lambda f: (f, 0, 0, 0))],
        out_specs=[out, out],
        out_shape=[shape, shape],
        compiler_params=_params("parallel"),
        name="twiddled_dft_tables",
    )(tabs["base_c"], tabs["base_s"], tabs["phase_row"], tabs["phase_col"])


def _fft_a_kernel(x_ref, m_ref, re_ref, im_ref):
    k1, ec = x_ref.shape[2], x_ref.shape[4]
    half = m_ref.shape[0] // 2
    f1p = half // FFT_J
    re_parts, im_parts = [], []
    xf = x_ref[0, 0].astype(jnp.float32)
    for h in range(x_ref.shape[3] // FFT_J):
        x = xf[:, h * FFT_J:(h + 1) * FFT_J, :].reshape(k1 * FFT_J, ec)
        r = jnp.dot(m_ref[...], x.astype(jnp.bfloat16), preferred_element_type=jnp.float32)
        re_parts.append(r[:half].reshape(f1p, FFT_J, ec))
        im_parts.append(r[half:].reshape(f1p, FFT_J, ec))
    re_ref[0] = jnp.concatenate(re_parts, axis=1).astype(jnp.bfloat16)
    im_ref[0] = jnp.concatenate(im_parts, axis=1).astype(jnp.bfloat16)


def _fft_stage_a(arr, group, tabs):
    _, bt, rows, e = arr.shape
    k1 = rows // FFT_N2
    f1p = tabs["f1p"]
    mat = tabs["a_data"]
    assert mat.shape[1] == k1 * FFT_J
    ec = min(e, FFT_EC)
    av = arr.reshape(arr.shape[0], bt, k1, FFT_N2, e)
    out_spec = pl.BlockSpec((1, f1p, BF16_ROWS, ec), lambda bb, j, cc: (bb, 0, j, cc))
    shape = jax.ShapeDtypeStruct((bt, f1p, FFT_N2, e), jnp.bfloat16)
    return pl.pallas_call(
        _fft_a_kernel,
        grid=(bt, FFT_N2 // BF16_ROWS, e // ec),
        in_specs=[pl.BlockSpec((1, 1, k1, BF16_ROWS, ec), lambda bb, j, cc: (group, bb, 0, j, cc)),
                  pl.BlockSpec(mat.shape, lambda bb, j, cc: (0, 0))],
        out_specs=[out_spec, out_spec],
        out_shape=[shape, shape],
        compiler_params=_params("parallel", "parallel", "parallel"),
        name="fft_stage_a",
    )(av, mat)


def _lane_chunks(e, width=256):
    return [slice(c0, min(c0 + width, e)) for c0 in range(0, e, width)]


def _fft_b_filter_kernel(re_ref, im_ref, sc_ref, mf_ref, kre_ref, kim_ref):
    for b in range(re_ref.shape[0]):
        for sl in _lane_chunks(re_ref.shape[3]):
            x = jnp.concatenate([re_ref[b, 0, :, sl], im_ref[b, 0, :, sl]], axis=0)
            u = jnp.dot(mf_ref[0], x, preferred_element_type=jnp.float32) * sc_ref[b, 0:1, sl]
            kre_ref[b, 0, :, sl] = u[:FFT_N2].astype(kre_ref.dtype)
            kim_ref[b, 0, :, sl] = u[FFT_N2:].astype(kim_ref.dtype)


def _fft_b_conv_kernel(re_ref, im_ref, kre_ref, kim_ref, mf_ref, mi_ref, ore_ref, oim_ref):
    for sl in _lane_chunks(re_ref.shape[3]):
        k_re = kre_ref[0, 0, :, sl].astype(jnp.float32)
        k_im = kim_ref[0, 0, :, sl].astype(jnp.float32)
        for b in range(re_ref.shape[0]):
            x = jnp.concatenate([re_ref[b, 0, :, sl], im_ref[b, 0, :, sl]], axis=0)
            u = jnp.dot(mf_ref[0], x, preferred_element_type=jnp.float32)
            u_re, u_im = u[:FFT_N2], u[FFT_N2:]
            p = jnp.concatenate([u_re * k_re - u_im * k_im, u_re * k_im + u_im * k_re], axis=0)
            v = jnp.dot(mi_ref[0], p.astype(jnp.bfloat16), preferred_element_type=jnp.float32)
            ore_ref[b, 0, :, sl] = v[:FFT_N2].astype(jnp.bfloat16)
            oim_ref[b, 0, :, sl] = v[FFT_N2:].astype(jnp.bfloat16)


def _fft_stage_b_filter(a_re, a_im, inv_s, mf, tabs):
    bt, _, n2, e = a_re.shape
    f1 = tabs["f1"]
    blk = pl.BlockSpec((bt, 1, n2, e), lambda f: (0, f, 0, 0))
    shape = jax.ShapeDtypeStruct((bt, f1, n2, e), jnp.bfloat16)
    return pl.pallas_call(
        _fft_b_filter_kernel,
        grid=(f1,),
        in_specs=[blk, blk, pl.BlockSpec(inv_s.shape, lambda f: (0, 0, 0)),
                  pl.BlockSpec((1, 2 * n2, 2 * n2), lambda f: (f, 0, 0))],
        out_specs=[blk, blk],
        out_shape=[shape, shape],
        compiler_params=_params("parallel"),
        name="fft_stage_b_filter",
    )(a_re, a_im, inv_s, mf)


def _fft_stage_b_conv(a_re, a_im, kf_re, kf_im, order_idx, mf, mi, tabs):
    bt, f1p, n2, e = a_re.shape
    f1 = tabs["f1"]
    fc = lambda f: jnp.minimum(f, f1 - 1)
    blk = pl.BlockSpec((bt, 1, n2, e), lambda f: (0, f, 0, 0))
    kblk = pl.BlockSpec((1, 1, n2, e), lambda f: (order_idx, fc(f), 0, 0))
    wblk = pl.BlockSpec((1, 2 * n2, 2 * n2), lambda f: (fc(f), 0, 0))
    shape = jax.ShapeDtypeStruct((bt, f1p, n2, e), jnp.bfloat16)
    return pl.pallas_call(
        _fft_b_conv_kernel,
        grid=(f1p,),
        in_specs=[blk, blk, kblk, kblk, wblk, wblk],
        out_specs=[blk, blk],
        out_shape=[shape, shape],
        compiler_params=_params("parallel"),
        name="fft_stage_b_conv",
    )(a_re, a_im, kf_re, kf_im, mf, mi)


def _fft_c_kernel(*refs, chain):
    re_ref, im_ref, m_ref, u_ref, mul_ref, d_ref = refs[:6]
    if chain:
        ma_ref, o_ref, are_ref, aim_ref = refs[6:]
    else:
        o_ref, = refs[6:]
    k1, ec = o_ref.shape[1], o_ref.shape[3]
    f1p = re_ref.shape[1]
    nyq = m_ref.shape[1] // (2 * FFT_J)
    re = re_ref[0].astype(jnp.float32)
    im = im_ref[0].astype(jnp.float32)
    uf = u_ref[0, 0].astype(jnp.float32)
    mf = mul_ref[0, 0].astype(jnp.float32)
    parts, re_parts, im_parts = [], [], []
    for h in range(o_ref.shape[2] // FFT_J):
        sl = slice(h * FFT_J, (h + 1) * FFT_J)
        x = jnp.concatenate([re[:nyq, sl, :].reshape(nyq * FFT_J, ec),
                             im[:nyq, sl, :].reshape(nyq * FFT_J, ec)], axis=0)
        y = jnp.dot(m_ref[...], x.astype(jnp.bfloat16), preferred_element_type=jnp.float32)
        r_nyq = re[nyq, sl, :]
        y = (y.reshape(k1 // 2, 2, FFT_J, ec) + jnp.stack([r_nyq, -r_nyq])[None]).reshape(k1 * FFT_J, ec)
        u = uf[:, sl, :].reshape(k1 * FFT_J, ec)
        mul = mf[:, sl, :].reshape(k1 * FFT_J, ec)
        z = (mul * (y + u * d_ref[...])).astype(jnp.bfloat16)
        parts.append(z.astype(jnp.float32).reshape(k1, FFT_J, ec))
        if chain:
            r = jnp.dot(ma_ref[...], z, preferred_element_type=jnp.float32)
            half = ma_ref.shape[0] // 2
            re_parts.append(r[:half].reshape(f1p, FFT_J, ec))
            im_parts.append(r[half:].reshape(f1p, FFT_J, ec))
    o_ref[0] = jnp.concatenate(parts, axis=1).astype(o_ref.dtype)
    if chain:
        are_ref[0] = jnp.concatenate(re_parts, axis=1).astype(jnp.bfloat16)
        aim_ref[0] = jnp.concatenate(im_parts, axis=1).astype(jnp.bfloat16)


def _fft_stage_c(b_re, b_im, u_arr, u_group, proj, mul_group, d_row, tabs, chain):
    bt, f1p, n2, e = b_re.shape
    l = u_arr.shape[2]
    k1 = l // n2
    ec = min(e, FFT_EC)
    uv = u_arr.reshape(u_arr.shape[0], bt, k1, n2, e)
    pv = proj.reshape(proj.shape[0], bt, k1, n2, e)
    mat = tabs["m_c"]
    fblk = pl.BlockSpec((1, f1p, BF16_ROWS, ec), lambda bb, j, cc: (bb, 0, j, cc))
    full = lambda m: pl.BlockSpec(m.shape, lambda bb, j, cc: (0, 0))
    in_specs = [fblk, fblk, full(mat),
                pl.BlockSpec((1, 1, k1, BF16_ROWS, ec), lambda bb, j, cc: (u_group, bb, 0, j, cc)),
                pl.BlockSpec((1, 1, k1, BF16_ROWS, ec), lambda bb, j, cc: (mul_group, bb, 0, j, cc)),
                pl.BlockSpec((1, ec), lambda bb, j, cc: (0, cc))]
    args = [b_re, b_im, mat, uv, pv, d_row]
    out_specs = [pl.BlockSpec((1, k1, BF16_ROWS, ec), lambda bb, j, cc: (bb, 0, j, cc))]
    out_shape = [jax.ShapeDtypeStruct((bt, k1, n2, e), jnp.bfloat16)]
    if chain:
        in_specs.append(full(tabs["a_data"]))
        args.append(tabs["a_data"])
        out_specs += [fblk, fblk]
        out_shape += [jax.ShapeDtypeStruct((bt, f1p, n2, e), jnp.bfloat16)] * 2
    res = pl.pallas_call(
        functools.partial(_fft_c_kernel, chain=chain),
        grid=(bt, n2 // BF16_ROWS, e // ec),
        in_specs=in_specs,
        out_specs=out_specs,
        out_shape=out_shape,
        compiler_params=_params("parallel", "parallel", "parallel"),
        name="fft_stage_ca" if chain else "fft_stage_c",
    )(*args)
    return (res[0].reshape(bt, l, e),) + tuple(res[1:])


def _hyena_long(proj, f_re, f_im, inv_s, d_bias, tabs, mf, mi):
    order, e = d_bias.shape
    kf_re, kf_im = _fft_stage_b_filter(f_re, f_im, inv_s, mf, tabs)
    u_arr, u_group = proj, 0
    a_re, a_im = _fft_stage_a(u_arr, u_group, tabs)
    for o in range(order):
        b_re, b_im = _fft_stage_b_conv(a_re, a_im, kf_re, kf_im, o, mf, mi, tabs)
        chain = o + 1 < order
        res = _fft_stage_c(b_re, b_im, u_arr, u_group, proj, o + 1, d_bias[o].reshape(1, e), tabs, chain)
        if chain:
            a_re, a_im = res[1:]
        u_arr, u_group = res[0][None], 0
    return res[0]


@functools.lru_cache(maxsize=None)
def _dft_tables(l):
    n = 2 * l
    nf = l + 1
    nfp = _round_up(nf, BF16_ROWS)
    c, s = _angles(np.arange(nf)[:, None] * np.arange(n)[None, :], n)
    fwd = np.zeros((2 * nfp, n))
    fwd[:nf], fwd[nfp:nfp + nf] = c, -s
    wgt = np.full((nf,), 2.0)
    wgt[0] = wgt[-1] = 1.0
    inv = np.zeros((l, 2 * nfp))
    inv[:, :nf], inv[:, nfp:nfp + nf] = (c[:, :l] * wgt[:, None]).T, (-s[:, :l] * wgt[:, None]).T
    bf = lambda a: jnp.asarray(a, jnp.float32).astype(jnp.bfloat16)
    return nfp, bf(fwd), bf(inv)


def _hyena_short_kernel(p_ref, k_ref, s_ref, d_ref, fwd_ref, inv_ref, o_ref, *, l, nfp, order):
    fwd = fwd_ref[...]
    u = p_ref[0, 0]
    for o in range(order):
        kn = (k_ref[o] * s_ref[o, 0:1, :]).astype(jnp.bfloat16)
        kf = jnp.dot(fwd, kn, preferred_element_type=jnp.float32)
        uf = jnp.dot(fwd[:, :l], u.astype(jnp.bfloat16), preferred_element_type=jnp.float32)
        k_re, k_im, u_re, u_im = kf[:nfp], kf[nfp:], uf[:nfp], uf[nfp:]
        p = jnp.concatenate([u_re * k_re - u_im * k_im, u_re * k_im + u_im * k_re], axis=0)
        y = jnp.dot(inv_ref[...], p.astype(jnp.bfloat16), preferred_element_type=jnp.float32)
        u = p_ref[o + 1, 0] * (y + u * d_ref[o:o + 1, :])
    o_ref[0] = u


def _hyena_short(proj, k, inv_s, d_bias, e):
    nt, b, l, _ = proj.shape
    order = k.shape[0]
    nfp, fwd, inv = _dft_tables(l)
    return pl.pallas_call(
        functools.partial(_hyena_short_kernel, l=l, nfp=nfp, order=order),
        grid=(b,),
        in_specs=[pl.BlockSpec((nt, 1, l, e), lambda bb: (0, bb, 0, 0)),
                  pl.BlockSpec((order, 2 * l, e), lambda bb: (0, 0, 0)),
                  pl.BlockSpec((order, SUBLANES, e), lambda bb: (0, 0, 0)),
                  pl.BlockSpec((order, e), lambda bb: (0, 0)),
                  pl.BlockSpec(fwd.shape, lambda bb: (0, 0)),
                  pl.BlockSpec(inv.shape, lambda bb: (0, 0))],
        out_specs=pl.BlockSpec((1, l, e), lambda bb: (bb, 0, 0)),
        out_shape=jax.ShapeDtypeStruct((b, l, e), jnp.float32),
        compiler_params=_params("parallel"),
        name="hyena_short",
    )(proj, k, inv_s, d_bias, fwd, inv)


def kernel(x, c, ctx, c_ctx, w_mod, b_mod, ln_g, ln_b, rg_w_in, rg_conv_w, rg_conv_b, rg_w_r, rg_b_r, rg_w_i, rg_b_i, rg_lambda, rg_w_out, hy_w_in, hy_conv_w, hy_conv_b, hy_f_w1, hy_f_b1, hy_f_w2, hy_f_b2, hy_f_w3, hy_f_freq, hy_d, hy_w_out):
    bsz, seq, d = x.shape
    depth = w_mod.shape[0]
    e = rg_w_out.shape[1]
    alpha = (2 * depth) ** 0.25
    assert bsz + 1 <= SUBLANES

    cvec = jnp.zeros((SUBLANES, d), jnp.float32).at[:bsz].set(c).at[bsz].set(c_ctx)
    mod = _modulation(cvec, w_mod, b_mod)

    def mod_rows(layer, part, for_ctx):
        v = mod[layer, :, part * d:(part + 1) * d]
        v = jnp.broadcast_to(v[bsz], (bsz, d)) if for_ctx else v[:bsz]
        return v.reshape(bsz, 1, d)

    fft = None

    def filter_hidden(occ, l):
        zz, tcol = _filter_positions(l, hy_f_w1.shape[-1])
        hid = _filter_hidden(zz, hy_f_w1[occ], hy_f_b1[occ], hy_f_w2[occ], hy_f_b2[occ], hy_f_freq[occ])
        return hid, tcol

    for i in range(depth):
        kind = i % N_MIXERS
        occ = i // N_MIXERS
        need_ctx_out = any(j % N_MIXERS == 0 for j in range(i + 1, depth))
        col_major = occ % 2 == 1
        use_ctx = kind == 0 or need_ctx_out
        shift, scale, gate = (mod_rows(i, p, False) for p in range(3))
        if use_ctx:
            shift_c, scale_c, gate_c = (mod_rows(i, p, True) for p in range(3))

        if kind == 0:
            w_in = rg_w_in[occ].astype(jnp.bfloat16)
            w_out = rg_w_out[occ].astype(jnp.bfloat16)
            proj = _inproj(x, shift, scale, w_in, rg_conv_w[occ], rg_conv_b[occ],
                           e=e, pad_l=1, col_major=col_major)
            w_in_c = w_in if need_ctx_out else w_in[:, :e]
            proj_c = _inproj(ctx, shift_c, scale_c, w_in_c, rg_conv_w[occ], rg_conv_b[occ],
                             e=e, pad_l=1, col_major=False)
            zeros = jnp.zeros((bsz, 1, e), jnp.float32)
            gates = [(_gate_weights(rg_w_r[occ, dd], rg_w_i[occ, dd]), rg_b_r[occ, dd], rg_b_i[occ, dd],
                      rg_lambda[occ, dd]) for dd in range(2)]
            hc_f, st_f = _rg_scan(proj_c, zeros, *gates[0], reverse=False, write_seq=need_ctx_out)
            h_f, _ = _rg_scan(proj, st_f, *gates[0], reverse=False)
            y_c, st_b = _rg_scan(proj_c, zeros, *gates[1], reverse=True, prev=hc_f, write_seq=need_ctx_out)
            x_new, _ = _rg_scan(proj, st_b, *gates[1], reverse=True, prev=h_f,
                                out=(w_out, x, gate, ln_g[i], ln_b[i], alpha, col_major))
            g_group = 1
        else:
            w_in = hy_w_in[occ].astype(jnp.bfloat16)
            w_out = hy_w_out[occ].astype(jnp.bfloat16)
            proj = _inproj(x, shift, scale, w_in, hy_conv_w[occ], hy_conv_b[occ],
                           e=e, pad_l=1, col_major=col_major)
            if fft is None:
                tabs = _fft_tables(2 * seq // FFT_N2)
                fft = (tabs,) + tuple(_twiddled_dfts(tabs))
            f_re, f_im, inv_s = _filter_fft_a(*filter_hidden(occ, seq), hy_f_w3[occ], e, fft[0])
            y = _hyena_long(proj, f_re, f_im, inv_s, hy_d[occ], *fft)
            if need_ctx_out:
                proj_c = _inproj(ctx, shift_c, scale_c, w_in, hy_conv_w[occ], hy_conv_b[occ],
                                 e=e, pad_l=1, col_major=False)
                k_c, inv_s_c = _filters(*filter_hidden(occ, ctx.shape[1]), hy_f_w3[occ], e)
                y_c = _hyena_short(proj_c, k_c, inv_s_c, hy_d[occ], e)
            g_group = 3
            x_new = _outproj(y, proj, g_group, w_out, x, gate, ln_g[i], ln_b[i],
                             alpha=alpha, col_major=col_major)

        x = x_new
        if need_ctx_out:
            ctx = _outproj(y_c, proj_c, g_group, w_out, ctx, gate_c, ln_g[i], ln_b[i],
                           alpha=alpha, col_major=False)
    return x
```

```python
import functools
import math

import numpy as np
import jax
import jax.numpy as jnp
from jax import lax
from jax.experimental import pallas as pl
from jax.experimental.pallas import tpu as pltpu

GRID_W = 64
N_MIXERS = 2
RG_C = 8.0
RG_GATE_GROUP = 4
HY_EMB_BANDS = 8
HY_FAST_DECAY_PCT = 0.3
HY_SLOW_DECAY_PCT = 1.5
HY_DECAY_TARGET = 1e-2
HY_MOD_SHIFT = 0.05
LN_EPS = 1e-5

SUBLANES = 8
LANES = 128
BF16_ROWS = 16
FFT_N2 = 256
FFT_J = SUBLANES
FFT_EC = 768
STREAM_TILE = 512
SCAN_CHUNK = 512
OUT_ROWS = 512
SCAN_SLABS = 4
LANE_CHUNK = 256
VMEM_LIMIT = 56 * 1024 * 1024

_HIGHEST = lax.Precision.HIGHEST


def _round_up(a, m):
    return (a + m - 1) // m * m


def _params(*sem):
    return pltpu.CompilerParams(dimension_semantics=sem, vmem_limit_bytes=VMEM_LIMIT)


def _sigmoid(v):
    return 1.0 / (1.0 + jnp.exp(-v))


def _mod_kernel(c_ref, w_ref, b_ref, o_ref):
    cv = c_ref[...]
    act = cv * _sigmoid(cv)
    o_ref[0] = jnp.dot(act, w_ref[0], preferred_element_type=jnp.float32,
                       precision=_HIGHEST) + b_ref[0]


def _modulation(cvec, w_mod, b_mod):
    depth, d, d3 = w_mod.shape
    nblk = d3 // d
    return pl.pallas_call(
        _mod_kernel,
        grid=(depth, nblk),
        in_specs=[pl.BlockSpec((SUBLANES, d), lambda l, n: (0, 0)),
                  pl.BlockSpec((1, d, d), lambda l, n: (l, 0, n)),
                  pl.BlockSpec((1, 1, d), lambda l, n: (l, 0, n))],
        out_specs=pl.BlockSpec((1, SUBLANES, d), lambda l, n: (l, 0, n)),
        out_shape=jax.ShapeDtypeStruct((depth, SUBLANES, d3), jnp.float32),
        compiler_params=_params("parallel", "parallel"),
        name="modulation",
    )(cvec, w_mod, b_mod.reshape(depth, 1, d3))


def _stream_view(x, col_major):
    b, l, d = x.shape
    if col_major:
        return x.reshape(b, l // GRID_W, GRID_W * d)
    return x


def _stream_specs(l, d, tm, col_major, nb_grid_prefix):
    def wrap(fn):
        return lambda *g: fn(*g[nb_grid_prefix:])
    if col_major:
        rows = l // GRID_W
        ncol = tm // rows
        assert tm == rows * ncol
        main = pl.BlockSpec((1, rows, ncol * d), wrap(lambda b, i: (b, 0, i)))
        prev = pl.BlockSpec((1, SUBLANES, d),
                            wrap(lambda b, i: (b, rows // SUBLANES - 1, jnp.maximum(i * ncol - 1, 0))))
        nxt = pl.BlockSpec((1, SUBLANES, d),
                           wrap(lambda b, i: (b, 0, jnp.minimum((i + 1) * ncol, GRID_W - 1))))
    else:
        per = tm // SUBLANES
        last = l // SUBLANES - 1
        main = pl.BlockSpec((1, tm, d), wrap(lambda b, i: (b, i, 0)))
        prev = pl.BlockSpec((1, SUBLANES, d), wrap(lambda b, i: (b, jnp.maximum(i * per - 1, 0), 0)))
        nxt = pl.BlockSpec((1, SUBLANES, d), wrap(lambda b, i: (b, jnp.minimum((i + 1) * per, last), 0)))
    return main, prev, nxt


def _tile_rows(l, col_major):
    tm = min(l, STREAM_TILE)
    if col_major:
        rows = l // GRID_W
        return max(tm // rows, 1) * rows
    return tm


def _tile_tokens(ref, d):
    blk = ref[0]
    ncol = blk.shape[1] // d
    if ncol == 1:
        return blk
    return jnp.concatenate([blk[:, j * d:(j + 1) * d] for j in range(ncol)], axis=0)


def _inproj_kernel(x_ref, xp_ref, xn_ref, sh_ref, sc_ref, w_ref, cw_ref, cb_ref, o_ref, u_scr,
                   *, n_conv, conv_k, pad_l, tm):
    i = pl.program_id(1)
    nt = pl.num_programs(1)
    d = w_ref.shape[0]
    e = o_ref.shape[3]
    scale = 1.0 + sc_ref[0]
    shift = sh_ref[0]
    mod = lambda v: v * scale + shift
    h_prev = jnp.where(i > 0, mod(xp_ref[0]), 0.0)
    h_next = jnp.where(i < nt - 1, mod(xn_ref[0]), 0.0)
    hh = jnp.concatenate([h_prev, mod(_tile_tokens(x_ref, d)), h_next], axis=0).astype(jnp.bfloat16)
    for n in range(o_ref.shape[0]):
        for c0 in range(0, e, LANE_CHUNK):
            sl = slice(c0, min(c0 + LANE_CHUNK, e))
            u = jnp.dot(hh, w_ref[:, n * e + sl.start:n * e + sl.stop], preferred_element_type=jnp.float32)
            if n >= n_conv:
                o_ref[n, 0, :, sl] = u[SUBLANES:SUBLANES + tm].astype(o_ref.dtype)
                continue
            for t0 in range(sl.start, sl.stop, LANES):
                slab = t0 // LANES
                u_scr[slab] = u[:, t0 - sl.start:t0 - sl.start + LANES]
                col = slice(n * e + t0, n * e + t0 + LANES)
                acc = cb_ref[:, col]
                for k in range(conv_k):
                    tap = u_scr[slab, pl.ds(SUBLANES - pad_l + k, tm, stride=1), :]
                    acc = acc + cw_ref[k:k + 1, col] * tap
                o_ref[n, 0, :, t0:t0 + LANES] = acc.astype(o_ref.dtype)


def _inproj(x, shift, scale, w_bf16, conv_w, conv_b, *, e, pad_l, col_major):
    b, l, d = x.shape
    nt_groups = w_bf16.shape[1] // e
    conv_k = conv_w.shape[0]
    n_conv = conv_w.shape[1] // e
    tm = _tile_rows(l, col_major)
    ntile = l // tm
    main, prev, nxt = _stream_specs(l, d, tm, col_major, 0)
    xv = _stream_view(x, col_major)
    kern = functools.partial(_inproj_kernel, n_conv=n_conv, conv_k=conv_k, pad_l=pad_l, tm=tm)
    const = lambda shape: pl.BlockSpec(shape, lambda bb, i: (0, 0), pipeline_mode=pl.Buffered(1))
    return pl.pallas_call(
        kern,
        grid=(b, ntile),
        in_specs=[main, prev, nxt,
                  pl.BlockSpec((1, 1, d), lambda bb, i: (bb, 0, 0)),
                  pl.BlockSpec((1, 1, d), lambda bb, i: (bb, 0, 0)),
                  const(w_bf16.shape), const(conv_w.shape), const((1, conv_w.shape[1]))],
        out_specs=pl.BlockSpec((nt_groups, 1, tm, e), lambda bb, i: (0, bb, i, 0)),
        out_shape=jax.ShapeDtypeStruct((nt_groups, b, l, e), jnp.bfloat16),
        scratch_shapes=[pltpu.VMEM((e // LANES, tm + 2 * SUBLANES, LANES), jnp.float32)],
        compiler_params=_params("parallel", "parallel"),
        name="inproj_conv",
    )(xv, xv, xv, shift, scale, w_bf16, conv_w, conv_b.reshape(1, -1))


def _rg_scan_kernel(*refs, reverse, add_prev, write_seq, out_alpha, tm, gw, ngroups):
    it = iter(refs)
    xc_ref = next(it)
    prev_ref = next(it) if add_prev else None
    h0_ref, wg_ref, br_ref, bi_ref, lam_ref = next(it), next(it), next(it), next(it), next(it)
    out_refs = [next(it) for _ in range(6)] if out_alpha is not None else None
    o_ref = next(it) if write_seq else None
    st_ref = next(it)
    a_scr, b_scr, carry = next(it), next(it), next(it)
    seq_ref = next(it) if out_alpha is not None else o_ref

    i = pl.program_id(1)
    nchunk = pl.num_programs(1)
    nslab, e = a_scr.shape[0], carry.shape[2]
    base = 0 if reverse else SUBLANES
    edge = tm if reverse else 0

    @pl.when(i == 0)
    def _():
        ones = jnp.ones((SUBLANES, e), jnp.float32)
        zeros = jnp.zeros((SUBLANES, e), jnp.float32)
        carry[0], carry[1], carry[2], carry[3] = ones, zeros, ones, zeros
        carry[4] = jnp.broadcast_to(h0_ref[0], (SUBLANES, e))
        a_scr[:, edge:edge + SUBLANES, :] = jnp.ones((nslab, SUBLANES, LANES), jnp.float32)
        b_scr[:, edge:edge + SUBLANES, :] = jnp.zeros((nslab, SUBLANES, LANES), jnp.float32)

    lam = lam_ref[...]
    neg = -lam
    softplus = jnp.maximum(neg, 0.0) + jnp.log(1.0 + jnp.exp(-jnp.abs(neg)))
    rate = (0.5 * RG_C) * softplus
    for g in range(ngroups):
        sl = slice(g * gw, (g + 1) * gw)
        xg = xc_ref[0, 0, :, sl]
        pre = jnp.dot(xg, wg_ref[g], preferred_element_type=jnp.float32)
        t_r = jnp.tanh(pre[:, :gw] + br_ref[:, sl])
        t_i = jnp.tanh(pre[:, gw:] + bi_ref[:, sl])
        neg_log_a = rate[:, sl] * t_r + rate[:, sl]
        a = jnp.exp(-neg_log_a)
        s = jnp.tanh(neg_log_a) * (a * a + 1.0)
        mult = jnp.where(s > 0.0, s * lax.rsqrt(s), 0.0)
        bx = (0.5 * mult) * ((t_i + 1.0) * xg.astype(jnp.float32))
        for j in range(gw // LANES):
            slab = g * (gw // LANES) + j
            a_scr[slab, base:base + tm, :] = a[:, j * LANES:(j + 1) * LANES]
            b_scr[slab, base:base + tm, :] = bx[:, j * LANES:(j + 1) * LANES]

    nblk = tm // BF16_ROWS
    step = 1 if reverse else -1
    per = min(nslab, SCAN_SLABS)
    row = lax.broadcasted_iota(jnp.int32, (SUBLANES, per * LANES), 0)

    def from_behind(cur, prev_rolled, dist):
        rolled = pltpu.roll(cur, (SUBLANES - dist) if reverse else dist, 0)
        inside = (row + dist < SUBLANES) if reverse else (row >= dist)
        return jnp.where(inside, rolled, prev_rolled), rolled

    for s0 in range(0, nslab, per):
        lanes = slice(s0 * LANES, (s0 + per) * LANES)

        def rows(ref, start):
            return jnp.concatenate([ref[s, pl.ds(start, SUBLANES, stride=1), :]
                                    for s in range(s0, s0 + per)], axis=1)

        def scan8(r0, st):
            ra1, rh1, ra2, rh2, h_prev = st
            a, b = rows(a_scr, base + r0), rows(b_scr, base + r0)
            a1 = a * rows(a_scr, base + r0 + step)
            h1 = b + a * rows(b_scr, base + r0 + step)
            a1s, ra1 = from_behind(a1, ra1, 2)
            h1s, rh1 = from_behind(h1, rh1, 2)
            a2, h2 = a1 * a1s, h1 + a1 * h1s
            a2s, ra2 = from_behind(a2, ra2, 4)
            h2s, rh2 = from_behind(h2, rh2, 4)
            h = (h2 + a2 * h2s) + (a2 * a2s) * h_prev
            return h, (ra1, rh1, ra2, rh2, h)

        def block(k, c):
            kk = (nblk - 1 - k) if reverse else k
            r0 = pl.multiple_of(kk * BF16_ROWS, BF16_ROWS)
            halves = [None, None]
            for hh in ((1, 0) if reverse else (0, 1)):
                halves[hh], c = scan8(r0 + hh * SUBLANES, c)
            if write_seq:
                h = jnp.concatenate(halves, axis=0)
                if add_prev:
                    h = h + prev_ref[0, pl.ds(r0, BF16_ROWS), lanes].astype(jnp.float32)
                if out_alpha is None:
                    seq_ref[0, pl.ds(r0, BF16_ROWS), lanes] = h.astype(seq_ref.dtype)
                else:
                    seq_ref[pl.ds(r0, BF16_ROWS), lanes] = h.astype(seq_ref.dtype)
            return c

        c_fin = lax.fori_loop(0, nblk, block, tuple(carry[j, :, lanes] for j in range(5)), unroll=4)
        for j in range(5):
            carry[j, :, lanes] = c_fin[j]

    last = base + (0 if reverse else tm - SUBLANES)
    a_scr[:, edge:edge + SUBLANES, :] = a_scr[:, last:last + SUBLANES, :]
    b_scr[:, edge:edge + SUBLANES, :] = b_scr[:, last:last + SUBLANES, :]

    @pl.when(i == nchunk - 1)
    def _():
        st_ref[0] = carry[4, 0:1, :] if reverse else carry[4, SUBLANES - 1:SUBLANES, :]

    if out_alpha is not None:
        g_ref, w_ref, x_ref, gate_ref, lg_ref, lb_ref = out_refs
        _outproj_tile(seq_ref[...], g_ref[0, 0], w_ref, x_ref, gate_ref, lg_ref, lb_ref, o_ref, out_alpha)


def _rg_scan(proj, h0, wg, b_r, b_i, lam, *, reverse, prev=None, write_seq=True, out=None):
    _, b, l, e = proj.shape
    tm = min(l, SCAN_CHUNK)
    nchunk = l // tm
    ngroups, gw, _ = wg.shape
    pos = (lambda i: nchunk - 1 - i) if reverse else (lambda i: i)
    add_prev = prev is not None
    in_specs = [pl.BlockSpec((1, 1, tm, e), lambda bb, i: (0, bb, pos(i), 0))]
    args = [proj]
    if add_prev:
        in_specs.append(pl.BlockSpec((1, tm, e), lambda bb, i: (bb, pos(i), 0)))
        args.append(prev)
    in_specs += [pl.BlockSpec((1, 1, e), lambda bb, i: (bb, 0, 0)),
                 pl.BlockSpec((ngroups, gw, 2 * gw), lambda bb, i: (0, 0, 0)),
                 pl.BlockSpec((1, e), lambda bb, i: (0, 0)),
                 pl.BlockSpec((1, e), lambda bb, i: (0, 0)),
                 pl.BlockSpec((1, e), lambda bb, i: (0, 0))]
    args += [h0, wg, 0.5 * b_r.reshape(1, e), 0.5 * b_i.reshape(1, e), lam.reshape(1, e)]
    out_specs, out_shape = [], []
    slabs = (e // LANES, tm + SUBLANES, LANES)
    scratch = [pltpu.VMEM(slabs, jnp.float32), pltpu.VMEM(slabs, jnp.float32),
               pltpu.VMEM((5, SUBLANES, e), jnp.float32)]
    out_alpha = None
    if out is not None:
        w_out, x, gate, ln_g, ln_b, out_alpha, col_major = out
        d = x.shape[-1]
        assert write_seq and tm == _tile_rows(l, col_major)
        main, _, _ = _stream_specs(l, d, tm, col_major, 0)
        tile = pl.BlockSpec(main.block_shape, lambda bb, i: main.index_map(bb, pos(i)))
        xv = _stream_view(x, col_major)
        in_specs += [pl.BlockSpec((1, 1, tm, e), lambda bb, i: (1, bb, pos(i), 0)),
                     pl.BlockSpec((e, d), lambda bb, i: (0, 0)),
                     tile,
                     pl.BlockSpec((1, 1, d), lambda bb, i: (bb, 0, 0)),
                     pl.BlockSpec((1, d), lambda bb, i: (0, 0)),
                     pl.BlockSpec((1, d), lambda bb, i: (0, 0))]
        args += [proj, w_out, xv, gate, ln_g.reshape(1, d), ln_b.reshape(1, d)]
        out_specs.append(tile)
        out_shape.append(jax.ShapeDtypeStruct(xv.shape, jnp.float32))
        scratch.append(pltpu.VMEM((tm, e), jnp.bfloat16))
    elif write_seq:
        out_specs.append(pl.BlockSpec((1, tm, e), lambda bb, i: (bb, pos(i), 0)))
        out_shape.append(jax.ShapeDtypeStruct((b, l, e), jnp.bfloat16))
    out_specs.append(pl.BlockSpec((1, 1, e), lambda bb, i: (bb, 0, 0)))
    out_shape.append(jax.ShapeDtypeStruct((b, 1, e), jnp.float32))
    kern = functools.partial(_rg_scan_kernel, reverse=reverse, add_prev=add_prev, write_seq=write_seq,
                             out_alpha=out_alpha, tm=tm, gw=gw, ngroups=ngroups)
    res = pl.pallas_call(
        kern,
        grid=(b, nchunk),
        in_specs=in_specs,
        out_specs=out_specs,
        out_shape=out_shape,
        scratch_shapes=scratch,
        compiler_params=_params("parallel", "arbitrary"),
        name="rg_scan_bwd" if reverse else "rg_scan_fwd",
    )(*args)
    if out is not None:
        return res[0].reshape(x.shape), res[1]
    if write_seq:
        return res[0], res[1]
    return None, res[0]


def _gate_weights(w_r, w_i):
    nb, bw, _ = w_r.shape
    g = RG_GATE_GROUP
    ng = nb // g
    eye = jnp.eye(g, dtype=w_r.dtype)

    def bd(w):
        w = w.reshape(ng, g, bw, bw)
        return jnp.einsum("nhij,hk->nhikj", w, eye).reshape(ng, g * bw, g * bw)

    return (0.5 * jnp.concatenate([bd(w_r), bd(w_i)], axis=-1)).astype(jnp.bfloat16)


def _outproj_tile(y, g, w_ref, x_ref, gate_ref, lg_ref, lb_ref, o_ref, alpha):
    d = w_ref.shape[1]
    rows = o_ref.shape[1]
    chunk = min(rows, OUT_ROWS)
    for j in range(o_ref.shape[2] // d):
        cols = slice(j * d, (j + 1) * d)
        for r0 in range(0, rows, chunk):
            tok = slice(j * rows + r0, j * rows + r0 + chunk)
            hg = 0.5 * g[tok].astype(jnp.float32)
            z = (y[tok].astype(jnp.float32) * (hg * (jnp.tanh(hg) + 1.0))).astype(jnp.bfloat16)
            out = jnp.dot(z, w_ref[...], preferred_element_type=jnp.float32)
            r = alpha * x_ref[0, r0:r0 + chunk, cols] + gate_ref[0] * out
            mu = jnp.mean(r, axis=-1, keepdims=True)
            cen = r - mu
            var = jnp.mean(cen * cen, axis=-1, keepdims=True)
            o_ref[0, r0:r0 + chunk, cols] = cen * lax.rsqrt(var + LN_EPS) * lg_ref[...] + lb_ref[...]


def _outproj_kernel(y_ref, g_ref, w_ref, x_ref, gate_ref, lg_ref, lb_ref, o_ref, *, alpha):
    _outproj_tile(y_ref[0], g_ref[0, 0], w_ref, x_ref, gate_ref, lg_ref, lb_ref, o_ref, alpha)


def _outproj(y, proj, g_group, w_bf16, x, gate, ln_g, ln_b, *, alpha, col_major):
    b, l, e = y.shape
    d = x.shape[-1]
    tm = _tile_rows(l, col_major)
    ntile = l // tm
    main, _, _ = _stream_specs(l, d, tm, col_major, 0)
    xv = _stream_view(x, col_major)
    out = pl.pallas_call(
        functools.partial(_outproj_kernel, alpha=alpha),
        grid=(b, ntile),
        in_specs=[pl.BlockSpec((1, tm, e), lambda bb, i: (bb, i, 0)),
                  pl.BlockSpec((1, 1, tm, e), lambda bb, i: (g_group, bb, i, 0)),
                  pl.BlockSpec((e, d), lambda bb, i: (0, 0)),
                  main,
                  pl.BlockSpec((1, 1, d), lambda bb, i: (bb, 0, 0)),
                  pl.BlockSpec((1, d), lambda bb, i: (0, 0)),
                  pl.BlockSpec((1, d), lambda bb, i: (0, 0))],
        out_specs=main,
        out_shape=jax.ShapeDtypeStruct(xv.shape, jnp.float32),
        compiler_params=_params("parallel", "parallel"),
        name="outproj_ln",
    )(y, proj, w_bf16, xv, gate, ln_g.reshape(1, d), ln_b.reshape(1, d))
    return out.reshape(x.shape)


def _filter_positions(l, fh):
    f32 = jnp.float32
    t = jnp.linspace(0.0, 1.0, l, dtype=f32)[:, None]
    bands = jnp.linspace(1e-4, HY_EMB_BANDS - 1, HY_EMB_BANDS, dtype=f32)
    w = (2.0 * math.pi) * jnp.arange(l, dtype=f32)[:, None] / l
    z = jnp.concatenate([t, jnp.cos(bands * w), -jnp.sin(bands * w)], axis=-1)
    z_rev = jnp.concatenate([z[:1], z[:0:-1]], axis=0)
    pad = lambda a: jnp.pad(a, ((0, 0), (0, fh - a.shape[1])))
    return jnp.concatenate([pad(z), pad(z_rev)], axis=1), jnp.stack([z[:, 0:1], z_rev[:, 0:1]])


def _hidden_kernel(z_ref, w1_ref, b1_ref, w2_ref, b2_ref, fr_ref, o_ref):
    fr = fr_ref[...]
    h = jnp.dot(z_ref[...], w1_ref[...], preferred_element_type=jnp.float32, precision=_HIGHEST)
    h = jnp.sin(fr * (h + b1_ref[...]))
    h = jnp.dot(h, w2_ref[...], preferred_element_type=jnp.float32, precision=_HIGHEST)
    o_ref[...] = jnp.sin(fr * (h + b2_ref[...]))


def _filter_hidden(zz, w1, b1, w2, b2, freq):
    n, zp = zz.shape
    fh = w1.shape[1]
    tr = min(n, 1024)
    both = lambda v: jnp.concatenate([v, v]).reshape(1, 2 * fh)
    blockdiag = lambda w: jnp.kron(jnp.eye(2, dtype=w.dtype), w)
    w1p = blockdiag(jnp.pad(w1, ((0, fh - w1.shape[0]), (0, 0))))
    full = lambda shape: pl.BlockSpec(shape, lambda i: (0,) * len(shape))
    return pl.pallas_call(
        _hidden_kernel,
        grid=(n // tr,),
        in_specs=[pl.BlockSpec((tr, zp), lambda i: (i, 0)), full((zp, zp)), full((1, zp)),
                  full((zp, zp)), full((1, zp)), full((1, zp))],
        out_specs=pl.BlockSpec((tr, zp), lambda i: (i, 0)),
        out_shape=jax.ShapeDtypeStruct((n, zp), jnp.float32),
        compiler_params=_params("parallel"),
        name="filter_hidden",
    )(zz, w1p, both(b1), blockdiag(w2), both(b2), both(freq))


def _split_bf16(a):
    hi = a.astype(jnp.bfloat16)
    return hi, (a - hi.astype(jnp.float32)).astype(jnp.bfloat16)


def _filter_kernel(hid_ref, t_ref, w3_ref, ad_ref, k_ref, s_ref, *, tr, n_total):
    d = pl.program_id(1)
    i = pl.program_id(2)
    last = (d == pl.num_programs(1) - 1) & (i == pl.num_programs(2) - 1)
    hi, lo = _split_bf16(hid_ref[...])
    h = jnp.dot(jnp.concatenate([hi, lo, hi], axis=1), w3_ref[0, 0], preferred_element_type=jnp.float32)
    window = jnp.exp(-t_ref[0] * ad_ref[...]) + HY_MOD_SHIFT
    row = lax.broadcasted_iota(jnp.int32, h.shape, 0)
    gap = (row == 0) & (d == 1) & (i == 0)
    k = jnp.where(gap, 0.0, h * window)
    k_ref[0] = k

    @pl.when((d == 0) & (i == 0))
    def _():
        s_ref[...] = jnp.zeros_like(s_ref)

    s_ref[0] += jnp.sum(jnp.abs(k).reshape(tr // SUBLANES, SUBLANES, -1), axis=0)

    @pl.when(last)
    def _():
        tot = jnp.sum(s_ref[0], axis=0, keepdims=True)
        s_ref[0] = jnp.broadcast_to(1.0 / (tot * n_total), s_ref.shape[1:])


def _filter_weights(w3, fh, e, compact):
    order = w3.shape[1] // (2 * e)
    max_decay = math.log(HY_DECAY_TARGET) / HY_FAST_DECAY_PCT
    min_decay = math.log(HY_DECAY_TARGET) / HY_SLOW_DECAY_PCT
    absdelta = jnp.abs(jnp.linspace(min_decay, max_decay, e, dtype=jnp.float32)).reshape(1, e)
    w3 = w3.reshape(fh, order, 2, e).transpose(1, 2, 0, 3)
    if compact:
        w_hi, w_lo = _split_bf16(w3)
        return jnp.concatenate([w_hi, w_hi, w_lo, jnp.zeros_like(w_hi)], axis=2), absdelta
    w3 = jnp.stack([jnp.pad(w3[:, 0], ((0, 0), (0, fh), (0, 0))),
                    jnp.pad(w3[:, 1], ((0, 0), (fh, 0), (0, 0)))], axis=1)
    w_hi, w_lo = _split_bf16(w3)
    return jnp.concatenate([w_hi, w_hi, w_lo], axis=2), absdelta


def _filter_fft_a_kernel(hid_ref, t_ref, w3_ref, ad_ref, m_ref, re_ref, im_ref, s_ref, *, n_total):
    j = pl.program_id(2)
    k1h, rows, fh2 = hid_ref.shape
    ec = ad_ref.shape[1]
    hi, lo = _split_bf16(hid_ref[...].reshape(k1h * rows, fh2))
    fh = fh2 // 2
    stacked = [jnp.concatenate([hi[:, own], lo[:, own], hi[:, own], jnp.zeros_like(hi[:, own])], axis=1)
               for own in (slice(0, fh), slice(fh, fh2))]
    tcol = [t_ref[d].reshape(k1h * rows, 1) for d in range(2)]
    half = m_ref.shape[0] // 2
    f1p = half // FFT_J

    @pl.when(j == 0)
    def _():
        s_ref[...] = jnp.zeros_like(s_ref)

    abs_sums = []
    for sl in _lane_chunks(ec):
        halves = []
        for d in range(2):
            h = jnp.dot(stacked[d], w3_ref[0, d, :, sl], preferred_element_type=jnp.float32)
            k = h * (jnp.exp(-tcol[d] * ad_ref[:, sl]) + HY_MOD_SHIFT)
            if d == 1:
                row = lax.broadcasted_iota(jnp.int32, k.shape, 0)
                k = jnp.where((row == 0) & (j == 0), 0.0, k)
            halves.append(k)
        k = jnp.concatenate(halves, axis=0)
        width = k.shape[1]
        abs_sums.append(jnp.sum(jnp.abs(k).reshape(-1, SUBLANES, width), axis=0))
        k3 = k.reshape(2 * k1h, rows, width)
        re_parts, im_parts = [], []
        for h in range(rows // FFT_J):
            x = k3[:, h * FFT_J:(h + 1) * FFT_J, :].reshape(2 * k1h * FFT_J, width)
            r = jnp.dot(m_ref[...], x.astype(jnp.bfloat16), preferred_element_type=jnp.float32)
            re_parts.append(r[:half].reshape(f1p, FFT_J, width))
            im_parts.append(r[half:].reshape(f1p, FFT_J, width))
        re_ref[0, :, :, sl] = jnp.concatenate(re_parts, axis=1).astype(jnp.bfloat16)
        im_ref[0, :, :, sl] = jnp.concatenate(im_parts, axis=1).astype(jnp.bfloat16)
    s_ref[0] += jnp.concatenate(abs_sums, axis=1)

    @pl.when(j == pl.num_programs(2) - 1)
    def _():
        tot = jnp.sum(s_ref[0], axis=0, keepdims=True)
        s_ref[0] = jnp.broadcast_to(1.0 / (tot * n_total), s_ref.shape[1:])


def _filter_fft_a(hid, tcol, w3, e, tabs):
    l, fh2 = hid.shape
    n2 = FFT_N2
    k1h = l // n2
    f1p = tabs["f1p"]
    mat = tabs["a_full"]
    w_stack, absdelta = _filter_weights(w3, fh2 // 2, e, compact=True)
    order = w_stack.shape[0]
    ec = min(e, FFT_EC)
    out_spec = pl.BlockSpec((1, f1p, BF16_ROWS, ec), lambda o, cc, j: (o, 0, j, cc))
    shape = jax.ShapeDtypeStruct((order, f1p, n2, e), jnp.bfloat16)
    return pl.pallas_call(
        functools.partial(_filter_fft_a_kernel, n_total=float(2 * l)),
        grid=(order, e // ec, n2 // BF16_ROWS),
        in_specs=[pl.BlockSpec((k1h, BF16_ROWS, fh2), lambda o, cc, j: (0, j, 0)),
                  pl.BlockSpec((2, k1h, BF16_ROWS, 1), lambda o, cc, j: (0, 0, j, 0)),
                  pl.BlockSpec((1, 2, w_stack.shape[2], ec), lambda o, cc, j: (o, 0, 0, cc)),
                  pl.BlockSpec((1, ec), lambda o, cc, j: (0, cc)),
                  pl.BlockSpec(mat.shape, lambda o, cc, j: (0, 0))],
        out_specs=[out_spec, out_spec,
                   pl.BlockSpec((1, SUBLANES, ec), lambda o, cc, j: (o, 0, cc))],
        out_shape=[shape, shape, jax.ShapeDtypeStruct((order, SUBLANES, e), jnp.float32)],
        compiler_params=_params("parallel", "parallel", "arbitrary"),
        name="filter_fft_a",
    )(hid.reshape(k1h, n2, fh2), tcol.reshape(2, k1h, n2, 1), w_stack, absdelta, mat)


def _filters(hid, tcol, w3, e):
    l, fh2 = hid.shape
    n = 2 * l
    tr = min(l, 512)
    nl = l // tr
    w_stack, absdelta = _filter_weights(w3, fh2 // 2, e, compact=False)
    order = w_stack.shape[0]
    return pl.pallas_call(
        functools.partial(_filter_kernel, tr=tr, n_total=float(n)),
        grid=(order, 2, nl),
        in_specs=[pl.BlockSpec((tr, fh2), lambda o, d, i: (i, 0)),
                  pl.BlockSpec((1, tr, 1), lambda o, d, i: (d, i, 0)),
                  pl.BlockSpec((1, 1, 3 * fh2, e), lambda o, d, i: (o, d, 0, 0)),
                  pl.BlockSpec((1, e), lambda o, d, i: (0, 0))],
        out_specs=[pl.BlockSpec((1, tr, e), lambda o, d, i: (o, d * nl + i, 0)),
                   pl.BlockSpec((1, SUBLANES, e), lambda o, d, i: (o, 0, 0))],
        out_shape=[jax.ShapeDtypeStruct((order, n, e), jnp.float32),
                   jax.ShapeDtypeStruct((order, SUBLANES, e), jnp.float32)],
        compiler_params=_params("parallel", "arbitrary", "arbitrary"),
        name="filter_gen",
    )(hid, tcol, w_stack, absdelta)


def _angles(num, den):
    ang = 2.0 * np.pi * (np.asarray(num, np.int64) % den).astype(np.float64) / den
    return np.cos(ang), np.sin(ang)


@functools.lru_cache(maxsize=None)
def _fft_tables(n1):
    n2 = FFT_N2
    n = n1 * n2
    f1 = n1 // 2 + 1
    f1p = _round_up(f1, 2)
    fr = np.arange(f1)[:, None]
    eye = np.eye(FFT_J)

    def stage_a(k1):
        c, s = _angles(fr * np.arange(k1)[None, :], n1)
        m = np.zeros((2 * f1p, k1))
        m[:f1], m[f1p:f1p + f1] = c, -s
        return np.kron(m, eye)

    nyq = n1 // 2
    wgt = np.full((nyq,), 2.0)
    wgt[0] = 1.0
    c, s = _angles(np.arange(n1 // 2)[:, None] * np.arange(nyq)[None, :], n1)
    m_c = np.concatenate([c * wgt, -s * wgt], axis=1)

    c2, s2 = _angles(np.arange(n2)[:, None] * np.arange(n2)[None, :], n2)
    cphi, sphi = _angles(fr * np.arange(n2)[None, :], n)
    phase = np.stack([cphi, sphi], axis=1)
    bf = lambda a: jnp.asarray(a, jnp.float32).astype(jnp.bfloat16)
    f32 = lambda a: jnp.asarray(a, jnp.float32)
    return dict(f1=f1, f1p=f1p, a_data=bf(stage_a(n1 // 2)), a_full=bf(stage_a(n1)),
                m_c=bf(np.kron(m_c, eye)), base_c=f32(c2), base_s=f32(s2),
                phase_row=f32(phase[:, :, None, :]), phase_col=f32(phase[..., None]))


def _twiddled_dft_kernel(c_ref, s_ref, row_ref, col_ref, mf_ref, mi_ref):
    c, s = c_ref[...], s_ref[...]
    n2 = c.shape[0]
    bf = jnp.bfloat16
    cr, sr = row_ref[0, 0], row_ref[0, 1]
    cf, sf = (c * cr - s * sr).astype(bf), (s * cr + c * sr).astype(bf)
    mf_ref[0, :n2, :n2] = cf
    mf_ref[0, :n2, n2:] = sf
    mf_ref[0, n2:, :n2] = -sf
    mf_ref[0, n2:, n2:] = cf
    cc, sc = col_ref[0, 0], col_ref[0, 1]
    ci, si = (c * cc - s * sc).astype(bf), (s * cc + c * sc).astype(bf)
    mi_ref[0, :n2, :n2] = ci
    mi_ref[0, :n2, n2:] = -si
    mi_ref[0, n2:, :n2] = si
    mi_ref[0, n2:, n2:] = ci


def _twiddled_dfts(tabs):
    f1 = tabs["f1"]
    n2 = FFT_N2
    shape = jax.ShapeDtypeStruct((f1, 2 * n2, 2 * n2), jnp.bfloat16)
    base = pl.BlockSpec((n2, n2), lambda f: (0, 0))
    out = pl.BlockSpec((1, 2 * n2, 2 * n2), lambda f: (f, 0, 0))
    return pl.pallas_call(
        _twiddled_dft_kernel,
        grid=(f1,),
        in_specs=[base, base, pl.BlockSpec((1, 2, 1, n2), lambda f: (f, 0, 0, 0)),
                  pl.BlockSpec((1, 2, n2, 1), lambda f: (f, 0, 0, 0))],
        out_specs=[out, out],
        out_shape=[shape, shape],
        compiler_params=_params("parallel"),
        name="twiddled_dft_tables",
    )(tabs["base_c"], tabs["base_s"], tabs["phase_row"], tabs["phase_col"])


def _fft_a_kernel(x_ref, m_ref, re_ref, im_ref):
    k1, ec = x_ref.shape[2], x_ref.shape[4]
    half = m_ref.shape[0] // 2
    f1p = half // FFT_J
    re_parts, im_parts = [], []
    xf = x_ref[0, 0].astype(jnp.float32)
    for h in range(x_ref.shape[3] // FFT_J):
        x = xf[:, h * FFT_J:(h + 1) * FFT_J, :].reshape(k1 * FFT_J, ec)
        r = jnp.dot(m_ref[...], x.astype(jnp.bfloat16), preferred_element_type=jnp.float32)
        re_parts.append(r[:half].reshape(f1p, FFT_J, ec))
        im_parts.append(r[half:].reshape(f1p, FFT_J, ec))
    re_ref[0] = jnp.concatenate(re_parts, axis=1).astype(jnp.bfloat16)
    im_ref[0] = jnp.concatenate(im_parts, axis=1).astype(jnp.bfloat16)


def _fft_stage_a(arr, group, tabs):
    _, bt, rows, e = arr.shape
    k1 = rows // FFT_N2
    f1p = tabs["f1p"]
    mat = tabs["a_data"]
    assert mat.shape[1] == k1 * FFT_J
    ec = min(e, FFT_EC)
    av = arr.reshape(arr.shape[0], bt, k1, FFT_N2, e)
    out_spec = pl.BlockSpec((1, f1p, BF16_ROWS, ec), lambda bb, j, cc: (bb, 0, j, cc))
    shape = jax.ShapeDtypeStruct((bt, f1p, FFT_N2, e), jnp.bfloat16)
    return pl.pallas_call(
        _fft_a_kernel,
        grid=(bt, FFT_N2 // BF16_ROWS, e // ec),
        in_specs=[pl.BlockSpec((1, 1, k1, BF16_ROWS, ec), lambda bb, j, cc: (group, bb, 0, j, cc)),
                  pl.BlockSpec(mat.shape, lambda bb, j, cc: (0, 0))],
        out_specs=[out_spec, out_spec],
        out_shape=[shape, shape],
        compiler_params=_params("parallel", "parallel", "parallel"),
        name="fft_stage_a",
    )(av, mat)


def _lane_chunks(e, width=256):
    return [slice(c0, min(c0 + width, e)) for c0 in range(0, e, width)]


def _fft_b_filter_kernel(re_ref, im_ref, sc_ref, mf_ref, kre_ref, kim_ref):
    for b in range(re_ref.shape[0]):
        for sl in _lane_chunks(re_ref.shape[3]):
            x = jnp.concatenate([re_ref[b, 0, :, sl], im_ref[b, 0, :, sl]], axis=0)
            u = jnp.dot(mf_ref[0], x, preferred_element_type=jnp.float32) * sc_ref[b, 0:1, sl]
            kre_ref[b, 0, :, sl] = u[:FFT_N2].astype(kre_ref.dtype)
            kim_ref[b, 0, :, sl] = u[FFT_N2:].astype(kim_ref.dtype)


def _fft_b_conv_kernel(re_ref, im_ref, kre_ref, kim_ref, mf_ref, mi_ref, ore_ref, oim_ref):
    for sl in _lane_chunks(re_ref.shape[3]):
        k_re = kre_ref[0, 0, :, sl].astype(jnp.float32)
        k_im = kim_ref[0, 0, :, sl].astype(jnp.float32)
        for b in range(re_ref.shape[0]):
            x = jnp.concatenate([re_ref[b, 0, :, sl], im_ref[b, 0, :, sl]], axis=0)
            u = jnp.dot(mf_ref[0], x, preferred_element_type=jnp.float32)
            u_re, u_im = u[:FFT_N2], u[FFT_N2:]
            p = jnp.concatenate([u_re * k_re - u_im * k_im, u_re * k_im + u_im * k_re], axis=0)
            v = jnp.dot(mi_ref[0], p.astype(jnp.bfloat16), preferred_element_type=jnp.float32)
            ore_ref[b, 0, :, sl] = v[:FFT_N2].astype(jnp.bfloat16)
            oim_ref[b, 0, :, sl] = v[FFT_N2:].astype(jnp.bfloat16)


def _fft_stage_b_filter(a_re, a_im, inv_s, mf, tabs):
    bt, _, n2, e = a_re.shape
    f1 = tabs["f1"]
    blk = pl.BlockSpec((bt, 1, n2, e), lambda f: (0, f, 0, 0))
    shape = jax.ShapeDtypeStruct((bt, f1, n2, e), jnp.bfloat16)
    return pl.pallas_call(
        _fft_b_filter_kernel,
        grid=(f1,),
        in_specs=[blk, blk, pl.BlockSpec(inv_s.shape, lambda f: (0, 0, 0)),
                  pl.BlockSpec((1, 2 * n2, 2 * n2), lambda f: (f, 0, 0))],
        out_specs=[blk, blk],
        out_shape=[shape, shape],
        compiler_params=_params("parallel"),
        name="fft_stage_b_filter",
    )(a_re, a_im, inv_s, mf)


def _fft_stage_b_conv(a_re, a_im, kf_re, kf_im, order_idx, mf, mi, tabs):
    bt, f1p, n2, e = a_re.shape
    f1 = tabs["f1"]
    fc = lambda f: jnp.minimum(f, f1 - 1)
    blk = pl.BlockSpec((bt, 1, n2, e), lambda f: (0, f, 0, 0))
    kblk = pl.BlockSpec((1, 1, n2, e), lambda f: (order_idx, fc(f), 0, 0))
    wblk = pl.BlockSpec((1, 2 * n2, 2 * n2), lambda f: (fc(f), 0, 0))
    shape = jax.ShapeDtypeStruct((bt, f1p, n2, e), jnp.bfloat16)
    return pl.pallas_call(
        _fft_b_conv_kernel,
        grid=(f1p,),
        in_specs=[blk, blk, kblk, kblk, wblk, wblk],
        out_specs=[blk, blk],
        out_shape=[shape, shape],
        compiler_params=_params("parallel"),
        name="fft_stage_b_conv",
    )(a_re, a_im, kf_re, kf_im, mf, mi)


def _fft_c_kernel(*refs, chain):
    re_ref, im_ref, m_ref, u_ref, mul_ref, d_ref = refs[:6]
    if chain:
        ma_ref, o_ref, are_ref, aim_ref = refs[6:]
    else:
        o_ref, = refs[6:]
    k1, ec = o_ref.shape[1], o_ref.shape[3]
    f1p = re_ref.shape[1]
    nyq = m_ref.shape[1] // (2 * FFT_J)
    re = re_ref[0].astype(jnp.float32)
    im = im_ref[0].astype(jnp.float32)
    uf = u_ref[0, 0].astype(jnp.float32)
    mf = mul_ref[0, 0].astype(jnp.float32)
    parts, re_parts, im_parts = [], [], []
    for h in range(o_ref.shape[2] // FFT_J):
        sl = slice(h * FFT_J, (h + 1) * FFT_J)
        x = jnp.concatenate([re[:nyq, sl, :].reshape(nyq * FFT_J, ec),
                             im[:nyq, sl, :].reshape(nyq * FFT_J, ec)], axis=0)
        y = jnp.dot(m_ref[...], x.astype(jnp.bfloat16), preferred_element_type=jnp.float32)
        r_nyq = re[nyq, sl, :]
        y = (y.reshape(k1 // 2, 2, FFT_J, ec) + jnp.stack([r_nyq, -r_nyq])[None]).reshape(k1 * FFT_J, ec)
        u = uf[:, sl, :].reshape(k1 * FFT_J, ec)
        mul = mf[:, sl, :].reshape(k1 * FFT_J, ec)
        z = (mul * (y + u * d_ref[...])).astype(jnp.bfloat16)
        parts.append(z.astype(jnp.float32).reshape(k1, FFT_J, ec))
        if chain:
            r = jnp.dot(ma_ref[...], z, preferred_element_type=jnp.float32)
            half = ma_ref.shape[0] // 2
            re_parts.append(r[:half].reshape(f1p, FFT_J, ec))
            im_parts.append(r[half:].reshape(f1p, FFT_J, ec))
    o_ref[0] = jnp.concatenate(parts, axis=1).astype(o_ref.dtype)
    if chain:
        are_ref[0] = jnp.concatenate(re_parts, axis=1).astype(jnp.bfloat16)
        aim_ref[0] = jnp.concatenate(im_parts, axis=1).astype(jnp.bfloat16)


def _fft_stage_c(b_re, b_im, u_arr, u_group, proj, mul_group, d_row, tabs, chain):
    bt, f1p, n2, e = b_re.shape
    l = u_arr.shape[2]
    k1 = l // n2
    ec = min(e, FFT_EC)
    uv = u_arr.reshape(u_arr.shape[0], bt, k1, n2, e)
    pv = proj.reshape(proj.shape[0], bt, k1, n2, e)
    mat = tabs["m_c"]
    fblk = pl.BlockSpec((1, f1p, BF16_ROWS, ec), lambda bb, j, cc: (bb, 0, j, cc))
    full = lambda m: pl.BlockSpec(m.shape, lambda bb, j, cc: (0, 0))
    in_specs = [fblk, fblk, full(mat),
                pl.BlockSpec((1, 1, k1, BF16_ROWS, ec), lambda bb, j, cc: (u_group, bb, 0, j, cc)),
                pl.BlockSpec((1, 1, k1, BF16_ROWS, ec), lambda bb, j, cc: (mul_group, bb, 0, j, cc)),
                pl.BlockSpec((1, ec), lambda bb, j, cc: (0, cc))]
    args = [b_re, b_im, mat, uv, pv, d_row]
    out_specs = [pl.BlockSpec((1, k1, BF16_ROWS, ec), lambda bb, j, cc: (bb, 0, j, cc))]
    out_shape = [jax.ShapeDtypeStruct((bt, k1, n2, e), jnp.bfloat16)]
    if chain:
        in_specs.append(full(tabs["a_data"]))
        args.append(tabs["a_data"])
        out_specs += [fblk, fblk]
        out_shape += [jax.ShapeDtypeStruct((bt, f1p, n2, e), jnp.bfloat16)] * 2
    res = pl.pallas_call(
        functools.partial(_fft_c_kernel, chain=chain),
        grid=(bt, n2 // BF16_ROWS, e // ec),
        in_specs=in_specs,
        out_specs=out_specs,
        out_shape=out_shape,
        compiler_params=_params("parallel", "parallel", "parallel"),
        name="fft_stage_ca" if chain else "fft_stage_c",
    )(*args)
    return (res[0].reshape(bt, l, e),) + tuple(res[1:])


def _hyena_long(proj, f_re, f_im, inv_s, d_bias, tabs, mf, mi):
    order, e = d_bias.shape
    kf_re, kf_im = _fft_stage_b_filter(f_re, f_im, inv_s, mf, tabs)
    u_arr, u_group = proj, 0
    a_re, a_im = _fft_stage_a(u_arr, u_group, tabs)
    for o in range(order):
        b_re, b_im = _fft_stage_b_conv(a_re, a_im, kf_re, kf_im, o, mf, mi, tabs)
        chain = o + 1 < order
        res = _fft_stage_c(b_re, b_im, u_arr, u_group, proj, o + 1, d_bias[o].reshape(1, e), tabs, chain)
        if chain:
            a_re, a_im = res[1:]
        u_arr, u_group = res[0][None], 0
    return res[0]


@functools.lru_cache(maxsize=None)
def _dft_tables(l):
    n = 2 * l
    nf = l + 1
    nfp = _round_up(nf, BF16_ROWS)
    c, s = _angles(np.arange(nf)[:, None] * np.arange(n)[None, :], n)
    fwd = np.zeros((2 * nfp, n))
    fwd[:nf], fwd[nfp:nfp + nf] = c, -s
    wgt = np.full((nf,), 2.0)
    wgt[0] = wgt[-1] = 1.0
    inv = np.zeros((l, 2 * nfp))
    inv[:, :nf], inv[:, nfp:nfp + nf] = (c[:, :l] * wgt[:, None]).T, (-s[:, :l] * wgt[:, None]).T
    bf = lambda a: jnp.asarray(a, jnp.float32).astype(jnp.bfloat16)
    return nfp, bf(fwd), bf(inv)


def _hyena_short_kernel(p_ref, k_ref, s_ref, d_ref, fwd_ref, inv_ref, o_ref, *, l, nfp, order):
    fwd = fwd_ref[...]
    u = p_ref[0, 0]
    for o in range(order):
        kn = (k_ref[o] * s_ref[o, 0:1, :]).astype(jnp.bfloat16)
        kf = jnp.dot(fwd, kn, preferred_element_type=jnp.float32)
        uf = jnp.dot(fwd[:, :l], u.astype(jnp.bfloat16), preferred_element_type=jnp.float32)
        k_re, k_im, u_re, u_im = kf[:nfp], kf[nfp:], uf[:nfp], uf[nfp:]
        p = jnp.concatenate([u_re * k_re - u_im * k_im, u_re * k_im + u_im * k_re], axis=0)
        y = jnp.dot(inv_ref[...], p.astype(jnp.bfloat16), preferred_element_type=jnp.float32)
        u = p_ref[o + 1, 0] * (y + u * d_ref[o:o + 1, :])
    o_ref[0] = u


def _hyena_short(proj, k, inv_s, d_bias, e):
    nt, b, l, _ = proj.shape
    order = k.shape[0]
    nfp, fwd, inv = _dft_tables(l)
    return pl.pallas_call(
        functools.partial(_hyena_short_kernel, l=l, nfp=nfp, order=order),
        grid=(b,),
        in_specs=[pl.BlockSpec((nt, 1, l, e), lambda bb: (0, bb, 0, 0)),
                  pl.BlockSpec((order, 2 * l, e), lambda bb: (0, 0, 0)),
                  pl.BlockSpec((order, SUBLANES, e), lambda bb: (0, 0, 0)),
                  pl.BlockSpec((order, e), lambda bb: (0, 0)),
                  pl.BlockSpec(fwd.shape, lambda bb: (0, 0)),
                  pl.BlockSpec(inv.shape, lambda bb: (0, 0))],
        out_specs=pl.BlockSpec((1, l, e), lambda bb: (bb, 0, 0)),
        out_shape=jax.ShapeDtypeStruct((b, l, e), jnp.float32),
        compiler_params=_params("parallel"),
        name="hyena_short",
    )(proj, k, inv_s, d_bias, fwd, inv)


def kernel(x, c, ctx, c_ctx, w_mod, b_mod, ln_g, ln_b, rg_w_in, rg_conv_w, rg_conv_b, rg_w_r, rg_b_r, rg_w_i, rg_b_i, rg_lambda, rg_w_out, hy_w_in, hy_conv_w, hy_conv_b, hy_f_w1, hy_f_b1, hy_f_w2, hy_f_b2, hy_f_w3, hy_f_freq, hy_d, hy_w_out):
    bsz, seq, d = x.shape
    depth = w_mod.shape[0]
    e = rg_w_out.shape[1]
    alpha = (2 * depth) ** 0.25
    assert bsz + 1 <= SUBLANES

    cvec = jnp.zeros((SUBLANES, d), jnp.float32).at[:bsz].set(c).at[bsz].set(c_ctx)
    mod = _modulation(cvec, w_mod, b_mod)

    def mod_rows(layer, part, for_ctx):
        v = mod[layer, :, part * d:(part + 1) * d]
        v = jnp.broadcast_to(v[bsz], (bsz, d)) if for_ctx else v[:bsz]
        return v.reshape(bsz, 1, d)

    fft = None

    def filter_hidden(occ, l):
        zz, tcol = _filter_positions(l, hy_f_w1.shape[-1])
        hid = _filter_hidden(zz, hy_f_w1[occ], hy_f_b1[occ], hy_f_w2[occ], hy_f_b2[occ], hy_f_freq[occ])
        return hid, tcol

    for i in range(depth):
        kind = i % N_MIXERS
        occ = i // N_MIXERS
        need_ctx_out = any(j % N_MIXERS == 0 for j in range(i + 1, depth))
        col_major = occ % 2 == 1
        use_ctx = kind == 0 or need_ctx_out
        shift, scale, gate = (mod_rows(i, p, False) for p in range(3))
        if use_ctx:
            shift_c, scale_c, gate_c = (mod_rows(i, p, True) for p in range(3))

        if kind == 0:
            w_in = rg_w_in[occ].astype(jnp.bfloat16)
            w_out = rg_w_out[occ].astype(jnp.bfloat16)
            proj = _inproj(x, shift, scale, w_in, rg_conv_w[occ], rg_conv_b[occ],
                           e=e, pad_l=1, col_major=col_major)
            w_in_c = w_in if need_ctx_out else w_in[:, :e]
            proj_c = _inproj(ctx, shift_c, scale_c, w_in_c, rg_conv_w[occ], rg_conv_b[occ],
                             e=e, pad_l=1, col_major=False)
            zeros = jnp.zeros((bsz, 1, e), jnp.float32)
            gates = [(_gate_weights(rg_w_r[occ, dd], rg_w_i[occ, dd]), rg_b_r[occ, dd], rg_b_i[occ, dd],
                      rg_lambda[occ, dd]) for dd in range(2)]
            hc_f, st_f = _rg_scan(proj_c, zeros, *gates[0], reverse=False, write_seq=need_ctx_out)
            h_f, _ = _rg_scan(proj, st_f, *gates[0], reverse=False)
            y_c, st_b = _rg_scan(proj_c, zeros, *gates[1], reverse=True, prev=hc_f, write_seq=need_ctx_out)
            x_new, _ = _rg_scan(proj, st_b, *gates[1], reverse=True, prev=h_f,
                                out=(w_out, x, gate, ln_g[i], ln_b[i], alpha, col_major))
            g_group = 1
        else:
            w_in = hy_w_in[occ].astype(jnp.bfloat16)
            w_out = hy_w_out[occ].astype(jnp.bfloat16)
            proj = _inproj(x, shift, scale, w_in, hy_conv_w[occ], hy_conv_b[occ],
                           e=e, pad_l=1, col_major=col_major)
            if fft is None:
                tabs = _fft_tables(2 * seq // FFT_N2)
                fft = (tabs,) + tuple(_twiddled_dfts(tabs))
            f_re, f_im, inv_s = _filter_fft_a(*filter_hidden(occ, seq), hy_f_w3[occ], e, fft[0])
            y = _hyena_long(proj, f_re, f_im, inv_s, hy_d[occ], *fft)
            if need_ctx_out:
                proj_c = _inproj(ctx, shift_c, scale_c, w_in, hy_conv_w[occ], hy_conv_b[occ],
                                 e=e, pad_l=1, col_major=False)
                k_c, inv_s_c = _filters(*filter_hidden(occ, ctx.shape[1]), hy_f_w3[occ], e)
                y_c = _hyena_short(proj_c, k_c, inv_s_c, hy_d[occ], e)
            g_group = 3
            x_new = _outproj(y, proj, g_group, w_out, x, gate, ln_g[i], ln_b[i],
                             alpha=alpha, col_major=col_major)

        x = x_new
        if need_ctx_out:
            ctx = _outproj(y_c, proj_c, g_group, w_out, ctx, gate_c, ln_g[i], ln_b[i],
                           alpha=alpha, col_major=False)
    return x
```
